```python
import jax, jax.numpy as jnp
from jax import lax
import numpy as np

D_MODEL = 2048
BATCH = 8
SEQ = 8192
DEPTH = 4

GRID_W = 64
MLA_HEADS = 8
Q_LORA = 512
KV_LORA = 512
QK_NOPE = 128
QK_ROPE = 64
V_HEAD = 128
ROPE_THETA = 10000.0
Q_BLOCK = 128
NA_HEADS = 8
NA_HEAD_DIM = 128
NA_KH = 8
NA_KW = 16
D_FF = 4 * D_MODEL
EPS = 1e-6

MLA_W = MLA_HEADS * V_HEAD
NA_W = NA_HEADS * NA_HEAD_DIM
IN_WIDTHS = (Q_LORA, KV_LORA, QK_ROPE, NA_W, NA_W, NA_W, D_MODEL, D_MODEL)
IN_SPLITS = tuple(int(v) for v in np.cumsum(IN_WIDTHS)[:-1])
IN_TOTAL = int(sum(IN_WIDTHS))

kernel_name = "hybrid_mla_natten_sqrelu_encoder"


def rmsnorm(x, g):
    xf = x.astype(jnp.float32)
    y = xf * lax.rsqrt(jnp.mean(xf * xf, axis=-1, keepdims=True) + EPS)
    return (y * g.astype(jnp.float32)).astype(x.dtype)


def rope(x, cos, sin):
    x1, x2 = jnp.split(x, 2, axis=-1)
    return jnp.concatenate([x1 * cos - x2 * sin, x2 * cos + x1 * sin], axis=-1)


def mla_attention(q_nope, q_pe, k_nope, k_pe, v):
    B, S, H, _ = q_nope.shape
    nblk = S // Q_BLOCK
    scale = (QK_NOPE + QK_ROPE) ** -0.5
    qn_b = q_nope.reshape(B, nblk, Q_BLOCK, H, QK_NOPE).transpose(1, 0, 2, 3, 4)
    qp_b = q_pe.reshape(B, nblk, Q_BLOCK, H, QK_ROPE).transpose(1, 0, 2, 3, 4)

    def block(args):
        qn, qp = args
        s = (jnp.einsum('bqhd,bkhd->bhqk', qn, k_nope)
             + jnp.einsum('bqhd,bkd->bhqk', qp, k_pe)).astype(jnp.float32) * scale
        p = jax.nn.softmax(s, axis=-1).astype(v.dtype)
        return jnp.einsum('bhqk,bkhd->bqhd', p, v)

    out = lax.map(block, (qn_b, qp_b))
    return out.transpose(1, 0, 2, 3, 4).reshape(B, S, H * V_HEAD)


def neighbourhood_attention(q, k, v, rpb):
    B, S, H, d = q.shape
    rows = S // GRID_W
    kh = min(NA_KH, rows)
    r = jnp.arange(rows)
    row_start = jnp.clip(r - kh // 2, 0, rows - kh)
    row_idx = row_start[:, None] + jnp.arange(kh)[None, :]
    c = jnp.arange(GRID_W)
    col_start = jnp.clip(c - NA_KW // 2, 0, GRID_W - NA_KW)
    col_ok = (c[None, :] >= col_start[:, None]) & (c[None, :] < col_start[:, None] + NA_KW)

    qg = q.reshape(B, rows, GRID_W, H, d)
    kg = k.reshape(B, rows, GRID_W, H, d)[:, row_idx]
    vg = v.reshape(B, rows, GRID_W, H, d)[:, row_idx]
    s = jnp.einsum('brqhd,brikhd->brhqik', qg, kg).astype(jnp.float32) * (d ** -0.5)

    dy = row_idx - r[:, None] + (NA_KH - 1)
    dx = jnp.clip(c[None, :] - c[:, None], -(NA_KW - 1), NA_KW - 1) + (NA_KW - 1)
    bias = rpb.astype(jnp.float32)[:, dy][..., dx]
    bias = bias.transpose(1, 0, 3, 2, 4)
    s = jnp.where(col_ok[:, None, :], s + bias[None], -jnp.inf)
    p = jax.nn.softmax(s.reshape(B, rows, H, GRID_W, kh * GRID_W), axis=-1)
    p = p.reshape(B, rows, H, GRID_W, kh, GRID_W).astype(v.dtype)
    out = jnp.einsum('brhqik,brikhd->brqhd', p, vg)
    return out.reshape(B, S, H * d)


def _fwd_setup_inputs(seed: int = 0) -> dict:
    key = jax.random.key(seed)
    ks = jax.random.split(key, 16)
    f32 = jnp.float32

    def w(k, shape, fan_in):
        return jax.random.normal(k, shape, f32) * fan_in ** -0.5

    def gain(k, shape):
        return 1.0 + 0.01 * jax.random.normal(k, shape, f32)

    return {
        "x": jax.random.normal(ks[0], (BATCH, SEQ, D_MODEL), f32),
        "norm_mix": gain(ks[1], (DEPTH, D_MODEL)),
        "w_in": w(ks[2], (DEPTH, D_MODEL, IN_TOTAL), D_MODEL),
        "norm_qa": gain(ks[3], (DEPTH, Q_LORA)),
        "w_uq": w(ks[4], (DEPTH, Q_LORA, MLA_HEADS * (QK_NOPE + QK_ROPE)), Q_LORA),
        "norm_kva": gain(ks[5], (DEPTH, KV_LORA)),
        "w_ukv": w(ks[6], (DEPTH, KV_LORA, MLA_HEADS * (QK_NOPE + V_HEAD)), KV_LORA),
        "rpb": 0.02 * jax.random.normal(ks[7], (DEPTH, NA_HEADS, 2 * NA_KH - 1, 2 * NA_KW - 1), f32),
        "w_o_mla": w(ks[8], (DEPTH, MLA_W, D_MODEL), MLA_W),
        "w_o_na": w(ks[9], (DEPTH, NA_W, D_MODEL), NA_W),
        "w_out": w(ks[10], (DEPTH, D_MODEL, D_MODEL), D_MODEL),
        "norm_mlp": gain(ks[11], (DEPTH, D_MODEL)),
        "w_ff1": w(ks[12], (DEPTH, D_MODEL, D_FF), D_MODEL),
        "w_ff2": w(ks[13], (DEPTH, D_FF, D_MODEL), D_FF),
        "norm_final": gain(ks[14], (D_MODEL,)),
    }


def _fwd_reference(x, norm_mix, w_in, norm_qa, w_uq, norm_kva, w_ukv, rpb, w_o_mla, w_o_na,
              w_out, norm_mlp, w_ff1, w_ff2, norm_final):
    B, S, _ = x.shape
    pos = jnp.arange(S, dtype=jnp.float32)
    inv_freq = 1.0 / (ROPE_THETA ** (jnp.arange(0, QK_ROPE, 2, dtype=jnp.float32) / QK_ROPE))
    ang = pos[:, None] * inv_freq[None, :]
    cos = jnp.cos(ang).astype(x.dtype)
    sin = jnp.sin(ang).astype(x.dtype)

    for l in range(DEPTH):
        u = rmsnorm(x, norm_mix[l])
        proj = u @ w_in[l]
        c_q, c_kv, k_pe, q_na, k_na, v_na, gate_a, gate_b = jnp.split(proj, IN_SPLITS, axis=-1)

        q = (rmsnorm(c_q, norm_qa[l]) @ w_uq[l]).reshape(B, S, MLA_HEADS, QK_NOPE + QK_ROPE)
        kv = (rmsnorm(c_kv, norm_kva[l]) @ w_ukv[l]).reshape(B, S, MLA_HEADS, QK_NOPE + V_HEAD)
        q_nope, q_pe = q[..., :QK_NOPE], q[..., QK_NOPE:]
        k_nope, v = kv[..., :QK_NOPE], kv[..., QK_NOPE:]
        q_pe = rope(q_pe, cos[:, None, :], sin[:, None, :])
        k_pe = rope(k_pe, cos, sin)
        y_a = mla_attention(q_nope, q_pe, k_nope, k_pe, v) @ w_o_mla[l]

        hs = (B, S, NA_HEADS, NA_HEAD_DIM)
        y_b = neighbourhood_attention(q_na.reshape(hs), k_na.reshape(hs), v_na.reshape(hs), rpb[l]) @ w_o_na[l]

        merged = jax.nn.sigmoid(gate_a) * y_a + jax.nn.sigmoid(gate_b) * y_b
        x = x + merged @ w_out[l]

        h = rmsnorm(x, norm_mlp[l]) @ w_ff1[l]
        x = x + jnp.square(jax.nn.relu(h)) @ w_ff2[l]

    return rmsnorm(x, norm_final)


import jax as _jax
import jax.numpy as _jnp

TWIN_FORMAT = 'train_step'
FWD_PARAMS = ['x', 'norm_mix', 'w_in', 'norm_qa', 'w_uq', 'norm_kva', 'w_ukv', 'rpb', 'w_o_mla', 'w_o_na', 'w_out', 'norm_mlp', 'w_ff1', 'w_ff2', 'norm_final']
TWIN_WEIGHTS = ['norm_mix', 'w_in', 'norm_qa', 'w_uq', 'norm_kva', 'w_ukv', 'rpb', 'w_o_mla', 'w_o_na', 'w_out', 'norm_mlp', 'w_ff1', 'w_ff2', 'norm_final']
TWIN_DIFF_INPUT = 'x'
TWIN_INPUTS = ['x', 'norm_mix', 'w_in', 'norm_qa', 'w_uq', 'norm_kva', 'w_ukv', 'rpb', 'w_o_mla', 'w_o_na', 'w_out', 'norm_mlp', 'w_ff1', 'w_ff2', 'norm_final', 'loss_target', 'm_norm_mix', 'm_w_in', 'm_norm_qa', 'm_w_uq', 'm_norm_kva', 'm_w_ukv', 'm_rpb', 'm_w_o_mla', 'm_w_o_na', 'm_w_out', 'm_norm_mlp', 'm_w_ff1', 'm_w_ff2', 'm_norm_final', 'v_norm_mix', 'v_w_in', 'v_norm_qa', 'v_w_uq', 'v_norm_kva', 'v_w_ukv', 'v_rpb', 'v_w_o_mla', 'v_w_o_na', 'v_w_out', 'v_norm_mlp', 'v_w_ff1', 'v_w_ff2', 'v_norm_final']
TWIN_OUTPUTS = ['loss', 'grad_x', 'grad_norm_mix', 'grad_w_in', 'grad_norm_qa', 'grad_w_uq', 'grad_norm_kva', 'grad_w_ukv', 'grad_rpb', 'grad_w_o_mla', 'grad_w_o_na', 'grad_w_out', 'grad_norm_mlp', 'grad_w_ff1', 'grad_w_ff2', 'grad_norm_final', 'delta_norm_mix', 'delta_w_in', 'delta_norm_qa', 'delta_w_uq', 'delta_norm_kva', 'delta_w_ukv', 'delta_rpb', 'delta_w_o_mla', 'delta_w_o_na', 'delta_w_out', 'delta_norm_mlp', 'delta_w_ff1', 'delta_w_ff2', 'delta_norm_final', 'new_m_norm_mix', 'new_m_w_in', 'new_m_norm_qa', 'new_m_w_uq', 'new_m_norm_kva', 'new_m_w_ukv', 'new_m_rpb', 'new_m_w_o_mla', 'new_m_w_o_na', 'new_m_w_out', 'new_m_norm_mlp', 'new_m_w_ff1', 'new_m_w_ff2', 'new_m_norm_final', 'new_v_norm_mix', 'new_v_w_in', 'new_v_norm_qa', 'new_v_w_uq', 'new_v_norm_kva', 'new_v_w_ukv', 'new_v_rpb', 'new_v_w_o_mla', 'new_v_w_o_na', 'new_v_w_out', 'new_v_norm_mlp', 'new_v_w_ff1', 'new_v_w_ff2', 'new_v_norm_final']
TWIN_LEAF_KINDS = {'loss': 'loss', 'grad_x': 'grad_x', 'grad_norm_mix': 'grad_w', 'grad_w_in': 'grad_w', 'grad_norm_qa': 'grad_w', 'grad_w_uq': 'grad_w', 'grad_norm_kva': 'grad_w', 'grad_w_ukv': 'grad_w', 'grad_rpb': 'grad_w', 'grad_w_o_mla': 'grad_w', 'grad_w_o_na': 'grad_w', 'grad_w_out': 'grad_w', 'grad_norm_mlp': 'grad_w', 'grad_w_ff1': 'grad_w', 'grad_w_ff2': 'grad_w', 'grad_norm_final': 'grad_w', 'delta_norm_mix': 'delta_w', 'delta_w_in': 'delta_w', 'delta_norm_qa': 'delta_w', 'delta_w_uq': 'delta_w', 'delta_norm_kva': 'delta_w', 'delta_w_ukv': 'delta_w', 'delta_rpb': 'delta_w', 'delta_w_o_mla': 'delta_w', 'delta_w_o_na': 'delta_w', 'delta_w_out': 'delta_w', 'delta_norm_mlp': 'delta_w', 'delta_w_ff1': 'delta_w', 'delta_w_ff2': 'delta_w', 'delta_norm_final': 'delta_w', 'new_m_norm_mix': 'new_m', 'new_m_w_in': 'new_m', 'new_m_norm_qa': 'new_m', 'new_m_w_uq': 'new_m', 'new_m_norm_kva': 'new_m', 'new_m_w_ukv': 'new_m', 'new_m_rpb': 'new_m', 'new_m_w_o_mla': 'new_m', 'new_m_w_o_na': 'new_m', 'new_m_w_out': 'new_m', 'new_m_norm_mlp': 'new_m', 'new_m_w_ff1': 'new_m', 'new_m_w_ff2': 'new_m', 'new_m_norm_final': 'new_m', 'new_v_norm_mix': 'new_v', 'new_v_w_in': 'new_v', 'new_v_norm_qa': 'new_v', 'new_v_w_uq': 'new_v', 'new_v_norm_kva': 'new_v', 'new_v_w_ukv': 'new_v', 'new_v_rpb': 'new_v', 'new_v_w_o_mla': 'new_v', 'new_v_w_o_na': 'new_v', 'new_v_w_out': 'new_v', 'new_v_norm_mlp': 'new_v', 'new_v_w_ff1': 'new_v', 'new_v_w_ff2': 'new_v', 'new_v_norm_final': 'new_v'}


def _forward(args):
    return _fwd_reference(*[args[k] for k in FWD_PARAMS])


def _output_shape():
    def fwd():
        inp = _fwd_setup_inputs(0)
        return _fwd_reference(*[inp[k] for k in FWD_PARAMS])
    out = _jax.eval_shape(fwd)
    return out.shape, out.dtype

N_MICROBATCH = 1
ADAM_LR = 0.001
ADAM_B1 = 0.9
ADAM_B2 = 0.999
ADAM_EPS = 1e-08
ADAM_WD = 0.01
ADAM_STEP = 10
PER_EXAMPLE_BATCH_AXIS = {'x': 0, 'loss_target': 0}
SHARED_INPUTS = []
_WEIGHT_DTYPES = {'norm_mix': _jnp.float32, 'w_in': _jnp.float32, 'norm_qa': _jnp.float32, 'w_uq': _jnp.float32, 'norm_kva': _jnp.float32, 'w_ukv': _jnp.float32, 'rpb': _jnp.float32, 'w_o_mla': _jnp.float32, 'w_o_na': _jnp.float32, 'w_out': _jnp.float32, 'norm_mlp': _jnp.float32, 'w_ff1': _jnp.float32, 'w_ff2': _jnp.float32, 'norm_final': _jnp.float32}
MOMENT_SCALE = {'norm_mix': 2.792406e-02, 'w_in': 1.383121e-02, 'norm_qa': 1.058760e-02, 'w_uq': 6.258144e-03, 'norm_kva': 3.495293e-02, 'w_ukv': 1.779641e-02, 'rpb': 7.019968e-03, 'w_o_mla': 1.788479e-02, 'w_o_na': 1.828823e-02, 'w_out': 2.431662e-02, 'norm_mlp': 9.500537e-02, 'w_ff1': 4.773074e-02, 'w_ff2': 9.455452e-02, 'norm_final': 3.328121e+01}


def _to_microbatches(a, axis):
    t = _jnp.moveaxis(a, axis, 0)
    t = t.reshape((N_MICROBATCH, t.shape[0] // N_MICROBATCH) + t.shape[1:])
    return _jnp.moveaxis(t, 1, axis + 1)


def setup_inputs(seed: int = 0) -> dict:
    inp = _fwd_setup_inputs(seed)
    key = _jax.random.fold_in(_jax.random.key(seed), 7919)
    shape, _ = _output_shape()
    out = dict(inp)
    out["loss_target"] = _jax.random.normal(_jax.random.fold_in(key, 0), shape, _jnp.float32)
    for i, name in enumerate(TWIN_WEIGHTS):
        w = inp[name].astype(_jnp.float32)
        if MOMENT_SCALE is None:
            s = _jnp.sqrt(_jnp.mean(_jnp.square(w)) + 1e-30)
        else:
            s = MOMENT_SCALE[name]
        km, kv = _jax.random.split(_jax.random.fold_in(key, i + 1))
        out[name] = w
        out["m_" + name] = s * _jax.random.normal(km, w.shape, _jnp.float32)
        out["v_" + name] = (s * s) * _jax.random.uniform(kv, w.shape, _jnp.float32, 0.5, 1.5)
    if N_MICROBATCH > 1:
        for name, axis in PER_EXAMPLE_BATCH_AXIS.items():
            out[name] = _to_microbatches(out[name], axis)
    return {'x': out['x'], 'norm_mix': out['norm_mix'], 'w_in': out['w_in'], 'norm_qa': out['norm_qa'], 'w_uq': out['w_uq'], 'norm_kva': out['norm_kva'], 'w_ukv': out['w_ukv'], 'rpb': out['rpb'], 'w_o_mla': out['w_o_mla'], 'w_o_na': out['w_o_na'], 'w_out': out['w_out'], 'norm_mlp': out['norm_mlp'], 'w_ff1': out['w_ff1'], 'w_ff2': out['w_ff2'], 'norm_final': out['norm_final'], 'loss_target': out['loss_target'], 'm_norm_mix': out['m_norm_mix'], 'm_w_in': out['m_w_in'], 'm_norm_qa': out['m_norm_qa'], 'm_w_uq': out['m_w_uq'], 'm_norm_kva': out['m_norm_kva'], 'm_w_ukv': out['m_w_ukv'], 'm_rpb': out['m_rpb'], 'm_w_o_mla': out['m_w_o_mla'], 'm_w_o_na': out['m_w_o_na'], 'm_w_out': out['m_w_out'], 'm_norm_mlp': out['m_norm_mlp'], 'm_w_ff1': out['m_w_ff1'], 'm_w_ff2': out['m_w_ff2'], 'm_norm_final': out['m_norm_final'], 'v_norm_mix': out['v_norm_mix'], 'v_w_in': out['v_w_in'], 'v_norm_qa': out['v_norm_qa'], 'v_w_uq': out['v_w_uq'], 'v_norm_kva': out['v_norm_kva'], 'v_w_ukv': out['v_w_ukv'], 'v_rpb': out['v_rpb'], 'v_w_o_mla': out['v_w_o_mla'], 'v_w_o_na': out['v_w_o_na'], 'v_w_out': out['v_w_out'], 'v_norm_mlp': out['v_norm_mlp'], 'v_w_ff1': out['v_w_ff1'], 'v_w_ff2': out['v_w_ff2'], 'v_norm_final': out['v_norm_final']}


def _loss(weights, diff, rest, loss_target):
    with _jax.named_scope("forward"):
        args = {**rest, TWIN_DIFF_INPUT: diff, **{k: w.astype(_WEIGHT_DTYPES[k]) for k, w in weights.items()}}
        y = _forward(args)
    with _jax.named_scope("loss_head"):
        err = _jnp.square(y.astype(_jnp.float32) - loss_target)
        return 0.5 * _jnp.sum(_jnp.mean(err, axis=-1)) if err.ndim else 0.5 * err


def _adamw(w, g, m, v):
    m = ADAM_B1 * m + (1.0 - ADAM_B1) * g
    v = ADAM_B2 * v + (1.0 - ADAM_B2) * _jnp.square(g)
    m_hat = m / (1.0 - ADAM_B1 ** ADAM_STEP)
    v_hat = v / (1.0 - ADAM_B2 ** ADAM_STEP)
    delta = -ADAM_LR * (m_hat / (_jnp.sqrt(v_hat) + ADAM_EPS) + ADAM_WD * w)
    return delta, m, v


def reference(x, norm_mix, w_in, norm_qa, w_uq, norm_kva, w_ukv, rpb, w_o_mla, w_o_na, w_out, norm_mlp, w_ff1, w_ff2, norm_final, loss_target, m_norm_mix, m_w_in, m_norm_qa, m_w_uq, m_norm_kva, m_w_ukv, m_rpb, m_w_o_mla, m_w_o_na, m_w_out, m_norm_mlp, m_w_ff1, m_w_ff2, m_norm_final, v_norm_mix, v_w_in, v_norm_qa, v_w_uq, v_norm_kva, v_w_ukv, v_rpb, v_w_o_mla, v_w_o_na, v_w_out, v_norm_mlp, v_w_ff1, v_w_ff2, v_norm_final):
    given = dict(x=x, norm_mix=norm_mix, w_in=w_in, norm_qa=norm_qa, w_uq=w_uq, norm_kva=norm_kva, w_ukv=w_ukv, rpb=rpb, w_o_mla=w_o_mla, w_o_na=w_o_na, w_out=w_out, norm_mlp=norm_mlp, w_ff1=w_ff1, w_ff2=w_ff2, norm_final=norm_final, loss_target=loss_target, m_norm_mix=m_norm_mix, m_w_in=m_w_in, m_norm_qa=m_norm_qa, m_w_uq=m_w_uq, m_norm_kva=m_norm_kva, m_w_ukv=m_w_ukv, m_rpb=m_rpb, m_w_o_mla=m_w_o_mla, m_w_o_na=m_w_o_na, m_w_out=m_w_out, m_norm_mlp=m_norm_mlp, m_w_ff1=m_w_ff1, m_w_ff2=m_w_ff2, m_norm_final=m_norm_final, v_norm_mix=v_norm_mix, v_w_in=v_w_in, v_norm_qa=v_norm_qa, v_w_uq=v_w_uq, v_norm_kva=v_norm_kva, v_w_ukv=v_w_ukv, v_rpb=v_rpb, v_w_o_mla=v_w_o_mla, v_w_o_na=v_w_o_na, v_w_out=v_w_out, v_norm_mlp=v_norm_mlp, v_w_ff1=v_w_ff1, v_w_ff2=v_w_ff2, v_norm_final=v_norm_final)
    weights = {n: given[n] for n in TWIN_WEIGHTS}
    shared = {n: given[n] for n in SHARED_INPUTS}
    per_example = {n: given[n] for n in ['x']}
    grad_fn = _jax.value_and_grad(_loss, argnums=(0, 1))

    def one_microbatch(ex, loss_target):
        ex = dict(ex)
        diff = ex.pop(TWIN_DIFF_INPUT)
        return grad_fn(weights, diff, {**shared, **ex}, loss_target)

    if N_MICROBATCH == 1:
        loss, (grad_w, grad_x) = one_microbatch(per_example, given["loss_target"])
    else:
        def body(carry, xs):
            loss_sum, grad_sum = carry
            l_k, (gw_k, gx_k) = one_microbatch(xs[0], xs[1])
            with _jax.named_scope("update"):
                return (loss_sum + l_k, _jax.tree.map(_jnp.add, grad_sum, gw_k)), gx_k

        init = (_jnp.zeros((), _jnp.float32), _jax.tree.map(_jnp.zeros_like, weights))
        (loss, grad_w), grad_x = _jax.lax.scan(body, init, (per_example, given["loss_target"]))
    with _jax.named_scope("update"):
        delta_w, new_m, new_v = {}, {}, {}
        for n in TWIN_WEIGHTS:
            delta_w[n], new_m[n], new_v[n] = _adamw(weights[n], grad_w[n], given["m_" + n], given["v_" + n])
    return (loss, grad_x, *[grad_w[n] for n in TWIN_WEIGHTS], *[delta_w[n] for n in TWIN_WEIGHTS],
            *[new_m[n] for n in TWIN_WEIGHTS], *[new_v[n] for n in TWIN_WEIGHTS])
```

```python
import functools

import numpy as np
import jax
import jax.numpy as jnp
from jax import lax
from jax.experimental import pallas as pl
from jax.experimental.pallas import tpu as pltpu

F32 = jnp.float32
BF16 = jnp.bfloat16
MESH = pl.DeviceIdType.MESH

EPS = 1e-6
ROPE_THETA = 10000.0
ROPE = 64
HEAD = 128
GRID_W = 64
NA_KH = 8
NA_KW = 16
N_CHIPS = 4
N_DEV = 8
NEG = -1e30

ADAM_LR = 0.001
ADAM_B1 = 0.9
ADAM_B2 = 0.999
ADAM_EPS = 1e-08
ADAM_WD = 0.01
ADAM_STEP = 10

VMEM_LIMIT = 56 * 1024 * 1024

BIG = ("w_in", "w_uq", "w_ukv", "w_o_mla", "w_o_na", "w_out", "w_ff1", "w_ff2")
ROW_SHARDED = ("w_out", "w_ff2")
SMALL = ("norm_mix", "norm_qa", "norm_kva", "rpb", "norm_mlp", "norm_final")
WEIGHTS = ("norm_mix", "w_in", "norm_qa", "w_uq", "norm_kva", "w_ukv", "rpb", "w_o_mla", "w_o_na",
           "w_out", "norm_mlp", "w_ff1", "w_ff2", "norm_final")


def _cparams(sem, **kw):
    return pltpu.CompilerParams(dimension_semantics=sem, vmem_limit_bytes=VMEM_LIMIT, **kw)


def _tile(n, cap, unit=128):
    if n <= cap:
        return n
    best = None
    for t in range(unit, cap + 1, unit):
        if n % t == 0:
            best = t
    assert best is not None, (n, cap, unit)
    return best


def _sds(shape, dtype):
    return jax.ShapeDtypeStruct(shape, dtype)


_DIMS = {"nn": (((1,), (0,)), ((), ())), "nt": (((1,), (1,)), ((), ())), "tn": (((0,), (0,)), ((), ()))}


def _store_cast(acc, extra, outs):
    outs[0][...] = acc.astype(outs[0].dtype)


def _mm(name, a, b, mode, out_shapes, *, lb=None, extras=(), extra_specs=(), out_specs=None,
        epilogue=_store_cast, tm_cap=1024, tn_cap=1024, tk_cap=1024):
    bshape = b.shape[1:] if lb is not None else b.shape
    if mode == "nn":
        (M, K), (K2, N) = a.shape, bshape
    elif mode == "nt":
        (M, K), (N, K2) = a.shape, bshape
    else:
        (K, M), (K2, N) = a.shape, bshape
    assert K == K2, (name, a.shape, b.shape)
    tm, tn, tk = _tile(M, tm_cap), _tile(N, tn_cap), _tile(K, tk_cap)
    nk = K // tk
    if mode == "tn":
        a_spec = pl.BlockSpec((tk, tm), lambda i, j, k: (k, i))
    else:
        a_spec = pl.BlockSpec((tm, tk), lambda i, j, k: (i, k))
    bblk, bidx = ((tn, tk), lambda i, j, k: (j, k)) if mode == "nt" else ((tk, tn), lambda i, j, k: (k, j))
    if lb is not None:
        b_spec = pl.BlockSpec((None,) + bblk, lambda i, j, k: (lb,) + bidx(i, j, k))
    else:
        b_spec = pl.BlockSpec(bblk, bidx)
    ne, no = len(extras), len(out_shapes)
    if out_specs is None:
        out_specs = [lambda tm, tn: pl.BlockSpec((tm, tn), lambda i, j, k: (i, j))] * no
    dims = _DIMS[mode]

    def body(*refs):
        a_ref, b_ref = refs[0], refs[1]
        extra, outs, acc = refs[2:2 + ne], refs[2 + ne:2 + ne + no], refs[-1]
        k = pl.program_id(2)

        @pl.when(k == 0)
        def _():
            acc[...] = jnp.zeros_like(acc)

        acc[...] += lax.dot_general(a_ref[...], b_ref[...], dims, preferred_element_type=F32)

        @pl.when(k == nk - 1)
        def _():
            epilogue(acc[...], extra, outs)

    return pl.pallas_call(
        body, name=name, grid=(M // tm, N // tn, nk),
        in_specs=[a_spec, b_spec] + [s(tm, tn) for s in extra_specs],
        out_specs=[s(tm, tn) for s in out_specs],
        out_shape=list(out_shapes),
        scratch_shapes=[pltpu.VMEM((tm, tn), F32)],
        compiler_params=_cparams(("parallel", "parallel", "arbitrary")),
    )(a, b, *extras)


def _tile_spec(tm, tn):
    return pl.BlockSpec((tm, tn), lambda i, j, k: (i, j))


def _row_spec(width):
    return lambda tm, tn: pl.BlockSpec((tm, width), lambda i, j, k: (i, 0))


def _col_spec(off_cols):
    def make(tm, tn):
        assert off_cols % tn == 0, (off_cols, tn)
        return pl.BlockSpec((tm, tn), lambda i, j, k: (i, off_cols // tn + j))
    return make


def _rms_fwd(name, x, g):
    S, D = x.shape
    tm = _tile(S, 256, 8)

    def body(x_ref, g_ref, u_ref, r_ref):
        xv = x_ref[...]
        r = lax.rsqrt(jnp.mean(xv * xv, axis=-1, keepdims=True) + EPS)
        u_ref[...] = (xv * r * g_ref[...]).astype(BF16)
        r_ref[...] = r

    return pl.pallas_call(
        body, name=name, grid=(S // tm,),
        in_specs=[pl.BlockSpec((tm, D), lambda i: (i, 0)), pl.BlockSpec((1, D), lambda i: (0, 0))],
        out_specs=[pl.BlockSpec((tm, D), lambda i: (i, 0)), pl.BlockSpec((tm, 1), lambda i: (i, 0))],
        out_shape=[_sds((S, D), BF16), _sds((S, 1), F32)],
        compiler_params=_cparams(("parallel",)),
    )(x, g)


def _rms_bwd(name, x, r, g, du, dres):
    S, D = x.shape
    tm = _tile(S, 256, 8)

    def body(x_ref, r_ref, g_ref, du_ref, dres_ref, dx_ref, dxb_ref, dg_ref):
        rv = r_ref[...]
        xhat = x_ref[...] * rv
        duv = du_ref[...].astype(F32)
        dxh = duv * g_ref[...]
        m = jnp.mean(dxh * xhat, axis=-1, keepdims=True)
        dx = dres_ref[...] + rv * (dxh - xhat * m)
        dx_ref[...] = dx
        dxb_ref[...] = dx.astype(BF16)

        @pl.when(pl.program_id(0) == 0)
        def _():
            dg_ref[...] = jnp.zeros_like(dg_ref)

        dg_ref[...] += jnp.sum(duv * xhat, axis=0, keepdims=True)

    row = pl.BlockSpec((tm, D), lambda i: (i, 0))
    vec = pl.BlockSpec((1, D), lambda i: (0, 0))
    return pl.pallas_call(
        body, name=name, grid=(S // tm,),
        in_specs=[row, pl.BlockSpec((tm, 1), lambda i: (i, 0)), vec, row, row],
        out_specs=[row, row, vec],
        out_shape=[_sds((S, D), F32), _sds((S, D), BF16), _sds((1, D), F32)],
        compiler_params=_cparams(("arbitrary",)),
    )(x, r, g, du, dres)


def _final_loss(x, t, g):
    S, D = x.shape
    tm = _tile(S, 256, 8)

    def body(x_ref, t_ref, g_ref, loss_ref, dx_ref, dxb_ref, dg_ref):
        xv = x_ref[...]
        gv = g_ref[...]
        rv = lax.rsqrt(jnp.mean(xv * xv, axis=-1, keepdims=True) + EPS)
        xhat = xv * rv
        diff = xhat * gv - t_ref[...]
        dy = diff * (1.0 / D)
        dxh = dy * gv
        m = jnp.mean(dxh * xhat, axis=-1, keepdims=True)
        dx = rv * (dxh - xhat * m)
        dx_ref[...] = dx
        dxb_ref[...] = dx.astype(BF16)

        @pl.when(pl.program_id(0) == 0)
        def _():
            dg_ref[...] = jnp.zeros_like(dg_ref)
            loss_ref[...] = jnp.zeros_like(loss_ref)

        dg_ref[...] += jnp.sum(dy * xhat, axis=0, keepdims=True)
        per_row = jnp.mean(diff * diff, axis=-1, keepdims=True)
        loss_ref[...] += 0.5 * jnp.sum(per_row, axis=0, keepdims=True)

    row = pl.BlockSpec((tm, D), lambda i: (i, 0))
    vec = pl.BlockSpec((1, D), lambda i: (0, 0))
    return pl.pallas_call(
        body, name="final_loss", grid=(S // tm,),
        in_specs=[row, row, vec],
        out_specs=[pl.BlockSpec((1, 128), lambda i: (0, 0)), row, row, vec],
        out_shape=[_sds((1, 128), F32), _sds((S, D), F32), _sds((S, D), BF16), _sds((1, D), F32)],
        compiler_params=_cparams(("arbitrary",)),
    )(x, t, g)


def _rope128(v, cos_t, sin_t):
    lane = lax.broadcasted_iota(jnp.int32, v.shape, 1)
    up = pltpu.roll(v, 128 - ROPE // 2, 1)
    dn = pltpu.roll(v, ROPE // 2, 1)
    return v * cos_t + jnp.where(lane < ROPE // 2, up, dn) * sin_t


def _lat_fwd(l, proj, g_qa, g_kva, cos_t, sin_t, off_cq, off_ckv, off_kpe):
    S = proj.shape[0]
    QL, KL = g_qa.shape[1], g_kva.shape[1]
    tm = _tile(S, 512, 8)
    assert off_cq % QL == 0 and off_ckv % KL == 0 and off_kpe % 128 == 0

    def body(cq_ref, ckv_ref, kpe_ref, gq_ref, gkv_ref, cos_ref, sin_ref, nq_ref, nkv_ref, rq_ref, rkv_ref, kp_ref):
        for c_ref, g_ref, n_ref, r_ref in ((cq_ref, gq_ref, nq_ref, rq_ref), (ckv_ref, gkv_ref, nkv_ref, rkv_ref)):
            cv = c_ref[...].astype(F32)
            r = lax.rsqrt(jnp.mean(cv * cv, axis=-1, keepdims=True) + EPS)
            n_ref[...] = (cv * r * g_ref[...]).astype(BF16)
            r_ref[...] = r
        kp_ref[...] = _rope128(kpe_ref[...].astype(F32), cos_ref[...], sin_ref[...]).astype(BF16)

    col = lambda w, off: pl.BlockSpec((tm, w), lambda i: (i, off // w))
    row = lambda w: pl.BlockSpec((tm, w), lambda i: (i, 0))
    vec = lambda w: pl.BlockSpec((1, w), lambda i: (0, 0))
    return pl.pallas_call(
        body, name=f"lat_fwd_{l}", grid=(S // tm,),
        in_specs=[col(QL, off_cq), col(KL, off_ckv), col(128, off_kpe), vec(QL), vec(KL), row(128), row(128)],
        out_specs=[row(QL), row(KL), row(1), row(1), row(128)],
        out_shape=[_sds((S, QL), BF16), _sds((S, KL), BF16), _sds((S, 1), F32), _sds((S, 1), F32), _sds((S, 128), BF16)],
        compiler_params=_cparams(("parallel",)),
    )(proj, proj, proj, g_qa, g_kva, cos_t, sin_t)


def _lat_bwd(l, proj, rq, rkv, g_qa, g_kva, dnq, dnkv, dkp_h, cos_t, sin_t, off_cq, off_ckv):
    S = proj.shape[0]
    QL, KL = g_qa.shape[1], g_kva.shape[1]
    H = dkp_h.shape[0]
    tm = _tile(S, 512, 8)

    def body(cq_ref, ckv_ref, rq_ref, rkv_ref, gq_ref, gkv_ref, dnq_ref, dnkv_ref, dkp_ref, cos_ref, sin_ref,
             dcq_ref, dckv_ref, dkpe_ref, dgq_ref, dgkv_ref):
        first = pl.program_id(0) == 0
        for c_ref, r_ref, g_ref, dn_ref, dc_ref, dg_ref in (
                (cq_ref, rq_ref, gq_ref, dnq_ref, dcq_ref, dgq_ref),
                (ckv_ref, rkv_ref, gkv_ref, dnkv_ref, dckv_ref, dgkv_ref)):
            rv = r_ref[...]
            xhat = c_ref[...].astype(F32) * rv
            dn = dn_ref[...]
            dxh = dn * g_ref[...]
            m = jnp.mean(dxh * xhat, axis=-1, keepdims=True)
            dc_ref[...] = (rv * (dxh - xhat * m)).astype(BF16)

            @pl.when(first)
            def _():
                dg_ref[...] = jnp.zeros_like(dg_ref)

            dg_ref[...] += jnp.sum(dn * xhat, axis=0, keepdims=True)
        dkp = dkp_ref[0]
        for h in range(1, H):
            dkp = dkp + dkp_ref[h]
        dkpe_ref[...] = _rope128(dkp, cos_ref[...], -sin_ref[...]).astype(BF16)

    col = lambda w, off: pl.BlockSpec((tm, w), lambda i: (i, off // w))
    row = lambda w: pl.BlockSpec((tm, w), lambda i: (i, 0))
    vec = lambda w: pl.BlockSpec((1, w), lambda i: (0, 0))
    return pl.pallas_call(
        body, name=f"lat_bwd_{l}", grid=(S // tm,),
        in_specs=[col(QL, off_cq), col(KL, off_ckv), row(1), row(1), vec(QL), vec(KL), row(QL), row(KL),
                  pl.BlockSpec((H, tm, 128), lambda i: (0, i, 0)), row(128), row(128)],
        out_specs=[row(QL), row(KL), row(128), vec(QL), vec(KL)],
        out_shape=[_sds((S, QL), BF16), _sds((S, KL), BF16), _sds((S, 128), BF16), _sds((1, QL), F32), _sds((1, KL), F32)],
        compiler_params=_cparams(("arbitrary",)),
    )(proj, proj, rq, rkv, g_qa, g_kva, dnq, dnkv, dkp_h, cos_t, sin_t)


def _rope_bwd_q(l, dq, cos_t, sin_t):
    S, W = dq.shape
    tm = _tile(S, 256, 8)
    nh = W // 256

    def body(dq_ref, cos_ref, sin_ref, out_ref):
        cv, sv = cos_ref[...], -sin_ref[...]
        for h in range(nh):
            out_ref[:, h * 256:h * 256 + 128] = dq_ref[:, h * 256:h * 256 + 128].astype(BF16)
            out_ref[:, h * 256 + 128:(h + 1) * 256] = _rope128(dq_ref[:, h * 256 + 128:(h + 1) * 256], cv, sv).astype(BF16)

    return pl.pallas_call(
        body, name=f"rope_bwd_q_{l}", grid=(S // tm,),
        in_specs=[pl.BlockSpec((tm, W), lambda i: (i, 0)), pl.BlockSpec((tm, 128), lambda i: (i, 0)),
                  pl.BlockSpec((tm, 128), lambda i: (i, 0))],
        out_specs=pl.BlockSpec((tm, W), lambda i: (i, 0)),
        out_shape=_sds((S, W), BF16),
        compiler_params=_cparams(("parallel",)),
    )(dq, cos_t, sin_t)


def _delta(l, o, do):
    S, W = o.shape
    H = W // HEAD
    tm = _tile(S, 1024, 8)

    def body(o_ref, do_ref, d_ref):
        d_ref[...] = jnp.sum(o_ref[...].astype(F32) * do_ref[...].astype(F32), axis=-1, keepdims=True)

    blk = pl.BlockSpec((tm, HEAD), lambda h, i: (i, h))
    return pl.pallas_call(
        body, name=f"delta_{l}", grid=(H, S // tm),
        in_specs=[blk, blk],
        out_specs=pl.BlockSpec((None, tm, 1), lambda h, i: (h, i, 0)),
        out_shape=_sds((H, S, 1), F32),
        compiler_params=_cparams(("parallel", "parallel")),
    )(o, do)


_NT = (((1,), (1,)), ((), ()))
_TN = (((0,), (0,)), ((), ()))


def _mla_fwd(l, q, kv, kp, scale):
    S = q.shape[0]
    H = q.shape[1] // 256
    tq, tk = _tile(S, 1024, 8), _tile(S, 512, 128)
    nk = S // tk

    def body(q_ref, kn_ref, v_ref, kp_ref, o_ref, lse_ref, m_sc, l_sc, acc_sc):
        ki = pl.program_id(2)

        @pl.when(ki == 0)
        def _():
            m_sc[...] = jnp.full_like(m_sc, NEG)
            l_sc[...] = jnp.zeros_like(l_sc)
            acc_sc[...] = jnp.zeros_like(acc_sc)

        kc = jnp.concatenate([kn_ref[...], kp_ref[...]], axis=1)
        s = lax.dot_general(q_ref[...], kc, _NT, preferred_element_type=F32) * scale
        m_prev = m_sc[...]
        m_new = jnp.maximum(m_prev, jnp.max(s, axis=-1, keepdims=True))
        alpha = jnp.exp(m_prev - m_new)
        p = jnp.exp(s - m_new)
        l_sc[...] = alpha * l_sc[...] + jnp.sum(p, axis=-1, keepdims=True)
        acc_sc[...] = alpha * acc_sc[...] + jnp.dot(p.astype(BF16), v_ref[...], preferred_element_type=F32)
        m_sc[...] = m_new

        @pl.when(ki == nk - 1)
        def _():
            o_ref[...] = (acc_sc[...] / l_sc[...]).astype(BF16)
            lse_ref[...] = m_sc[...] + jnp.log(l_sc[...])

    return pl.pallas_call(
        body, name=f"mla_fwd_{l}", grid=(H, S // tq, nk),
        in_specs=[pl.BlockSpec((tq, 256), lambda h, i, k: (i, h)),
                  pl.BlockSpec((tk, HEAD), lambda h, i, k: (k, 2 * h)),
                  pl.BlockSpec((tk, HEAD), lambda h, i, k: (k, 2 * h + 1)),
                  pl.BlockSpec((tk, 128), lambda h, i, k: (k, 0))],
        out_specs=[pl.BlockSpec((tq, HEAD), lambda h, i, k: (i, h)),
                   pl.BlockSpec((None, tq, 1), lambda h, i, k: (h, i, 0))],
        out_shape=[_sds((S, H * HEAD), BF16), _sds((H, S, 1), F32)],
        scratch_shapes=[pltpu.VMEM((tq, 1), F32), pltpu.VMEM((tq, 1), F32), pltpu.VMEM((tq, HEAD), F32)],
        compiler_params=_cparams(("parallel", "parallel", "arbitrary")),
    )(q, kv, kv, kp)


def _mla_bwd(l, q, kv, kp, do, lse, delta, scale):
    S = q.shape[0]
    H = q.shape[1] // 256
    tq, tk = _tile(S, 512, 8), _tile(S, 512, 128)
    nq = S // tq

    def body(q_ref, kn_ref, v_ref, kp_ref, do_ref, lse_ref, dl_ref, dq_ref, dkv_ref, dkp_ref, dkc_sc, dv_sc):
        ki, qi = pl.program_id(1), pl.program_id(2)

        @pl.when(qi == 0)
        def _():
            dkc_sc[...] = jnp.zeros_like(dkc_sc)
            dv_sc[...] = jnp.zeros_like(dv_sc)

        qv, dov = q_ref[...], do_ref[...]
        kc = jnp.concatenate([kn_ref[...], kp_ref[...]], axis=1)
        s = lax.dot_general(qv, kc, _NT, preferred_element_type=F32) * scale
        p = jnp.exp(s - lse_ref[...])
        dv_sc[...] += lax.dot_general(p.astype(BF16), dov, _TN, preferred_element_type=F32)
        dp = lax.dot_general(dov, v_ref[...], _NT, preferred_element_type=F32)
        ds = (p * (dp - dl_ref[...]) * scale).astype(BF16)
        dkc_sc[...] += lax.dot_general(ds, qv, _TN, preferred_element_type=F32)
        dq_tile = jnp.dot(ds, kc, preferred_element_type=F32)
        rows = pl.ds(pl.multiple_of(qi * tq, tq), tq)

        @pl.when(ki == 0)
        def _():
            dq_ref[rows, :] = dq_tile

        @pl.when(ki > 0)
        def _():
            dq_ref[rows, :] += dq_tile

        @pl.when(qi == nq - 1)
        def _():
            dkv_ref[:, :HEAD] = dkc_sc[:, :HEAD].astype(BF16)
            dkv_ref[:, HEAD:] = dv_sc[...].astype(BF16)
            dkp_ref[...] = dkc_sc[:, HEAD:]

    return pl.pallas_call(
        body, name=f"mla_bwd_{l}", grid=(H, S // tk, nq),
        in_specs=[pl.BlockSpec((tq, 256), lambda h, k, i: (i, h)),
                  pl.BlockSpec((tk, HEAD), lambda h, k, i: (k, 2 * h)),
                  pl.BlockSpec((tk, HEAD), lambda h, k, i: (k, 2 * h + 1)),
                  pl.BlockSpec((tk, 128), lambda h, k, i: (k, 0)),
                  pl.BlockSpec((tq, HEAD), lambda h, k, i: (i, h)),
                  pl.BlockSpec((None, tq, 1), lambda h, k, i: (h, i, 0)),
                  pl.BlockSpec((None, tq, 1), lambda h, k, i: (h, i, 0))],
        out_specs=[pl.BlockSpec((S, 256), lambda h, k, i: (0, h)),
                   pl.BlockSpec((tk, 256), lambda h, k, i: (k, h)),
                   pl.BlockSpec((None, tk, 128), lambda h, k, i: (h, k, 0))],
        out_shape=[_sds((S, H * 256), F32), _sds((S, H * 256), BF16), _sds((H, S, 128), F32)],
        scratch_shapes=[pltpu.VMEM((tk, 256), F32), pltpu.VMEM((tk, HEAD), F32)],
        compiler_params=_cparams(("parallel", "arbitrary", "arbitrary")),
    )(q, kv, kv, kp, do, lse, delta)


NA_RB = 8


def _na_bias_index(rows):
    j = np.arange(NA_KH)
    dy = j[None, :] - (np.arange(8)[:, None] - 4) + 3
    c = np.arange(GRID_W)
    col_start = np.clip(c - NA_KW // 2, 0, GRID_W - NA_KW)
    ok = (c[None, :] >= col_start[:, None]) & (c[None, :] < col_start[:, None] + NA_KW)
    dx = np.clip(c[None, :] - c[:, None], -(NA_KW - 1), NA_KW - 1) + (NA_KW - 1)
    dy_full = np.broadcast_to(dy[:, None, :, None], (8, GRID_W, NA_KH, GRID_W)).reshape(8, GRID_W, NA_KH * GRID_W)
    dx_full = np.broadcast_to(dx[None, :, None, :], (8, GRID_W, NA_KH, GRID_W)).reshape(8, GRID_W, NA_KH * GRID_W)
    ok_full = np.broadcast_to(ok[None, :, None, :], (8, GRID_W, NA_KH, GRID_W)).reshape(8, GRID_W, NA_KH * GRID_W)
    valid = ok_full & (dy_full >= 0) & (dy_full <= 2 * NA_KH - 2)
    return np.clip(dy_full, 0, 2 * NA_KH - 2), dx_full, valid


def _na_row_window(rb, i, rows):
    r = rb * NA_RB + i
    ks = jnp.clip(r - NA_KH // 2, 0, rows - NA_KH)
    variant = r - ks
    return pl.ds(pl.multiple_of(ks * GRID_W, GRID_W), NA_KH * GRID_W), variant


def _na_fwd(l, proj, brow, off_q, off_k, off_v, scale):
    S = proj.shape[0]
    H = brow.shape[0]
    rows = S // GRID_W
    assert rows % NA_RB == 0 and rows >= NA_KH
    tq = NA_RB * GRID_W

    def body(q_ref, k_ref, v_ref, b_ref, o_ref):
        rb = pl.program_id(1)
        for i in range(NA_RB):
            win, variant = _na_row_window(rb, i, rows)
            qs = slice(i * GRID_W, (i + 1) * GRID_W)
            s = lax.dot_general(q_ref[qs, :], k_ref[win, :], _NT, preferred_element_type=F32) * scale + b_ref[variant]
            e = jnp.exp(s - jnp.max(s, axis=-1, keepdims=True))
            p = e / jnp.sum(e, axis=-1, keepdims=True)
            o_ref[qs, :] = jnp.dot(p.astype(BF16), v_ref[win, :], preferred_element_type=F32).astype(BF16)

    return pl.pallas_call(
        body, name=f"na_fwd_{l}", grid=(H, rows // NA_RB),
        in_specs=[pl.BlockSpec((tq, HEAD), lambda h, r: (r, off_q // HEAD + h)),
                  pl.BlockSpec((S, HEAD), lambda h, r: (0, off_k // HEAD + h)),
                  pl.BlockSpec((S, HEAD), lambda h, r: (0, off_v // HEAD + h)),
                  pl.BlockSpec((None, 8, GRID_W, NA_KH * GRID_W), lambda h, r: (h, 0, 0, 0))],
        out_specs=pl.BlockSpec((tq, HEAD), lambda h, r: (r, h)),
        out_shape=_sds((S, H * HEAD), BF16),
        compiler_params=_cparams(("parallel", "arbitrary")),
    )(proj, proj, proj, brow)


def _na_bwd(l, proj, brow, o, do, off_q, off_k, off_v, scale):
    S = proj.shape[0]
    H = brow.shape[0]
    rows = S // GRID_W
    tq = NA_RB * GRID_W

    def body(q_ref, k_ref, v_ref, b_ref, o_ref, do_ref, dq_ref, dk_ref, dv_ref, db_ref):
        rb = pl.program_id(1)

        @pl.when(rb == 0)
        def _():
            dk_ref[...] = jnp.zeros_like(dk_ref)
            dv_ref[...] = jnp.zeros_like(dv_ref)
            db_ref[...] = jnp.zeros_like(db_ref)

        for i in range(NA_RB):
            win, variant = _na_row_window(rb, i, rows)
            qs = slice(i * GRID_W, (i + 1) * GRID_W)
            qv, kw, vw, dov = q_ref[qs, :], k_ref[win, :], v_ref[win, :], do_ref[qs, :]
            s = lax.dot_general(qv, kw, _NT, preferred_element_type=F32) * scale + b_ref[variant]
            e = jnp.exp(s - jnp.max(s, axis=-1, keepdims=True))
            p = e / jnp.sum(e, axis=-1, keepdims=True)
            dv_ref[win, :] += lax.dot_general(p.astype(BF16), dov, _TN, preferred_element_type=F32)
            dp = lax.dot_general(dov, vw, _NT, preferred_element_type=F32)
            dl = jnp.sum(dov.astype(F32) * o_ref[qs, :].astype(F32), axis=-1, keepdims=True)
            ds = p * (dp - dl)
            db_ref[variant] += ds
            dsb = (ds * scale).astype(BF16)
            dq_ref[qs, :] = jnp.dot(dsb, kw, preferred_element_type=F32).astype(BF16)
            dk_ref[win, :] += lax.dot_general(dsb, qv, _TN, preferred_element_type=F32)

    qblk = pl.BlockSpec((tq, HEAD), lambda h, r: (r, h))
    full = pl.BlockSpec((S, HEAD), lambda h, r: (0, h))
    bias = pl.BlockSpec((None, 8, GRID_W, NA_KH * GRID_W), lambda h, r: (h, 0, 0, 0))
    return pl.pallas_call(
        body, name=f"na_bwd_{l}", grid=(H, rows // NA_RB),
        in_specs=[pl.BlockSpec((tq, HEAD), lambda h, r: (r, off_q // HEAD + h)),
                  pl.BlockSpec((S, HEAD), lambda h, r: (0, off_k // HEAD + h)),
                  pl.BlockSpec((S, HEAD), lambda h, r: (0, off_v // HEAD + h)),
                  bias, qblk, qblk],
        out_specs=[qblk, full, full, bias],
        out_shape=[_sds((S, H * HEAD), BF16), _sds((S, H * HEAD), F32), _sds((S, H * HEAD), F32),
                   _sds(brow.shape, F32)],
        compiler_params=_cparams(("parallel", "arbitrary")),
    )(proj, proj, proj, brow, o, do)


def _place():
    return lax.axis_index("x"), lax.axis_index("y"), lax.axis_index("c")


def _weight_gather(wflat):
    R, W = wflat.shape
    assert R % 32 == 0
    Rh = R // 2

    def body(w_ref, out_ref, send_sems, recv_sems, local_sem):
        x, y, c = _place()
        sibling = (x, y, 1 - c)
        chips = [(1 - x, y), (x, 1 - y), (1 - x, 1 - y)]

        def half(cx, cy, hc):
            return out_ref.at[2 * cx + cy, pl.ds(pl.multiple_of(hc * Rh, 16), Rh), :]

        def copy(k, src, dst, to):
            return pltpu.make_async_remote_copy(src_ref=src, dst_ref=dst, send_sem=send_sems.at[k],
                                                recv_sem=recv_sems.at[k], device_id=to, device_id_type=MESH)

        mine = pltpu.make_async_copy(w_ref, out_ref.at[2 * x + y], local_sem)
        mine.start()
        my_half = w_ref.at[pl.ds(pl.multiple_of(c * Rh, 16), Rh), :]
        first = [copy(j, my_half, half(x, y, c), (*chip, c)) for j, chip in enumerate(chips)]
        for cp in first:
            cp.start()
        passed = [copy(3 + j, half(*chip, c), half(*chip, c), sibling) for j, chip in enumerate(chips)]
        for j, chip in enumerate(chips):
            copy(j, my_half, half(*chip, c), (*chip, c)).wait_recv()
            passed[j].start()
        for j, chip in enumerate(chips):
            copy(3 + j, my_half, half(*chip, 1 - c), sibling).wait_recv()
        for cp in first + passed:
            cp.wait_send()
        mine.wait()

    return pl.pallas_call(
        body, name="weight_gather",
        in_specs=[pl.BlockSpec(memory_space=pl.ANY)],
        out_specs=pl.BlockSpec(memory_space=pl.ANY),
        out_shape=_sds((N_CHIPS, R, W), wflat.dtype),
        scratch_shapes=[pltpu.SemaphoreType.DMA((6,)), pltpu.SemaphoreType.DMA((6,)), pltpu.SemaphoreType.DMA],
    )(wflat)


def _grad_scatter(gflat):
    _, R, W = gflat.shape

    def body(g_ref, own_ref, got_ref, send_sems, recv_sems, local_sem):
        x, y, c = _place()
        chips = [(1 - x, y), (x, 1 - y), (1 - x, 1 - y)]
        mine = pltpu.make_async_copy(g_ref.at[2 * x + y], own_ref, local_sem)
        mine.start()
        sends = [pltpu.make_async_remote_copy(
            src_ref=g_ref.at[2 * cx + cy], dst_ref=got_ref.at[j], send_sem=send_sems.at[j],
            recv_sem=recv_sems.at[j], device_id=(cx, cy, c), device_id_type=MESH) for j, (cx, cy) in enumerate(chips)]
        for cp in sends:
            cp.start()
        for cp in sends:
            cp.wait_recv()
        for cp in sends:
            cp.wait_send()
        mine.wait()

    return pl.pallas_call(
        body, name="grad_scatter",
        in_specs=[pl.BlockSpec(memory_space=pl.ANY)],
        out_specs=[pl.BlockSpec(memory_space=pl.ANY), pl.BlockSpec(memory_space=pl.ANY)],
        out_shape=[_sds((R, W), gflat.dtype), _sds((3, R, W), gflat.dtype)],
        scratch_shapes=[pltpu.SemaphoreType.DMA((3,)), pltpu.SemaphoreType.DMA((3,)), pltpu.SemaphoreType.DMA],
    )(gflat)


def _grad_partial(own, got):
    R, W = own.shape
    tm = 16
    while R % (tm * 2) == 0 and tm < 256:
        tm *= 2

    def body(own_ref, got_ref, out_ref):
        acc = own_ref[...].astype(F32)
        for j in range(3):
            acc = acc + got_ref[j].astype(F32)
        out_ref[...] = acc

    return pl.pallas_call(
        body, name="grad_partial", grid=(R // tm,),
        in_specs=[pl.BlockSpec((tm, W), lambda i: (i, 0)), pl.BlockSpec((3, tm, W), lambda i: (0, i, 0))],
        out_specs=pl.BlockSpec((tm, W), lambda i: (i, 0)),
        out_shape=_sds((R, W), F32),
        compiler_params=_cparams(("parallel",)),
    )(own, got)


def _sibling_swap(part):
    def body(p_ref, got_ref, send_sem, recv_sem):
        x, y, c = _place()
        cp = pltpu.make_async_remote_copy(src_ref=p_ref, dst_ref=got_ref, send_sem=send_sem, recv_sem=recv_sem,
                                          device_id=(x, y, 1 - c), device_id_type=MESH)
        cp.start()
        cp.wait()

    return pl.pallas_call(
        body, name="sibling_swap",
        in_specs=[pl.BlockSpec(memory_space=pl.ANY)],
        out_specs=pl.BlockSpec(memory_space=pl.ANY),
        out_shape=_sds(part.shape, part.dtype),
        scratch_shapes=[pltpu.SemaphoreType.DMA, pltpu.SemaphoreType.DMA],
    )(part)


def _small_allreduce(vec):
    NR, W = vec.shape

    def body(v_ref, all_ref, sum_ref, send_sems, recv_sems):
        x, y, c = _place()
        me = 4 * x + 2 * y + c
        all_ref[me] = v_ref[...]
        copies = []
        for k in range(1, N_DEV):
            fx, fy, fc = (k >> 2) & 1, (k >> 1) & 1, k & 1
            peer = (x ^ fx, y ^ fy, c ^ fc)
            copies.append(pltpu.make_async_remote_copy(
                src_ref=v_ref, dst_ref=all_ref.at[me], send_sem=send_sems.at[k - 1], recv_sem=recv_sems.at[k - 1],
                device_id=peer, device_id_type=MESH))
        for cp in copies:
            cp.start()
        for cp in copies:
            cp.wait_recv()
        for cp in copies:
            cp.wait_send()
        acc = all_ref[0]
        for d in range(1, N_DEV):
            acc = acc + all_ref[d]
        sum_ref[...] = acc

    return pl.pallas_call(
        body, name="small_allreduce",
        in_specs=[pl.BlockSpec(memory_space=pltpu.VMEM)],
        out_specs=[pl.BlockSpec(memory_space=pltpu.VMEM), pl.BlockSpec(memory_space=pltpu.VMEM)],
        out_shape=[_sds((N_DEV, NR, W), F32), _sds((NR, W), F32)],
        scratch_shapes=[pltpu.SemaphoreType.DMA((N_DEV - 1,)), pltpu.SemaphoreType.DMA((N_DEV - 1,))],
    )(vec)[1]


def _adamw_math(g, w, m, v):
    m = ADAM_B1 * m + (1.0 - ADAM_B1) * g
    v = ADAM_B2 * v + (1.0 - ADAM_B2) * (g * g)
    m_hat = m / (1.0 - ADAM_B1 ** ADAM_STEP)
    v_hat = v / (1.0 - ADAM_B2 ** ADAM_STEP)
    delta = -ADAM_LR * (m_hat / (jnp.sqrt(v_hat) + ADAM_EPS) + ADAM_WD * w)
    return delta, m, v


def _adamw(name, ga, gb, w, m, v):
    rows, n = w.shape
    tm = _tile(rows, max(8, (1 << 18) // n // 8 * 8), 8)

    def body(ga_ref, gb_ref, w_ref, m_ref, v_ref, g_out, d_out, m_out, v_out):
        g = ga_ref[...] + gb_ref[...]
        delta, mn, vn = _adamw_math(g, w_ref[...], m_ref[...], v_ref[...])
        g_out[...] = g
        d_out[...] = delta
        m_out[...] = mn
        v_out[...] = vn

    blk = pl.BlockSpec((tm, n), lambda i: (i, 0))
    return pl.pallas_call(
        body, name=name, grid=(rows // tm,),
        in_specs=[blk] * 5, out_specs=[blk] * 4, out_shape=[_sds((rows, n), F32)] * 4,
        compiler_params=_cparams(("parallel",)),
    )(ga, gb, w, m, v)


def _pack(parts):
    flat = jnp.concatenate([p.reshape(-1) for p in parts])
    pad = (-flat.shape[0]) % 1024
    if pad:
        flat = jnp.concatenate([flat, jnp.zeros((pad,), flat.dtype)])
    return flat.reshape(-1, 128)


def _unpack(flat, shapes):
    flat = flat.reshape(-1)
    out, off = [], 0
    for s in shapes:
        n = int(np.prod(s))
        out.append(flat[off:off + n].reshape(s))
        off += n
    return out


def kernel(x, norm_mix, w_in, norm_qa, w_uq, norm_kva, w_ukv, rpb, w_o_mla, w_o_na, w_out, norm_mlp, w_ff1, w_ff2, norm_final, loss_target, m_norm_mix, m_w_in, m_norm_qa, m_w_uq, m_norm_kva, m_w_ukv, m_rpb, m_w_o_mla, m_w_o_na, m_w_out, m_norm_mlp, m_w_ff1, m_w_ff2, m_norm_final, v_norm_mix, v_w_in, v_norm_qa, v_w_uq, v_norm_kva, v_w_ukv, v_rpb, v_w_o_mla, v_w_o_na, v_w_out, v_norm_mlp, v_w_ff1, v_w_ff2, v_norm_final):
    wts = dict(norm_mix=norm_mix, w_in=w_in, norm_qa=norm_qa, w_uq=w_uq, norm_kva=norm_kva, w_ukv=w_ukv, rpb=rpb,
               w_o_mla=w_o_mla, w_o_na=w_o_na, w_out=w_out, norm_mlp=norm_mlp, w_ff1=w_ff1, w_ff2=w_ff2,
               norm_final=norm_final)
    mom = dict(norm_mix=m_norm_mix, w_in=m_w_in, norm_qa=m_norm_qa, w_uq=m_w_uq, norm_kva=m_norm_kva, w_ukv=m_w_ukv,
               rpb=m_rpb, w_o_mla=m_w_o_mla, w_o_na=m_w_o_na, w_out=m_w_out, norm_mlp=m_norm_mlp, w_ff1=m_w_ff1,
               w_ff2=m_w_ff2, norm_final=m_norm_final)
    var = dict(norm_mix=v_norm_mix, w_in=v_w_in, norm_qa=v_norm_qa, w_uq=v_w_uq, norm_kva=v_norm_kva, w_ukv=v_w_ukv,
               rpb=v_rpb, w_o_mla=v_w_o_mla, w_o_na=v_w_o_na, w_out=v_w_out, norm_mlp=v_norm_mlp, w_ff1=v_w_ff1,
               w_ff2=v_w_ff2, norm_final=v_norm_final)

    _, S, D = x.shape
    L = w_in.shape[0]
    QL, KL = norm_qa.shape[1], norm_kva.shape[1]
    H = w_uq.shape[2] * N_CHIPS // (HEAD + ROPE)
    NAW = w_o_na.shape[1]
    NH = NAW // HEAD
    rows = S // GRID_W
    x = x.reshape(S, D)
    target = loss_target.reshape(S, D)

    wflat = jnp.concatenate([wts[n].astype(BF16).reshape(-1) for n in BIG]).reshape(-1, 1024)
    R = wflat.shape[0]
    gathered = _weight_gather(wflat).reshape(N_CHIPS, R * 1024)
    full, off = {}, 0
    for n in BIG:
        shp = wts[n].shape
        cnt = int(np.prod(shp))
        seg = gathered[:, off:off + cnt].reshape((N_CHIPS,) + shp)
        off += cnt
        if n in ROW_SHARDED:
            full[n] = seg.transpose(1, 0, 2, 3).reshape(L, N_CHIPS * shp[1], shp[2])
        else:
            full[n] = seg.transpose(1, 2, 0, 3).reshape(L, shp[1], N_CHIPS * shp[2])

    widths = (QL, KL, ROPE, NAW, NAW, NAW, D, D)
    starts = np.concatenate([[0], np.cumsum(widths)]).astype(int)
    order = (6, 7, 3, 4, 5, 0, 1, 2)
    w_in_p = jnp.concatenate([full["w_in"][:, :, starts[i]:starts[i + 1]] for i in order]
                             + [jnp.zeros((L, D, 128 - ROPE), BF16)], axis=2)
    new_off = np.concatenate([[0], np.cumsum([widths[i] for i in order])]).astype(int)
    off_ga, off_gb, off_q, off_k, off_v, off_cq, off_ckv, off_kpe = (int(o) for o in new_off[:8])
    PW = w_in_p.shape[2]
    w_uq_p = jnp.pad(full["w_uq"].reshape(L, QL, H, HEAD + ROPE), ((0, 0), (0, 0), (0, 0), (0, 256 - HEAD - ROPE)))
    w_uq_p = w_uq_p.reshape(L, QL, H * 256)

    pos = jnp.arange(S, dtype=F32)
    inv_freq = 1.0 / (ROPE_THETA ** (jnp.arange(0, ROPE, 2, dtype=F32) / ROPE))
    ang = pos[:, None] * inv_freq[None, :]
    cos, sin, zero = jnp.cos(ang), jnp.sin(ang), jnp.zeros((S, 128 - ROPE), F32)
    cos_t = jnp.concatenate([cos, cos, zero], axis=1)
    sin_t = jnp.concatenate([-sin, sin, zero], axis=1)

    dy_idx, dx_idx, bias_ok = _na_bias_index(rows)
    mla_scale = float((HEAD + ROPE) ** -0.5)
    na_scale = float(HEAD ** -0.5)

    def rope_q_epilogue(acc, extra, outs):
        cv, sv = extra[0][...], extra[1][...]
        for hh in range(acc.shape[1] // 256):
            outs[0][:, hh * 256:hh * 256 + 128] = acc[:, hh * 256:hh * 256 + 128].astype(BF16)
            outs[0][:, hh * 256 + 128:(hh + 1) * 256] = _rope128(acc[:, hh * 256 + 128:(hh + 1) * 256], cv, sv).astype(BF16)

    def store_f32(acc, extra, outs):
        outs[0][...] = acc

    def merge_epilogue(acc, extra, outs):
        ga, gb, ya = extra[0][...].astype(F32), extra[1][...].astype(F32), extra[2][...]
        outs[0][...] = (jax.nn.sigmoid(ga) * ya + jax.nn.sigmoid(gb) * acc).astype(BF16)
        outs[1][...] = acc

    def residual_epilogue(acc, extra, outs):
        outs[0][...] = extra[0][...] + acc

    def ff1_epilogue(acc, extra, outs):
        outs[0][...] = acc.astype(BF16)
        outs[1][...] = jnp.square(jnp.maximum(acc, 0.0)).astype(BF16)

    def dff_epilogue(acc, extra, outs):
        outs[0][...] = (acc * (2.0 * jnp.maximum(extra[0][...].astype(F32), 0.0))).astype(BF16)

    def dmerge_epilogue(acc, extra, outs):
        ga, gb = extra[0][...].astype(F32), extra[1][...].astype(F32)
        ya, yb = extra[2][...], extra[3][...]
        sa, sb = jax.nn.sigmoid(ga), jax.nn.sigmoid(gb)
        outs[0][...] = (acc * sa).astype(BF16)
        outs[1][...] = (acc * sb).astype(BF16)
        outs[2][...] = (acc * ya * sa * (1.0 - sa)).astype(BF16)
        outs[3][...] = (acc * yb * sb * (1.0 - sb)).astype(BF16)

    saved = []
    for l in range(L):
        u, r1 = _rms_fwd(f"rms_mix_{l}", x, wts["norm_mix"][l][None])
        proj, = _mm(f"proj_{l}", u, w_in_p, "nn", [_sds((S, PW), BF16)], lb=l, tk_cap=2048)
        nq, nkv, rq, rkv, kp = _lat_fwd(l, proj, wts["norm_qa"][l][None], wts["norm_kva"][l][None], cos_t, sin_t,
                                        off_cq, off_ckv, off_kpe)
        q, = _mm(f"q_up_{l}", nq, w_uq_p, "nn", [_sds((S, H * 256), BF16)], lb=l,
                 extras=(cos_t, sin_t), extra_specs=(_row_spec(128), _row_spec(128)), epilogue=rope_q_epilogue, tn_cap=512)
        kv, = _mm(f"kv_up_{l}", nkv, full["w_ukv"], "nn", [_sds((S, H * 256), BF16)], lb=l)
        o_a, lse = _mla_fwd(l, q, kv, kp, mla_scale)
        brow = jnp.where(bias_ok[None], wts["rpb"][l][:, dy_idx, dx_idx], NEG)
        o_b = _na_fwd(l, proj, brow, off_q, off_k, off_v, na_scale)
        y_a, = _mm(f"o_mla_{l}", o_a, full["w_o_mla"], "nn", [_sds((S, D), F32)], lb=l, epilogue=store_f32,
                   tm_cap=512, tn_cap=512)
        merged, y_b = _mm(f"o_na_merge_{l}", o_b, full["w_o_na"], "nn", [_sds((S, D), BF16), _sds((S, D), F32)], lb=l,
                          extras=(proj, proj, y_a), extra_specs=(_col_spec(off_ga), _col_spec(off_gb), _tile_spec),
                          epilogue=merge_epilogue, tm_cap=512, tn_cap=512)
        x2, = _mm(f"w_out_{l}", merged, full["w_out"], "nn", [_sds((S, D), F32)], lb=l,
                  extras=(x,), extra_specs=(_tile_spec,), epilogue=residual_epilogue, tm_cap=512)
        u2, r2 = _rms_fwd(f"rms_mlp_{l}", x2, wts["norm_mlp"][l][None])
        h, a = _mm(f"ff1_{l}", u2, full["w_ff1"], "nn", [_sds((S, 4 * D), BF16), _sds((S, 4 * D), BF16)], lb=l,
                   epilogue=ff1_epilogue, tk_cap=2048)
        x3, = _mm(f"ff2_{l}", a, full["w_ff2"], "nn", [_sds((S, D), F32)], lb=l,
                  extras=(x2,), extra_specs=(_tile_spec,), epilogue=residual_epilogue, tm_cap=512)
        saved.append(dict(x=x, r1=r1, u=u, proj=proj, nq=nq, nkv=nkv, rq=rq, rkv=rkv, q=q, kv=kv, kp=kp, o_a=o_a,
                          lse=lse, brow=brow, o_b=o_b, y_a=y_a, y_b=y_b, merged=merged, x2=x2, r2=r2, u2=u2, h=h, a=a))
        x = x3

    loss_lanes, dx, dxb, dg_final = _final_loss(x, target, wts["norm_final"][None])
    loss = lax.psum(loss_lanes[0, 0], ("x", "y", "c"))

    gbig = {n: [None] * L for n in BIG}
    gsmall = {n: [None] * L for n in SMALL if n != "norm_final"}
    oh_dy = jnp.asarray(dy_idx[:, 0, :, None] == np.arange(2 * NA_KH - 1), F32)
    oh_dx = jnp.asarray((dx_idx[0, :, :, None] == np.arange(2 * NA_KW - 1)) & bias_ok[0, :, :, None], F32)
    for l in reversed(range(L)):
        sv = saved[l]
        proj = sv["proj"]
        dh, = _mm(f"d_ff2_{l}", dxb, full["w_ff2"], "nt", [_sds((S, 4 * D), BF16)], lb=l,
                  extras=(sv["h"],), extra_specs=(_tile_spec,), epilogue=dff_epilogue, tk_cap=2048)
        gbig["w_ff2"][l], = _mm(f"g_ff2_{l}", sv["a"], dxb, "tn", [_sds((4 * D, D), BF16)], tn_cap=2048)
        du2, = _mm(f"d_ff1_{l}", dh, full["w_ff1"], "nt", [_sds((S, D), F32)], lb=l, epilogue=store_f32)
        gbig["w_ff1"][l], = _mm(f"g_ff1_{l}", sv["u2"], dh, "tn", [_sds((D, 4 * D), BF16)], tn_cap=2048)
        dx2, dx2b, gsmall["norm_mlp"][l] = _rms_bwd(f"rms_mlp_bwd_{l}", sv["x2"], sv["r2"], wts["norm_mlp"][l][None], du2, dx)
        dya, dyb, dga, dgb = _mm(
            f"d_w_out_{l}", dx2b, full["w_out"], "nt", [_sds((S, D), BF16)] * 4, lb=l,
            extras=(proj, proj, sv["y_a"], sv["y_b"]),
            extra_specs=(_col_spec(off_ga), _col_spec(off_gb), _tile_spec, _tile_spec),
            epilogue=dmerge_epilogue, tm_cap=512, tn_cap=512, tk_cap=2048)
        gbig["w_out"][l], = _mm(f"g_w_out_{l}", sv["merged"], dx2b, "tn", [_sds((D, D), BF16)], tn_cap=2048)
        do_a, = _mm(f"d_o_mla_{l}", dya, full["w_o_mla"], "nt", [_sds((S, H * HEAD), BF16)], lb=l, tk_cap=2048)
        gbig["w_o_mla"][l], = _mm(f"g_o_mla_{l}", sv["o_a"], dya, "tn", [_sds((H * HEAD, D), BF16)], tn_cap=2048)
        do_b, = _mm(f"d_o_na_{l}", dyb, full["w_o_na"], "nt", [_sds((S, NAW), BF16)], lb=l, tk_cap=2048)
        gbig["w_o_na"][l], = _mm(f"g_o_na_{l}", sv["o_b"], dyb, "tn", [_sds((NAW, D), BF16)], tn_cap=2048)
        dq_na, dk_na, dv_na, dbrow = _na_bwd(l, proj, sv["brow"], sv["o_b"], do_b, off_q, off_k, off_v, na_scale)
        tmp = jnp.einsum("hoqn,qnx->honx", dbrow, oh_dx, precision=lax.Precision.HIGHEST)
        gsmall["rpb"][l] = jnp.einsum("honx,ony->hyx", tmp, oh_dy, precision=lax.Precision.HIGHEST)
        dl = _delta(l, sv["o_a"], do_a)
        dq_f, dkv, dkp_h = _mla_bwd(l, sv["q"], sv["kv"], sv["kp"], do_a, sv["lse"], dl, mla_scale)
        dq = _rope_bwd_q(l, dq_f, cos_t, sin_t)
        dnq, = _mm(f"d_q_up_{l}", dq, w_uq_p, "nt", [_sds((S, QL), F32)], lb=l, epilogue=store_f32, tk_cap=2048)
        g_uq, = _mm(f"g_q_up_{l}", sv["nq"], dq, "tn", [_sds((QL, H * 256), BF16)], tn_cap=2048)
        gbig["w_uq"][l] = g_uq.reshape(QL, H, 256)[:, :, :HEAD + ROPE].reshape(QL, H * (HEAD + ROPE))
        dnkv, = _mm(f"d_kv_up_{l}", dkv, full["w_ukv"], "nt", [_sds((S, KL), F32)], lb=l, epilogue=store_f32, tk_cap=2048)
        gbig["w_ukv"][l], = _mm(f"g_kv_up_{l}", sv["nkv"], dkv, "tn", [_sds((KL, H * 256), BF16)], tn_cap=2048)
        dcq, dckv, dkpe, gsmall["norm_qa"][l], gsmall["norm_kva"][l] = _lat_bwd(
            l, proj, sv["rq"], sv["rkv"], wts["norm_qa"][l][None], wts["norm_kva"][l][None], dnq, dnkv, dkp_h,
            cos_t, sin_t, off_cq, off_ckv)
        dproj = jnp.concatenate([dga, dgb, dq_na, dk_na.astype(BF16), dv_na.astype(BF16), dcq, dckv, dkpe], axis=1)
        du, = _mm(f"d_proj_{l}", dproj, w_in_p, "nt", [_sds((S, D), F32)], lb=l, epilogue=store_f32)
        g_in, = _mm(f"g_proj_{l}", sv["u"], dproj, "tn", [_sds((D, PW), BF16)], tn_cap=2048)
        back = [None] * 8
        for pos_new, i in enumerate(order):
            back[i] = g_in[:, new_off[pos_new]:new_off[pos_new] + widths[i]]
        gbig["w_in"][l] = jnp.concatenate(back, axis=1)
        dx, dxb, gsmall["norm_mix"][l] = _rms_bwd(f"rms_mix_bwd_{l}", sv["x"], sv["r1"], wts["norm_mix"][l][None], du, dx2)
    grad_x = dx.reshape(1, S, D)

    slabs = []
    for n in BIG:
        g = jnp.stack(gbig[n])
        shp = wts[n].shape
        if n in ROW_SHARDED:
            g = g.reshape(L, N_CHIPS, shp[1], shp[2]).transpose(1, 0, 2, 3)
        else:
            g = g.reshape(L, shp[1], N_CHIPS, shp[2]).transpose(2, 0, 1, 3)
        slabs.append(g.reshape(N_CHIPS, -1))
    gflat = jnp.concatenate(slabs, axis=1).reshape(N_CHIPS, R, 1024)
    own, got = _grad_scatter(gflat)
    part = _grad_partial(own, got)
    other = _sibling_swap(part)
    big_shapes = [wts[n].shape for n in BIG]
    part_w = dict(zip(BIG, _unpack(part, big_shapes)))
    other_w = dict(zip(BIG, _unpack(other, big_shapes)))

    small_shapes = [wts[n].shape for n in SMALL]
    small_g = [jnp.stack([g.reshape(wts[n].shape[1:]) for g in gsmall[n]]) for n in SMALL if n != "norm_final"]
    small_g.append(dg_final.reshape(D))
    gsum = _small_allreduce(_pack(small_g))
    zeros = jnp.zeros_like(gsum)
    sg, sd, sm, svv = _adamw("adamw_small", gsum, zeros, _pack([wts[n] for n in SMALL]), _pack([mom[n] for n in SMALL]),
                             _pack([var[n] for n in SMALL]))
    res = {n: {} for n in WEIGHTS}
    for key, flat in (("g", sg), ("d", sd), ("m", sm), ("v", svv)):
        for n, arr in zip(SMALL, _unpack(flat, small_shapes)):
            res[n][key] = arr
    for n in BIG:
        shp = wts[n].shape
        two = lambda t: t.reshape(shp[0] * shp[1], shp[2])
        outs = _adamw(f"adamw_{n}", two(part_w[n]), two(other_w[n]), two(wts[n]), two(mom[n]), two(var[n]))
        for key, arr in zip(("g", "d", "m", "v"), outs):
            res[n][key] = arr.reshape(shp)

    return (loss, grad_x, *[res[n]["g"] for n in WEIGHTS], *[res[n]["d"] for n in WEIGHTS],
            *[res[n]["m"] for n in WEIGHTS], *[res[n]["v"] for n in WEIGHTS])
```

```python
import functools

import numpy as np
import jax
import jax.numpy as jnp
from jax import lax
from jax.experimental import pallas as pl
from jax.experimental.pallas import tpu as pltpu

F32 = jnp.float32
BF16 = jnp.bfloat16
MESH = pl.DeviceIdType.MESH

EPS = 1e-6
ROPE_THETA = 10000.0
ROPE = 64
HEAD = 128
GRID_W = 64
NA_KH = 8
NA_KW = 16
N_CHIPS = 4
N_DEV = 8
NEG = -1e30
LOG2E = 1.4426950408889634

ADAM_LR = 0.001
ADAM_B1 = 0.9
ADAM_B2 = 0.999
ADAM_EPS = 1e-08
ADAM_WD = 0.01
ADAM_STEP = 10

VMEM_LIMIT = 56 * 1024 * 1024

BIG = ("w_in", "w_uq", "w_ukv", "w_o_mla", "w_o_na", "w_out", "w_ff1", "w_ff2")
ROW_SHARDED = ("w_out", "w_ff2")
SMALL = ("norm_mix", "norm_qa", "norm_kva", "rpb", "norm_mlp", "norm_final")
WEIGHTS = ("norm_mix", "w_in", "norm_qa", "w_uq", "norm_kva", "w_ukv", "rpb", "w_o_mla", "w_o_na",
           "w_out", "norm_mlp", "w_ff1", "w_ff2", "norm_final")


def _cparams(sem, **kw):
    return pltpu.CompilerParams(dimension_semantics=sem, vmem_limit_bytes=VMEM_LIMIT, **kw)


def _tile(n, cap, unit=128):
    if n <= cap:
        return n
    best = None
    for t in range(unit, cap + 1, unit):
        if n % t == 0:
            best = t
    assert best is not None, (n, cap, unit)
    return best


def _sds(shape, dtype):
    return jax.ShapeDtypeStruct(shape, dtype)


_DIMS = {"nn": (((1,), (0,)), ((), ())), "nt": (((1,), (1,)), ((), ())), "tn": (((0,), (0,)), ((), ()))}


def _store_cast(acc, extra, outs):
    outs[0][...] = acc.astype(outs[0].dtype)


def _mm(name, a, b, mode, out_shapes, *, lb=None, extras=(), extra_specs=(), out_specs=None,
        epilogue=_store_cast, tm_cap=1024, tn_cap=1024, tk_cap=1024, into=None):
    bshape = b.shape[1:] if lb is not None else b.shape
    if mode == "nn":
        (M, K), (K2, N) = a.shape, bshape
    elif mode == "nt":
        (M, K), (N, K2) = a.shape, bshape
    else:
        (K, M), (K2, N) = a.shape, bshape
    assert K == K2, (name, a.shape, b.shape)
    tm, tn, tk = _tile(M, tm_cap), _tile(N, tn_cap), _tile(K, tk_cap)
    nk = K // tk
    if mode == "tn":
        a_spec = pl.BlockSpec((tk, tm), lambda i, j, k: (k, i))
    else:
        a_spec = pl.BlockSpec((tm, tk), lambda i, j, k: (i, k))
    bblk, bidx = ((tn, tk), lambda i, j, k: (j, k)) if mode == "nt" else ((tk, tn), lambda i, j, k: (k, j))
    if lb is not None:
        b_spec = pl.BlockSpec((None,) + bblk, lambda i, j, k: (lb,) + bidx(i, j, k))
    else:
        b_spec = pl.BlockSpec(bblk, bidx)
    aliases = {}
    if into is not None:
        stack, layer = into
        assert not extras and len(out_shapes) == 1 and stack.shape[1:] == (M, N)
        extras, extra_specs = (stack,), (lambda tm, tn: pl.BlockSpec(memory_space=pl.ANY),)
        out_shapes = [_sds(stack.shape, stack.dtype)]
        out_specs = [lambda tm, tn: pl.BlockSpec((None, tm, tn), lambda i, j, k: (layer, i, j))]
        aliases = {2: 0}
    ne, no = len(extras), len(out_shapes)
    if out_specs is None:
        out_specs = [lambda tm, tn: pl.BlockSpec((tm, tn), lambda i, j, k: (i, j))] * no
    dims = _DIMS[mode]

    def body(*refs):
        a_ref, b_ref = refs[0], refs[1]
        extra, outs, acc = refs[2:2 + ne], refs[2 + ne:2 + ne + no], refs[-1]
        k = pl.program_id(2)

        @pl.when(k == 0)
        def _():
            acc[...] = jnp.zeros_like(acc)

        acc[...] += lax.dot_general(a_ref[...], b_ref[...], dims, preferred_element_type=F32)

        @pl.when(k == nk - 1)
        def _():
            epilogue(acc[...], extra, outs)

    return pl.pallas_call(
        body, name=name, grid=(M // tm, N // tn, nk),
        in_specs=[a_spec, b_spec] + [s(tm, tn) for s in extra_specs],
        out_specs=[s(tm, tn) for s in out_specs],
        out_shape=list(out_shapes),
        scratch_shapes=[pltpu.VMEM((tm, tn), F32)],
        input_output_aliases=aliases,
        compiler_params=_cparams(("parallel", "parallel", "arbitrary")),
    )(a, b, *extras)


def _tile_spec(tm, tn):
    return pl.BlockSpec((tm, tn), lambda i, j, k: (i, j))


def _row_spec(width):
    return lambda tm, tn: pl.BlockSpec((tm, width), lambda i, j, k: (i, 0))


def _col_spec(off_cols):
    def make(tm, tn):
        assert off_cols % tn == 0, (off_cols, tn)
        return pl.BlockSpec((tm, tn), lambda i, j, k: (i, off_cols // tn + j))
    return make


def _rms_fwd(name, x, g):
    S, D = x.shape
    tm = _tile(S, 256, 8)

    def body(x_ref, g_ref, u_ref, r_ref):
        xv = x_ref[...]
        r = lax.rsqrt(jnp.mean(xv * xv, axis=-1, keepdims=True) + EPS)
        u_ref[...] = (xv * r * g_ref[...]).astype(BF16)
        r_ref[...] = r

    return pl.pallas_call(
        body, name=name, grid=(S // tm,),
        in_specs=[pl.BlockSpec((tm, D), lambda i: (i, 0)), pl.BlockSpec((1, D), lambda i: (0, 0))],
        out_specs=[pl.BlockSpec((tm, D), lambda i: (i, 0)), pl.BlockSpec((tm, 1), lambda i: (i, 0))],
        out_shape=[_sds((S, D), BF16), _sds((S, 1), F32)],
        compiler_params=_cparams(("parallel",)),
    )(x, g)


def _rms_bwd(name, x, r, g, du, dres):
    S, D = x.shape
    tm = _tile(S, 256, 8)

    def body(x_ref, r_ref, g_ref, du_ref, dres_ref, dx_ref, dxb_ref, dg_ref):
        rv = r_ref[...]
        xhat = x_ref[...] * rv
        duv = du_ref[...].astype(F32)
        dxh = duv * g_ref[...]
        m = jnp.mean(dxh * xhat, axis=-1, keepdims=True)
        dx = dres_ref[...] + rv * (dxh - xhat * m)
        dx_ref[...] = dx
        dxb_ref[...] = dx.astype(BF16)

        @pl.when(pl.program_id(0) == 0)
        def _():
            dg_ref[...] = jnp.zeros_like(dg_ref)

        dg_ref[...] += jnp.sum(duv * xhat, axis=0, keepdims=True)

    row = pl.BlockSpec((tm, D), lambda i: (i, 0))
    vec = pl.BlockSpec((1, D), lambda i: (0, 0))
    return pl.pallas_call(
        body, name=name, grid=(S // tm,),
        in_specs=[row, pl.BlockSpec((tm, 1), lambda i: (i, 0)), vec, row, row],
        out_specs=[row, row, vec],
        out_shape=[_sds((S, D), F32), _sds((S, D), BF16), _sds((1, D), F32)],
        compiler_params=_cparams(("arbitrary",)),
    )(x, r, g, du, dres)


def _final_loss(x, t, g):
    S, D = x.shape
    tm = _tile(S, 256, 8)

    def body(x_ref, t_ref, g_ref, loss_ref, dx_ref, dxb_ref, dg_ref):
        xv = x_ref[...]
        gv = g_ref[...]
        rv = lax.rsqrt(jnp.mean(xv * xv, axis=-1, keepdims=True) + EPS)
        xhat = xv * rv
        diff = xhat * gv - t_ref[...]
        dy = diff * (1.0 / D)
        dxh = dy * gv
        m = jnp.mean(dxh * xhat, axis=-1, keepdims=True)
        dx = rv * (dxh - xhat * m)
        dx_ref[...] = dx
        dxb_ref[...] = dx.astype(BF16)

        @pl.when(pl.program_id(0) == 0)
        def _():
            dg_ref[...] = jnp.zeros_like(dg_ref)
            loss_ref[...] = jnp.zeros_like(loss_ref)

        dg_ref[...] += jnp.sum(dy * xhat, axis=0, keepdims=True)
        per_row = jnp.mean(diff * diff, axis=-1, keepdims=True)
        loss_ref[...] += 0.5 * jnp.sum(per_row, axis=0, keepdims=True)

    row = pl.BlockSpec((tm, D), lambda i: (i, 0))
    vec = pl.BlockSpec((1, D), lambda i: (0, 0))
    return pl.pallas_call(
        body, name="final_loss", grid=(S // tm,),
        in_specs=[row, row, vec],
        out_specs=[pl.BlockSpec((1, 128), lambda i: (0, 0)), row, row, vec],
        out_shape=[_sds((1, 128), F32), _sds((S, D), F32), _sds((S, D), BF16), _sds((1, D), F32)],
        compiler_params=_cparams(("arbitrary",)),
    )(x, t, g)


def _rope128(v, cos_t, sin_t):
    lane = lax.broadcasted_iota(jnp.int32, v.shape, 1)
    up = pltpu.roll(v, 128 - ROPE // 2, 1)
    dn = pltpu.roll(v, ROPE // 2, 1)
    return v * cos_t + jnp.where(lane < ROPE // 2, up, dn) * sin_t


def _lat_fwd(l, proj, g_qa, g_kva, cos_t, sin_t, off_cq, off_ckv, off_kpe):
    S = proj.shape[0]
    QL, KL = g_qa.shape[1], g_kva.shape[1]
    tm = _tile(S, 512, 8)
    assert off_cq % QL == 0 and off_ckv % KL == 0 and off_kpe % 128 == 0

    def body(cq_ref, ckv_ref, kpe_ref, gq_ref, gkv_ref, cos_ref, sin_ref, nq_ref, nkv_ref, rq_ref, rkv_ref, kp_ref):
        for c_ref, g_ref, n_ref, r_ref in ((cq_ref, gq_ref, nq_ref, rq_ref), (ckv_ref, gkv_ref, nkv_ref, rkv_ref)):
            cv = c_ref[...].astype(F32)
            r = lax.rsqrt(jnp.mean(cv * cv, axis=-1, keepdims=True) + EPS)
            n_ref[...] = (cv * r * g_ref[...]).astype(BF16)
            r_ref[...] = r
        kp_ref[...] = _rope128(kpe_ref[...].astype(F32), cos_ref[...], sin_ref[...]).astype(BF16)

    col = lambda w, off: pl.BlockSpec((tm, w), lambda i: (i, off // w))
    row = lambda w: pl.BlockSpec((tm, w), lambda i: (i, 0))
    vec = lambda w: pl.BlockSpec((1, w), lambda i: (0, 0))
    return pl.pallas_call(
        body, name=f"lat_fwd_{l}", grid=(S // tm,),
        in_specs=[col(QL, off_cq), col(KL, off_ckv), col(128, off_kpe), vec(QL), vec(KL), row(128), row(128)],
        out_specs=[row(QL), row(KL), row(1), row(1), row(128)],
        out_shape=[_sds((S, QL), BF16), _sds((S, KL), BF16), _sds((S, 1), F32), _sds((S, 1), F32), _sds((S, 128), BF16)],
        compiler_params=_cparams(("parallel",)),
    )(proj, proj, proj, g_qa, g_kva, cos_t, sin_t)


def _lat_bwd(l, proj, rq, rkv, g_qa, g_kva, dnq, dnkv, dkp_h, cos_t, sin_t, off_cq, off_ckv):
    S = proj.shape[0]
    QL, KL = g_qa.shape[1], g_kva.shape[1]
    H = dkp_h.shape[0]
    tm = _tile(S, 512, 8)

    def body(cq_ref, ckv_ref, rq_ref, rkv_ref, gq_ref, gkv_ref, dnq_ref, dnkv_ref, dkp_ref, cos_ref, sin_ref,
             dcq_ref, dckv_ref, dkpe_ref, dgq_ref, dgkv_ref):
        first = pl.program_id(0) == 0
        for c_ref, r_ref, g_ref, dn_ref, dc_ref, dg_ref in (
                (cq_ref, rq_ref, gq_ref, dnq_ref, dcq_ref, dgq_ref),
                (ckv_ref, rkv_ref, gkv_ref, dnkv_ref, dckv_ref, dgkv_ref)):
            rv = r_ref[...]
            xhat = c_ref[...].astype(F32) * rv
            dn = dn_ref[...]
            dxh = dn * g_ref[...]
            m = jnp.mean(dxh * xhat, axis=-1, keepdims=True)
            dc_ref[...] = (rv * (dxh - xhat * m)).astype(BF16)

            @pl.when(first)
            def _():
                dg_ref[...] = jnp.zeros_like(dg_ref)

            dg_ref[...] += jnp.sum(dn * xhat, axis=0, keepdims=True)
        dkp = dkp_ref[0]
        for h in range(1, H):
            dkp = dkp + dkp_ref[h]
        dkpe_ref[...] = _rope128(dkp, cos_ref[...], -sin_ref[...]).astype(BF16)

    col = lambda w, off: pl.BlockSpec((tm, w), lambda i: (i, off // w))
    row = lambda w: pl.BlockSpec((tm, w), lambda i: (i, 0))
    vec = lambda w: pl.BlockSpec((1, w), lambda i: (0, 0))
    return pl.pallas_call(
        body, name=f"lat_bwd_{l}", grid=(S // tm,),
        in_specs=[col(QL, off_cq), col(KL, off_ckv), row(1), row(1), vec(QL), vec(KL), row(QL), row(KL),
                  pl.BlockSpec((H, tm, 128), lambda i: (0, i, 0)), row(128), row(128)],
        out_specs=[row(QL), row(KL), row(128), vec(QL), vec(KL)],
        out_shape=[_sds((S, QL), BF16), _sds((S, KL), BF16), _sds((S, 128), BF16), _sds((1, QL), F32), _sds((1, KL), F32)],
        compiler_params=_cparams(("arbitrary",)),
    )(proj, proj, rq, rkv, g_qa, g_kva, dnq, dnkv, dkp_h, cos_t, sin_t)


def _rope_bwd_q(l, dq, cos_t, sin_t):
    S, W = dq.shape
    tm = _tile(S, 256, 8)
    nh = W // 256

    def body(dq_ref, cos_ref, sin_ref, out_ref):
        cv, sv = cos_ref[...], -sin_ref[...]
        for h in range(nh):
            out_ref[:, h * 256:h * 256 + 128] = dq_ref[:, h * 256:h * 256 + 128].astype(BF16)
            out_ref[:, h * 256 + 128:(h + 1) * 256] = _rope128(dq_ref[:, h * 256 + 128:(h + 1) * 256], cv, sv).astype(BF16)

    return pl.pallas_call(
        body, name=f"rope_bwd_q_{l}", grid=(S // tm,),
        in_specs=[pl.BlockSpec((tm, W), lambda i: (i, 0)), pl.BlockSpec((tm, 128), lambda i: (i, 0)),
                  pl.BlockSpec((tm, 128), lambda i: (i, 0))],
        out_specs=pl.BlockSpec((tm, W), lambda i: (i, 0)),
        out_shape=_sds((S, W), BF16),
        compiler_params=_cparams(("parallel",)),
    )(dq, cos_t, sin_t)


def _delta(l, o, do):
    S, W = o.shape
    H = W // HEAD
    tm = _tile(S, 1024, 8)

    def body(o_ref, do_ref, d_ref):
        d_ref[...] = jnp.sum(o_ref[...].astype(F32) * do_ref[...].astype(F32), axis=-1, keepdims=True)

    blk = pl.BlockSpec((tm, HEAD), lambda h, i: (i, h))
    return pl.pallas_call(
        body, name=f"delta_{l}", grid=(H, S // tm),
        in_specs=[blk, blk],
        out_specs=pl.BlockSpec((None, tm, 1), lambda h, i: (h, i, 0)),
        out_shape=_sds((H, S, 1), F32),
        compiler_params=_cparams(("parallel", "parallel")),
    )(o, do)


_NT = (((1,), (1,)), ((), ()))
_TN = (((0,), (0,)), ((), ()))


def _mla_fwd(l, q, kv, kp, scale):
    S = q.shape[0]
    H = q.shape[1] // 256
    tq, tk = _tile(S, 1024, 8), _tile(S, 512, 128)
    nk = S // tk

    def body(q_ref, kn_ref, v_ref, kp_ref, o_ref, lse_ref, m_sc, acc_sc):
        ki = pl.program_id(2)

        @pl.when(ki == 0)
        def _():
            m_sc[...] = jnp.full_like(m_sc, NEG)
            acc_sc[...] = jnp.zeros_like(acc_sc)

        kc = jnp.concatenate([kn_ref[...], kp_ref[...]], axis=1)
        s = lax.dot_general(q_ref[...], kc, _NT, preferred_element_type=F32) * (scale * LOG2E)
        m_prev = m_sc[...]
        m_new = jnp.maximum(m_prev, jnp.max(s, axis=-1, keepdims=True))
        alpha = jnp.exp2(m_prev - m_new)
        p = jnp.exp2(s - jnp.tile(m_new, (1, tk // 128)))
        vx = jnp.concatenate([v_ref[...], jnp.ones((tk, 128), BF16)], axis=1)
        acc_sc[...] = jnp.tile(alpha, (1, 2)) * acc_sc[...] + jnp.dot(p.astype(BF16), vx, preferred_element_type=F32)
        m_sc[...] = m_new

        @pl.when(ki == nk - 1)
        def _():
            l = acc_sc[:, HEAD:]
            o_ref[...] = (acc_sc[:, :HEAD] / l).astype(BF16)
            lse_ref[...] = m_sc[:, :1] + jnp.log2(l[:, :1])

    return pl.pallas_call(
        body, name=f"mla_fwd_{l}", grid=(H, S // tq, nk),
        in_specs=[pl.BlockSpec((tq, 256), lambda h, i, k: (i, h)),
                  pl.BlockSpec((tk, HEAD), lambda h, i, k: (k, 2 * h)),
                  pl.BlockSpec((tk, HEAD), lambda h, i, k: (k, 2 * h + 1)),
                  pl.BlockSpec((tk, 128), lambda h, i, k: (k, 0))],
        out_specs=[pl.BlockSpec((tq, HEAD), lambda h, i, k: (i, h)),
                   pl.BlockSpec((None, tq, 1), lambda h, i, k: (h, i, 0))],
        out_shape=[_sds((S, H * HEAD), BF16), _sds((H, S, 1), F32)],
        scratch_shapes=[pltpu.VMEM((tq, 128), F32), pltpu.VMEM((tq, 2 * HEAD), F32)],
        compiler_params=_cparams(("parallel", "parallel", "arbitrary")),
    )(q, kv, kv, kp)


def _mla_bwd(l, q, kv, kp, do, lse, delta, scale):
    S = q.shape[0]
    H = q.shape[1] // 256
    tq, tk = _tile(S, 512, 8), _tile(S, 512, 128)
    nq = S // tq

    def body(q_ref, kn_ref, v_ref, kp_ref, do_ref, lse_ref, dl_ref, dq_ref, dkv_ref, dkp_ref, dkc_sc, dv_sc):
        ki, qi = pl.program_id(1), pl.program_id(2)

        @pl.when(qi == 0)
        def _():
            dkc_sc[...] = jnp.zeros_like(dkc_sc)
            dv_sc[...] = jnp.zeros_like(dv_sc)

        qv, dov = q_ref[...], do_ref[...]
        kc = jnp.concatenate([kn_ref[...], kp_ref[...]], axis=1)
        s = lax.dot_general(qv, kc, _NT, preferred_element_type=F32) * (scale * LOG2E)
        p = jnp.exp2(s - lse_ref[...])
        dv_sc[...] += lax.dot_general(p.astype(BF16), dov, _TN, preferred_element_type=F32)
        dp = lax.dot_general(dov, v_ref[...], _NT, preferred_element_type=F32)
        ds = (p * (dp - dl_ref[...]) * scale).astype(BF16)
        dkc_sc[...] += lax.dot_general(ds, qv, _TN, preferred_element_type=F32)
        dq_tile = jnp.dot(ds, kc, preferred_element_type=F32)
        rows = pl.ds(pl.multiple_of(qi * tq, tq), tq)

        @pl.when(ki == 0)
        def _():
            dq_ref[rows, :] = dq_tile

        @pl.when(ki > 0)
        def _():
            dq_ref[rows, :] += dq_tile

        @pl.when(qi == nq - 1)
        def _():
            dkv_ref[:, :HEAD] = dkc_sc[:, :HEAD].astype(BF16)
            dkv_ref[:, HEAD:] = dv_sc[...].astype(BF16)
            dkp_ref[...] = dkc_sc[:, HEAD:]

    return pl.pallas_call(
        body, name=f"mla_bwd_{l}", grid=(H, S // tk, nq),
        in_specs=[pl.BlockSpec((tq, 256), lambda h, k, i: (i, h)),
                  pl.BlockSpec((tk, HEAD), lambda h, k, i: (k, 2 * h)),
                  pl.BlockSpec((tk, HEAD), lambda h, k, i: (k, 2 * h + 1)),
                  pl.BlockSpec((tk, 128), lambda h, k, i: (k, 0)),
                  pl.BlockSpec((tq, HEAD), lambda h, k, i: (i, h)),
                  pl.BlockSpec((None, tq, 1), lambda h, k, i: (h, i, 0)),
                  pl.BlockSpec((None, tq, 1), lambda h, k, i: (h, i, 0))],
        out_specs=[pl.BlockSpec((S, 256), lambda h, k, i: (0, h)),
                   pl.BlockSpec((tk, 256), lambda h, k, i: (k, h)),
                   pl.BlockSpec((None, tk, 128), lambda h, k, i: (h, k, 0))],
        out_shape=[_sds((S, H * 256), F32), _sds((S, H * 256), BF16), _sds((H, S, 128), F32)],
        scratch_shapes=[pltpu.VMEM((tk, 256), F32), pltpu.VMEM((tk, HEAD), F32)],
        compiler_params=_cparams(("parallel", "arbitrary", "arbitrary")),
    )(q, kv, kv, kp, do, lse, delta)


NA_RB = 8


def _na_bias_index(rows):
    j = np.arange(NA_KH)
    dy = j[None, :] - (np.arange(8)[:, None] - 4) + 3
    c = np.arange(GRID_W)
    col_start = np.clip(c - NA_KW // 2, 0, GRID_W - NA_KW)
    ok = (c[None, :] >= col_start[:, None]) & (c[None, :] < col_start[:, None] + NA_KW)
    dx = np.clip(c[None, :] - c[:, None], -(NA_KW - 1), NA_KW - 1) + (NA_KW - 1)
    dy_full = np.broadcast_to(dy[:, None, :, None], (8, GRID_W, NA_KH, GRID_W)).reshape(8, GRID_W, NA_KH * GRID_W)
    dx_full = np.broadcast_to(dx[None, :, None, :], (8, GRID_W, NA_KH, GRID_W)).reshape(8, GRID_W, NA_KH * GRID_W)
    ok_full = np.broadcast_to(ok[None, :, None, :], (8, GRID_W, NA_KH, GRID_W)).reshape(8, GRID_W, NA_KH * GRID_W)
    valid = ok_full & (dy_full >= 0) & (dy_full <= 2 * NA_KH - 2)
    return np.clip(dy_full, 0, 2 * NA_KH - 2), dx_full, valid


def _na_bias(rpb, dx_masked):
    L, H, NY, NX = rpb.shape
    nkeys = NA_KH * GRID_W

    def body(rpb_ref, dx_ref, out_ref):
        base = (pl.program_id(0) * H + pl.program_id(1)) * (NY * NX)
        dxv = dx_ref[...]
        key_row = lax.shift_right_logical(lax.broadcasted_iota(jnp.int32, (1, nkeys), 1), 6)

        def variant(o, carry):
            acc = jnp.full((GRID_W, nkeys), NEG, F32)
            for xx in range(NX):
                row = jnp.zeros((1, nkeys), F32)
                for j in range(NA_KH):
                    row = jnp.where(key_row == j, rpb_ref[base + (j - o + NA_KH - 1) * NX + xx], row)
                acc = jnp.where(dxv == xx, row, acc)
            out_ref[o] = acc
            return carry

        lax.fori_loop(0, 8, variant, 0)

    return pl.pallas_call(
        body, name="na_bias", grid=(L, H),
        in_specs=[pl.BlockSpec(memory_space=pltpu.SMEM), pl.BlockSpec((GRID_W, nkeys), lambda l, h: (0, 0))],
        out_specs=pl.BlockSpec((None, None, 8, GRID_W, nkeys), lambda l, h: (l, h, 0, 0, 0)),
        out_shape=_sds((L, H, 8, GRID_W, nkeys), F32),
        compiler_params=_cparams(("parallel", "parallel")),
    )(rpb.reshape(-1), dx_masked)


def _na_row_window(rb, i, rows):
    r = rb * NA_RB + i
    ks = jnp.clip(r - NA_KH // 2, 0, rows - NA_KH)
    variant = r - ks
    return pl.ds(pl.multiple_of(ks * GRID_W, GRID_W), NA_KH * GRID_W), variant


def _na_fwd(l, proj, brow, off_q, off_k, off_v, scale):
    S = proj.shape[0]
    H = brow.shape[1]
    rows = S // GRID_W
    assert rows % NA_RB == 0 and rows >= NA_KH
    tq = NA_RB * GRID_W

    def body(q_ref, k_ref, v_ref, b_ref, o_ref):
        rb = pl.program_id(1)
        for i in range(NA_RB):
            win, variant = _na_row_window(rb, i, rows)
            qs = slice(i * GRID_W, (i + 1) * GRID_W)
            s = lax.dot_general(q_ref[qs, :], k_ref[win, :], _NT, preferred_element_type=F32) * scale + b_ref[variant]
            e = jnp.exp(s - jnp.max(s, axis=-1, keepdims=True))
            p = e / jnp.sum(e, axis=-1, keepdims=True)
            o_ref[qs, :] = jnp.dot(p.astype(BF16), v_ref[win, :], preferred_element_type=F32).astype(BF16)

    return pl.pallas_call(
        body, name=f"na_fwd_{l}", grid=(H, rows // NA_RB),
        in_specs=[pl.BlockSpec((tq, HEAD), lambda h, r: (r, off_q // HEAD + h)),
                  pl.BlockSpec((S, HEAD), lambda h, r: (0, off_k // HEAD + h)),
                  pl.BlockSpec((S, HEAD), lambda h, r: (0, off_v // HEAD + h)),
                  pl.BlockSpec((None, None, 8, GRID_W, NA_KH * GRID_W), lambda h, r: (l, h, 0, 0, 0))],
        out_specs=pl.BlockSpec((tq, HEAD), lambda h, r: (r, h)),
        out_shape=_sds((S, H * HEAD), BF16),
        compiler_params=_cparams(("parallel", "arbitrary")),
    )(proj, proj, proj, brow)


def _na_bwd(l, proj, brow, o, do, off_q, off_k, off_v, scale):
    S = proj.shape[0]
    H = brow.shape[1]
    rows = S // GRID_W
    tq = NA_RB * GRID_W

    def body(q_ref, k_ref, v_ref, b_ref, o_ref, do_ref, dq_ref, dk_ref, dv_ref, db_ref):
        rb = pl.program_id(1)

        @pl.when(rb == 0)
        def _():
            dk_ref[...] = jnp.zeros_like(dk_ref)
            dv_ref[...] = jnp.zeros_like(dv_ref)
            db_ref[...] = jnp.zeros_like(db_ref)

        for i in range(NA_RB):
            win, variant = _na_row_window(rb, i, rows)
            qs = slice(i * GRID_W, (i + 1) * GRID_W)
            qv, kw, vw, dov = q_ref[qs, :], k_ref[win, :], v_ref[win, :], do_ref[qs, :]
            s = lax.dot_general(qv, kw, _NT, preferred_element_type=F32) * scale + b_ref[variant]
            e = jnp.exp(s - jnp.max(s, axis=-1, keepdims=True))
            p = e / jnp.sum(e, axis=-1, keepdims=True)
            dv_ref[win, :] += lax.dot_general(p.astype(BF16), dov, _TN, preferred_element_type=F32)
            dp = lax.dot_general(dov, vw, _NT, preferred_element_type=F32)
            dl = jnp.sum(dov.astype(F32) * o_ref[qs, :].astype(F32), axis=-1, keepdims=True)
            ds = p * (dp - dl)
            db_ref[variant] += ds
            dsb = (ds * scale).astype(BF16)
            dq_ref[qs, :] = jnp.dot(dsb, kw, preferred_element_type=F32).astype(BF16)
            dk_ref[win, :] += lax.dot_general(dsb, qv, _TN, preferred_element_type=F32)

    qblk = pl.BlockSpec((tq, HEAD), lambda h, r: (r, h))
    full = pl.BlockSpec((S, HEAD), lambda h, r: (0, h))
    bias = pl.BlockSpec((None, None, 8, GRID_W, NA_KH * GRID_W), lambda h, r: (l, h, 0, 0, 0))
    dbias = pl.BlockSpec((None, 8, GRID_W, NA_KH * GRID_W), lambda h, r: (h, 0, 0, 0))
    return pl.pallas_call(
        body, name=f"na_bwd_{l}", grid=(H, rows // NA_RB),
        in_specs=[pl.BlockSpec((tq, HEAD), lambda h, r: (r, off_q // HEAD + h)),
                  pl.BlockSpec((S, HEAD), lambda h, r: (0, off_k // HEAD + h)),
                  pl.BlockSpec((S, HEAD), lambda h, r: (0, off_v // HEAD + h)),
                  bias, qblk, qblk],
        out_specs=[qblk, full, full, dbias],
        out_shape=[_sds((S, H * HEAD), BF16), _sds((S, H * HEAD), F32), _sds((S, H * HEAD), F32),
                   _sds(brow.shape[1:], F32)],
        compiler_params=_cparams(("parallel", "arbitrary")),
    )(proj, proj, proj, brow, o, do)


def _place():
    return lax.axis_index("x"), lax.axis_index("y"), lax.axis_index("c")


def _full_shape(shard_shape, kind):
    L, A, B = shard_shape
    return {"col": (L, A, N_CHIPS * B), "row": (L, N_CHIPS * A, B), "slot": (N_CHIPS, L, A, B)}[kind]


def _shard_region(ref, kind, shard_shape, k, half=None):
    L, A, B = shard_shape
    lo, n = (0, A) if half is None else (pl.multiple_of(half * (A // 2), 16), A // 2)
    if kind == "col":
        return ref.at[:, pl.ds(lo, n), pl.ds(pl.multiple_of(k * B, 128), B)]
    if kind == "row":
        return ref.at[:, pl.ds(pl.multiple_of(k * A + lo, 16), n), :]
    return ref.at[k, :, pl.ds(lo, n), :]


def _weight_gather(shards, kinds):
    n = len(shards)
    shapes = [s.shape for s in shards]
    assert all(s[1] % 32 == 0 for s in shapes)

    def body(*refs):
        w, o = refs[:n], refs[n:2 * n]
        send_sems, recv_sems, local_sems = refs[2 * n:]
        x, y, c = _place()
        sibling = (x, y, 1 - c)
        chips = [(1 - x, y), (x, 1 - y), (1 - x, 1 - y)]

        def copy(k, src, dst, to):
            return pltpu.make_async_remote_copy(src_ref=src, dst_ref=dst, send_sem=send_sems.at[k],
                                                recv_sem=recv_sems.at[k], device_id=to, device_id_type=MESH)

        def region(i, cx, cy, half=None):
            return _shard_region(o[i], kinds[i], shapes[i], 2 * cx + cy, half)

        def my_half(i):
            A = shapes[i][1]
            return w[i].at[:, pl.ds(pl.multiple_of(c * (A // 2), 16), A // 2), :]

        local = [pltpu.make_async_copy(w[i], region(i, x, y), local_sems.at[i]) for i in range(n)]
        for cp in local:
            cp.start()
        first = [copy(6 * i + j, my_half(i), region(i, x, y, c), (*chip, c))
                 for i in range(n) for j, chip in enumerate(chips)]
        for cp in first:
            cp.start()
        passed = []
        for i in range(n):
            for j, chip in enumerate(chips):
                landed = region(i, *chip, c)
                copy(6 * i + j, landed, landed, (*chip, c)).wait_recv()
                passed.append(copy(6 * i + 3 + j, landed, landed, sibling))
                passed[-1].start()
        for i in range(n):
            for j, chip in enumerate(chips):
                other = region(i, *chip, 1 - c)
                copy(6 * i + 3 + j, other, other, sibling).wait_recv()
        for cp in first + passed:
            cp.wait_send()
        for cp in local:
            cp.wait()

    hbm = pl.BlockSpec(memory_space=pl.ANY)
    return pl.pallas_call(
        body, name="weight_gather",
        in_specs=[hbm] * n, out_specs=[hbm] * n,
        out_shape=[_sds(_full_shape(s.shape, k), s.dtype) for s, k in zip(shards, kinds)],
        scratch_shapes=[pltpu.SemaphoreType.DMA((6 * n,)), pltpu.SemaphoreType.DMA((6 * n,)),
                        pltpu.SemaphoreType.DMA((n,))],
    )(*shards)


def _grad_scatter(grads, kinds, shapes):
    n = len(grads)

    def body(*refs):
        g, outs = refs[:n], refs[n:3 * n]
        send_sems, recv_sems, local_sems = refs[3 * n:]
        own, got = outs[0::2], outs[1::2]
        x, y, c = _place()
        chips = [(1 - x, y), (x, 1 - y), (1 - x, 1 - y)]
        local = [pltpu.make_async_copy(_shard_region(g[i], kinds[i], shapes[i], 2 * x + y), own[i], local_sems.at[i])
                 for i in range(n)]
        for cp in local:
            cp.start()
        sends = [pltpu.make_async_remote_copy(
            src_ref=_shard_region(g[i], kinds[i], shapes[i], 2 * cx + cy), dst_ref=got[i].at[j],
            send_sem=send_sems.at[3 * i + j], recv_sem=recv_sems.at[3 * i + j],
            device_id=(cx, cy, c), device_id_type=MESH) for i in range(n) for j, (cx, cy) in enumerate(chips)]
        for cp in sends:
            cp.start()
        for cp in sends:
            cp.wait_recv()
        for cp in sends:
            cp.wait_send()
        for cp in local:
            cp.wait()

    hbm = pl.BlockSpec(memory_space=pl.ANY)
    out_shape = []
    for gr, s in zip(grads, shapes):
        out_shape += [_sds(tuple(s), gr.dtype), _sds((3,) + tuple(s), gr.dtype)]
    outs = pl.pallas_call(
        body, name="grad_scatter",
        in_specs=[hbm] * n, out_specs=[hbm] * (2 * n), out_shape=out_shape,
        scratch_shapes=[pltpu.SemaphoreType.DMA((3 * n,)), pltpu.SemaphoreType.DMA((3 * n,)),
                        pltpu.SemaphoreType.DMA((n,))],
    )(*grads)
    return outs[0::2], outs[1::2]


def _grad_partial(name, own, got):
    R, W = own.shape
    tm = _tile(R, max(16, (1 << 19) // W // 16 * 16), 16)

    def body(own_ref, got_ref, out_ref):
        acc = own_ref[...].astype(F32)
        for j in range(3):
            acc = acc + got_ref[j].astype(F32)
        out_ref[...] = acc

    return pl.pallas_call(
        body, name=name, grid=(R // tm,),
        in_specs=[pl.BlockSpec((tm, W), lambda i: (i, 0)), pl.BlockSpec((3, tm, W), lambda i: (0, i, 0))],
        out_specs=pl.BlockSpec((tm, W), lambda i: (i, 0)),
        out_shape=_sds((R, W), F32),
        compiler_params=_cparams(("parallel",)),
    )(own, got)


def _sibling_swap(parts):
    n = len(parts)

    def body(*refs):
        p, got = refs[:n], refs[n:2 * n]
        send_sems, recv_sems = refs[2 * n:]
        x, y, c = _place()
        copies = [pltpu.make_async_remote_copy(src_ref=p[i], dst_ref=got[i], send_sem=send_sems.at[i],
                                               recv_sem=recv_sems.at[i], device_id=(x, y, 1 - c), device_id_type=MESH)
                  for i in range(n)]
        for cp in copies:
            cp.start()
        for cp in copies:
            cp.wait()

    hbm = pl.BlockSpec(memory_space=pl.ANY)
    return pl.pallas_call(
        body, name="sibling_swap",
        in_specs=[hbm] * n, out_specs=[hbm] * n,
        out_shape=[_sds(p.shape, p.dtype) for p in parts],
        scratch_shapes=[pltpu.SemaphoreType.DMA((n,)), pltpu.SemaphoreType.DMA((n,))],
    )(*parts)


def _small_allreduce(vec):
    NR, W = vec.shape

    def body(v_ref, all_ref, sum_ref, send_sems, recv_sems):
        x, y, c = _place()
        me = 4 * x + 2 * y + c
        all_ref[me] = v_ref[...]
        copies = []
        for k in range(1, N_DEV):
            fx, fy, fc = (k >> 2) & 1, (k >> 1) & 1, k & 1
            peer = (x ^ fx, y ^ fy, c ^ fc)
            copies.append(pltpu.make_async_remote_copy(
                src_ref=v_ref, dst_ref=all_ref.at[me], send_sem=send_sems.at[k - 1], recv_sem=recv_sems.at[k - 1],
                device_id=peer, device_id_type=MESH))
        for cp in copies:
            cp.start()
        for cp in copies:
            cp.wait_recv()
        for cp in copies:
            cp.wait_send()
        acc = all_ref[0]
        for d in range(1, N_DEV):
            acc = acc + all_ref[d]
        sum_ref[...] = acc

    return pl.pallas_call(
        body, name="small_allreduce",
        in_specs=[pl.BlockSpec(memory_space=pltpu.VMEM)],
        out_specs=[pl.BlockSpec(memory_space=pltpu.VMEM), pl.BlockSpec(memory_space=pltpu.VMEM)],
        out_shape=[_sds((N_DEV, NR, W), F32), _sds((NR, W), F32)],
        scratch_shapes=[pltpu.SemaphoreType.DMA((N_DEV - 1,)), pltpu.SemaphoreType.DMA((N_DEV - 1,))],
    )(vec)[1]


def _adamw_math(g, w, m, v):
    m = ADAM_B1 * m + (1.0 - ADAM_B1) * g
    v = ADAM_B2 * v + (1.0 - ADAM_B2) * (g * g)
    m_hat = m / (1.0 - ADAM_B1 ** ADAM_STEP)
    v_hat = v / (1.0 - ADAM_B2 ** ADAM_STEP)
    delta = -ADAM_LR * (m_hat / (jnp.sqrt(v_hat) + ADAM_EPS) + ADAM_WD * w)
    return delta, m, v


def _adamw(name, ga, gb, w, m, v):
    rows, n = w.shape
    tm = _tile(rows, max(8, (1 << 18) // n // 8 * 8), 8)

    def body(ga_ref, gb_ref, w_ref, m_ref, v_ref, g_out, d_out, m_out, v_out):
        g = ga_ref[...] + gb_ref[...]
        delta, mn, vn = _adamw_math(g, w_ref[...], m_ref[...], v_ref[...])
        g_out[...] = g
        d_out[...] = delta
        m_out[...] = mn
        v_out[...] = vn

    blk = pl.BlockSpec((tm, n), lambda i: (i, 0))
    return pl.pallas_call(
        body, name=name, grid=(rows // tm,),
        in_specs=[blk] * 5, out_specs=[blk] * 4, out_shape=[_sds((rows, n), F32)] * 4,
        compiler_params=_cparams(("parallel",)),
    )(ga, gb, w, m, v)


def _pack(parts):
    flat = jnp.concatenate([p.reshape(-1) for p in parts])
    pad = (-flat.shape[0]) % 1024
    if pad:
        flat = jnp.concatenate([flat, jnp.zeros((pad,), flat.dtype)])
    return flat.reshape(-1, 128)


def _unpack(flat, shapes):
    flat = flat.reshape(-1)
    out, off = [], 0
    for s in shapes:
        n = int(np.prod(s))
        out.append(flat[off:off + n].reshape(s))
        off += n
    return out


def kernel(x, norm_mix, w_in, norm_qa, w_uq, norm_kva, w_ukv, rpb, w_o_mla, w_o_na, w_out, norm_mlp, w_ff1, w_ff2, norm_final, loss_target, m_norm_mix, m_w_in, m_norm_qa, m_w_uq, m_norm_kva, m_w_ukv, m_rpb, m_w_o_mla, m_w_o_na, m_w_out, m_norm_mlp, m_w_ff1, m_w_ff2, m_norm_final, v_norm_mix, v_w_in, v_norm_qa, v_w_uq, v_norm_kva, v_w_ukv, v_rpb, v_w_o_mla, v_w_o_na, v_w_out, v_norm_mlp, v_w_ff1, v_w_ff2, v_norm_final):
    wts = dict(norm_mix=norm_mix, w_in=w_in, norm_qa=norm_qa, w_uq=w_uq, norm_kva=norm_kva, w_ukv=w_ukv, rpb=rpb,
               w_o_mla=w_o_mla, w_o_na=w_o_na, w_out=w_out, norm_mlp=norm_mlp, w_ff1=w_ff1, w_ff2=w_ff2,
               norm_final=norm_final)
    mom = dict(norm_mix=m_norm_mix, w_in=m_w_in, norm_qa=m_norm_qa, w_uq=m_w_uq, norm_kva=m_norm_kva, w_ukv=m_w_ukv,
               rpb=m_rpb, w_o_mla=m_w_o_mla, w_o_na=m_w_o_na, w_out=m_w_out, norm_mlp=m_norm_mlp, w_ff1=m_w_ff1,
               w_ff2=m_w_ff2, norm_final=m_norm_final)
    var = dict(norm_mix=v_norm_mix, w_in=v_w_in, norm_qa=v_norm_qa, w_uq=v_w_uq, norm_kva=v_norm_kva, w_ukv=v_w_ukv,
               rpb=v_rpb, w_o_mla=v_w_o_mla, w_o_na=v_w_o_na, w_out=v_w_out, norm_mlp=v_norm_mlp, w_ff1=v_w_ff1,
               w_ff2=v_w_ff2, norm_final=v_norm_final)

    _, S, D = x.shape
    L = w_in.shape[0]
    QL, KL = norm_qa.shape[1], norm_kva.shape[1]
    H = w_uq.shape[2] * N_CHIPS // (HEAD + ROPE)
    NAW = w_o_na.shape[1]
    NH = NAW // HEAD
    rows = S // GRID_W
    x = x.reshape(S, D)
    target = loss_target.reshape(S, D)

    kinds = ["slot" if n == "w_in" else ("row" if n in ROW_SHARDED else "col") for n in BIG]
    shard_shapes = [wts[n].shape for n in BIG]
    full = dict(zip(BIG, _weight_gather([wts[n].astype(BF16) for n in BIG], kinds)))

    widths = (QL, KL, ROPE, NAW, NAW, NAW, D, D)
    starts = np.concatenate([[0], np.cumsum(widths)]).astype(int)
    order = (6, 7, 3, 4, 5, 0, 1, 2)
    nloc = w_in.shape[2]
    pieces = []
    for i in order:
        for k in range(N_CHIPS):
            lo, hi = max(int(starts[i]), k * nloc), min(int(starts[i + 1]), (k + 1) * nloc)
            if lo < hi:
                pieces.append(full["w_in"][k, :, :, lo - k * nloc:hi - k * nloc])
    w_in_p = jnp.concatenate(pieces + [jnp.zeros((L, D, 128 - ROPE), BF16)], axis=2)
    new_off = np.concatenate([[0], np.cumsum([widths[i] for i in order])]).astype(int)
    off_ga, off_gb, off_q, off_k, off_v, off_cq, off_ckv, off_kpe = (int(o) for o in new_off[:8])
    PW = w_in_p.shape[2]
    w_uq_p = jnp.pad(full["w_uq"].reshape(L, QL, H, HEAD + ROPE), ((0, 0), (0, 0), (0, 0), (0, 256 - HEAD - ROPE)))
    w_uq_p = w_uq_p.reshape(L, QL, H * 256)

    pos = jnp.arange(S, dtype=F32)
    inv_freq = 1.0 / (ROPE_THETA ** (jnp.arange(0, ROPE, 2, dtype=F32) / ROPE))
    ang = pos[:, None] * inv_freq[None, :]
    cos, sin, zero = jnp.cos(ang), jnp.sin(ang), jnp.zeros((S, 128 - ROPE), F32)
    cos_t = jnp.concatenate([cos, cos, zero], axis=1)
    sin_t = jnp.concatenate([-sin, sin, zero], axis=1)

    dy_idx, dx_idx, bias_ok = _na_bias_index(rows)
    brow = _na_bias(wts["rpb"], jnp.asarray(np.where(bias_ok[0], dx_idx[0], -1), jnp.int32))
    mla_scale = float((HEAD + ROPE) ** -0.5)
    na_scale = float(HEAD ** -0.5)

    def rope_q_epilogue(acc, extra, outs):
        cv, sv = extra[0][...], extra[1][...]
        for hh in range(acc.shape[1] // 256):
            outs[0][:, hh * 256:hh * 256 + 128] = acc[:, hh * 256:hh * 256 + 128].astype(BF16)
            outs[0][:, hh * 256 + 128:(hh + 1) * 256] = _rope128(acc[:, hh * 256 + 128:(hh + 1) * 256], cv, sv).astype(BF16)

    def store_f32(acc, extra, outs):
        outs[0][...] = acc

    def merge_epilogue(acc, extra, outs):
        ga, gb, ya = extra[0][...].astype(F32), extra[1][...].astype(F32), extra[2][...]
        outs[0][...] = (jax.nn.sigmoid(ga) * ya + jax.nn.sigmoid(gb) * acc).astype(BF16)
        outs[1][...] = acc

    def residual_epilogue(acc, extra, outs):
        outs[0][...] = extra[0][...] + acc

    def ff1_epilogue(acc, extra, outs):
        outs[0][...] = acc.astype(BF16)
        outs[1][...] = jnp.square(jnp.maximum(acc, 0.0)).astype(BF16)

    def dff_epilogue(acc, extra, outs):
        outs[0][...] = (acc * (2.0 * jnp.maximum(extra[0][...].astype(F32), 0.0))).astype(BF16)

    def dmerge_epilogue(acc, extra, outs):
        ga, gb = extra[0][...].astype(F32), extra[1][...].astype(F32)
        ya, yb = extra[2][...], extra[3][...]
        sa, sb = jax.nn.sigmoid(ga), jax.nn.sigmoid(gb)
        outs[0][...] = (acc * sa).astype(BF16)
        outs[1][...] = (acc * sb).astype(BF16)
        outs[2][...] = (acc * ya * sa * (1.0 - sa)).astype(BF16)
        outs[3][...] = (acc * yb * sb * (1.0 - sb)).astype(BF16)

    saved = []
    for l in range(L):
        u, r1 = _rms_fwd(f"rms_mix_{l}", x, wts["norm_mix"][l][None])
        proj, = _mm(f"proj_{l}", u, w_in_p, "nn", [_sds((S, PW), BF16)], lb=l, tk_cap=2048)
        nq, nkv, rq, rkv, kp = _lat_fwd(l, proj, wts["norm_qa"][l][None], wts["norm_kva"][l][None], cos_t, sin_t,
                                        off_cq, off_ckv, off_kpe)
        q, = _mm(f"q_up_{l}", nq, w_uq_p, "nn", [_sds((S, H * 256), BF16)], lb=l,
                 extras=(cos_t, sin_t), extra_specs=(_row_spec(128), _row_spec(128)), epilogue=rope_q_epilogue, tn_cap=512)
        kv, = _mm(f"kv_up_{l}", nkv, full["w_ukv"], "nn", [_sds((S, H * 256), BF16)], lb=l)
        o_a, lse = _mla_fwd(l, q, kv, kp, mla_scale)
        o_b = _na_fwd(l, proj, brow, off_q, off_k, off_v, na_scale)
        y_a, = _mm(f"o_mla_{l}", o_a, full["w_o_mla"], "nn", [_sds((S, D), F32)], lb=l, epilogue=store_f32,
                   tm_cap=512, tn_cap=512)
        merged, y_b = _mm(f"o_na_merge_{l}", o_b, full["w_o_na"], "nn", [_sds((S, D), BF16), _sds((S, D), F32)], lb=l,
                          extras=(proj, proj, y_a), extra_specs=(_col_spec(off_ga), _col_spec(off_gb), _tile_spec),
                          epilogue=merge_epilogue, tm_cap=512, tn_cap=512)
        x2, = _mm(f"w_out_{l}", merged, full["w_out"], "nn", [_sds((S, D), F32)], lb=l,
                  extras=(x,), extra_specs=(_tile_spec,), epilogue=residual_epilogue, tm_cap=512)
        u2, r2 = _rms_fwd(f"rms_mlp_{l}", x2, wts["norm_mlp"][l][None])
        h, a = _mm(f"ff1_{l}", u2, full["w_ff1"], "nn", [_sds((S, 4 * D), BF16), _sds((S, 4 * D), BF16)], lb=l,
                   epilogue=ff1_epilogue, tk_cap=2048)
        x3, = _mm(f"ff2_{l}", a, full["w_ff2"], "nn", [_sds((S, D), F32)], lb=l,
                  extras=(x2,), extra_specs=(_tile_spec,), epilogue=residual_epilogue, tm_cap=512)
        saved.append(dict(x=x, r1=r1, u=u, proj=proj, nq=nq, nkv=nkv, rq=rq, rkv=rkv, q=q, kv=kv, kp=kp, o_a=o_a,
                          lse=lse, o_b=o_b, y_a=y_a, y_b=y_b, merged=merged, x2=x2, r2=r2, u2=u2, h=h, a=a))
        x = x3

    loss_lanes, dx, dxb, dg_final = _final_loss(x, target, wts["norm_final"][None])
    loss = lax.psum(loss_lanes[0, 0], ("x", "y", "c"))

    gbig = {n: [None] * L for n in ("w_in", "w_uq")}
    gstack = {n: lax.empty(full[n].shape, BF16) for n in BIG if n not in gbig}
    gsmall = {n: [None] * L for n in SMALL if n != "norm_final"}
    oh_dy = jnp.asarray(dy_idx[:, 0, :, None] == np.arange(2 * NA_KH - 1), F32)
    oh_dx = jnp.asarray((dx_idx[0, :, :, None] == np.arange(2 * NA_KW - 1)) & bias_ok[0, :, :, None], F32)
    for l in reversed(range(L)):
        sv = saved[l]
        proj = sv["proj"]
        dh, = _mm(f"d_ff2_{l}", dxb, full["w_ff2"], "nt", [_sds((S, 4 * D), BF16)], lb=l,
                  extras=(sv["h"],), extra_specs=(_tile_spec,), epilogue=dff_epilogue, tk_cap=2048)
        gstack["w_ff2"], = _mm(f"g_ff2_{l}", sv["a"], dxb, "tn", [None], tn_cap=2048, into=(gstack["w_ff2"], l))
        du2, = _mm(f"d_ff1_{l}", dh, full["w_ff1"], "nt", [_sds((S, D), F32)], lb=l, epilogue=store_f32)
        gstack["w_ff1"], = _mm(f"g_ff1_{l}", sv["u2"], dh, "tn", [None], tn_cap=2048, into=(gstack["w_ff1"], l))
        dx2, dx2b, gsmall["norm_mlp"][l] = _rms_bwd(f"rms_mlp_bwd_{l}", sv["x2"], sv["r2"], wts["norm_mlp"][l][None], du2, dx)
        dya, dyb, dga, dgb = _mm(
            f"d_w_out_{l}", dx2b, full["w_out"], "nt", [_sds((S, D), BF16)] * 4, lb=l,
            extras=(proj, proj, sv["y_a"], sv["y_b"]),
            extra_specs=(_col_spec(off_ga), _col_spec(off_gb), _tile_spec, _tile_spec),
            epilogue=dmerge_epilogue, tm_cap=512, tn_cap=512, tk_cap=2048)
        gstack["w_out"], = _mm(f"g_w_out_{l}", sv["merged"], dx2b, "tn", [None], tn_cap=2048, into=(gstack["w_out"], l))
        do_a, = _mm(f"d_o_mla_{l}", dya, full["w_o_mla"], "nt", [_sds((S, H * HEAD), BF16)], lb=l, tk_cap=2048)
        gstack["w_o_mla"], = _mm(f"g_o_mla_{l}", sv["o_a"], dya, "tn", [None], tn_cap=2048, into=(gstack["w_o_mla"], l))
        do_b, = _mm(f"d_o_na_{l}", dyb, full["w_o_na"], "nt", [_sds((S, NAW), BF16)], lb=l, tk_cap=2048)
        gstack["w_o_na"], = _mm(f"g_o_na_{l}", sv["o_b"], dyb, "tn", [None], tn_cap=2048, into=(gstack["w_o_na"], l))
        dq_na, dk_na, dv_na, dbrow = _na_bwd(l, proj, brow, sv["o_b"], do_b, off_q, off_k, off_v, na_scale)
        tmp = jnp.einsum("hoqn,qnx->honx", dbrow, oh_dx, precision=lax.Precision.HIGHEST)
        gsmall["rpb"][l] = jnp.einsum("honx,ony->hyx", tmp, oh_dy, precision=lax.Precision.HIGHEST)
        dl = _delta(l, sv["o_a"], do_a)
        dq_f, dkv, dkp_h = _mla_bwd(l, sv["q"], sv["kv"], sv["kp"], do_a, sv["lse"], dl, mla_scale)
        dq = _rope_bwd_q(l, dq_f, cos_t, sin_t)
        dnq, = _mm(f"d_q_up_{l}", dq, w_uq_p, "nt", [_sds((S, QL), F32)], lb=l, epilogue=store_f32, tk_cap=2048)
        g_uq, = _mm(f"g_q_up_{l}", sv["nq"], dq, "tn", [_sds((QL, H * 256), BF16)], tn_cap=2048)
        gbig["w_uq"][l] = g_uq.reshape(QL, H, 256)[:, :, :HEAD + ROPE].reshape(QL, H * (HEAD + ROPE))
        dnkv, = _mm(f"d_kv_up_{l}", dkv, full["w_ukv"], "nt", [_sds((S, KL), F32)], lb=l, epilogue=store_f32, tk_cap=2048)
        gstack["w_ukv"], = _mm(f"g_kv_up_{l}", sv["nkv"], dkv, "tn", [None], tn_cap=2048, into=(gstack["w_ukv"], l))
        dcq, dckv, dkpe, gsmall["norm_qa"][l], gsmall["norm_kva"][l] = _lat_bwd(
            l, proj, sv["rq"], sv["rkv"], wts["norm_qa"][l][None], wts["norm_kva"][l][None], dnq, dnkv, dkp_h,
            cos_t, sin_t, off_cq, off_ckv)
        dproj = jnp.concatenate([dga, dgb, dq_na, dk_na.astype(BF16), dv_na.astype(BF16), dcq, dckv, dkpe], axis=1)
        du, = _mm(f"d_proj_{l}", dproj, w_in_p, "nt", [_sds((S, D), F32)], lb=l, epilogue=store_f32)
        g_in, = _mm(f"g_proj_{l}", sv["u"], dproj, "tn", [_sds((D, PW), BF16)], tn_cap=2048)
        back = [None] * 8
        for pos_new, i in enumerate(order):
            back[i] = g_in[:, new_off[pos_new]:new_off[pos_new] + widths[i]]
        g_orig = jnp.concatenate(back, axis=1)
        gbig["w_in"][l] = [g_orig[:, k * nloc:(k + 1) * nloc] for k in range(N_CHIPS)]
        dx, dxb, gsmall["norm_mix"][l] = _rms_bwd(f"rms_mix_bwd_{l}", sv["x"], sv["r1"], wts["norm_mix"][l][None], du, dx2)
    grad_x = dx.reshape(1, S, D)

    gstack["w_in"] = jnp.stack([jnp.stack([gbig["w_in"][l][k] for l in range(L)]) for k in range(N_CHIPS)])
    gstack["w_uq"] = jnp.stack(gbig["w_uq"])
    own, got = _grad_scatter([gstack[n] for n in BIG], kinds, shard_shapes)
    two = lambda t: t.reshape(t.shape[:-3] + (t.shape[-3] * t.shape[-2], t.shape[-1]))
    parts = [_grad_partial(f"grad_partial_{n}", two(o), two(g)) for n, o, g in zip(BIG, own, got)]
    others = _sibling_swap(parts)
    part_w, other_w = dict(zip(BIG, parts)), dict(zip(BIG, others))

    small_shapes = [wts[n].shape for n in SMALL]
    small_g = [jnp.stack([g.reshape(wts[n].shape[1:]) for g in gsmall[n]]) for n in SMALL if n != "norm_final"]
    small_g.append(dg_final.reshape(D))
    gsum = _small_allreduce(_pack(small_g))
    zeros = jnp.zeros_like(gsum)
    sg, sd, sm, svv = _adamw("adamw_small", gsum, zeros, _pack([wts[n] for n in SMALL]), _pack([mom[n] for n in SMALL]),
                             _pack([var[n] for n in SMALL]))
    res = {n: {} for n in WEIGHTS}
    for key, flat in (("g", sg), ("d", sd), ("m", sm), ("v", svv)):
        for n, arr in zip(SMALL, _unpack(flat, small_shapes)):
            res[n][key] = arr
    for n in BIG:
        shp = wts[n].shape
        outs = _adamw(f"adamw_{n}", part_w[n], other_w[n], two(wts[n]), two(mom[n]), two(var[n]))
        for key, arr in zip(("g", "d", "m", "v"), outs):
            res[n][key] = arr.reshape(shp)

    return (loss, grad_x, *[res[n]["g"] for n in WEIGHTS], *[res[n]["d"] for n in WEIGHTS],
            *[res[n]["m"] for n in WEIGHTS], *[res[n]["v"] for n in WEIGHTS])
```

```python
import functools

import numpy as np
import jax
import jax.numpy as jnp
from jax import lax
from jax.experimental import pallas as pl
from jax.experimental.pallas import tpu as pltpu

F32 = jnp.float32
BF16 = jnp.bfloat16
MESH = pl.DeviceIdType.MESH

EPS = 1e-6
ROPE_THETA = 10000.0
ROPE = 64
HEAD = 128
GRID_W = 64
NA_KH = 8
NA_KW = 16
N_CHIPS = 4
N_DEV = 8
NEG = -1e30
LOG2E = 1.4426950408889634

ADAM_LR = 0.001
ADAM_B1 = 0.9
ADAM_B2 = 0.999
ADAM_EPS = 1e-08
ADAM_WD = 0.01
ADAM_STEP = 10

VMEM_LIMIT = 56 * 1024 * 1024

BIG = ("w_in", "w_uq", "w_ukv", "w_o_mla", "w_o_na", "w_out", "w_ff1", "w_ff2")
ROW_SHARDED = ("w_out", "w_ff2")
SMALL = ("norm_mix", "norm_qa", "norm_kva", "rpb", "norm_mlp", "norm_final")
WEIGHTS = ("norm_mix", "w_in", "norm_qa", "w_uq", "norm_kva", "w_ukv", "rpb", "w_o_mla", "w_o_na",
           "w_out", "norm_mlp", "w_ff1", "w_ff2", "norm_final")


def _cparams(sem, **kw):
    return pltpu.CompilerParams(dimension_semantics=sem, vmem_limit_bytes=VMEM_LIMIT, **kw)


def _tile(n, cap, unit=128):
    if n <= cap:
        return n
    best = None
    for t in range(unit, cap + 1, unit):
        if n % t == 0:
            best = t
    assert best is not None, (n, cap, unit)
    return best


def _sds(shape, dtype):
    return jax.ShapeDtypeStruct(shape, dtype)


_DIMS = {"nn": (((1,), (0,)), ((), ())), "nt": (((1,), (1,)), ((), ())), "tn": (((0,), (0,)), ((), ()))}


def _store_cast(acc, extra, outs):
    outs[0][...] = acc.astype(outs[0].dtype)


def _mm(name, a, b, mode, out_shapes, *, lb=None, extras=(), extra_specs=(), out_specs=None,
        epilogue=_store_cast, tm_cap=1024, tn_cap=1024, tk_cap=1024, into=None):
    bshape = b.shape[1:] if lb is not None else b.shape
    if mode == "nn":
        (M, K), (K2, N) = a.shape, bshape
    elif mode == "nt":
        (M, K), (N, K2) = a.shape, bshape
    else:
        (K, M), (K2, N) = a.shape, bshape
    assert K == K2, (name, a.shape, b.shape)
    tm, tn, tk = _tile(M, tm_cap), _tile(N, tn_cap), _tile(K, tk_cap)
    nk = K // tk
    if mode == "tn":
        a_spec = pl.BlockSpec((tk, tm), lambda i, j, k: (k, i))
    else:
        a_spec = pl.BlockSpec((tm, tk), lambda i, j, k: (i, k))
    bblk, bidx = ((tn, tk), lambda i, j, k: (j, k)) if mode == "nt" else ((tk, tn), lambda i, j, k: (k, j))
    if lb is not None:
        b_spec = pl.BlockSpec((None,) + bblk, lambda i, j, k: (lb,) + bidx(i, j, k))
    else:
        b_spec = pl.BlockSpec(bblk, bidx)
    aliases = {}
    if into is not None:
        stack, layer = into
        assert not extras and len(out_shapes) == 1 and stack.shape[1:] == (M, N)
        extras, extra_specs = (stack,), (lambda tm, tn: pl.BlockSpec(memory_space=pl.ANY),)
        out_shapes = [_sds(stack.shape, stack.dtype)]
        out_specs = [lambda tm, tn: pl.BlockSpec((None, tm, tn), lambda i, j, k: (layer, i, j))]
        aliases = {2: 0}
    ne, no = len(extras), len(out_shapes)
    if out_specs is None:
        out_specs = [lambda tm, tn: pl.BlockSpec((tm, tn), lambda i, j, k: (i, j))] * no
    dims = _DIMS[mode]

    def body(*refs):
        a_ref, b_ref = refs[0], refs[1]
        extra, outs, acc = refs[2:2 + ne], refs[2 + ne:2 + ne + no], refs[-1]
        k = pl.program_id(2)

        @pl.when(k == 0)
        def _():
            acc[...] = jnp.zeros_like(acc)

        acc[...] += lax.dot_general(a_ref[...], b_ref[...], dims, preferred_element_type=F32)

        @pl.when(k == nk - 1)
        def _():
            epilogue(acc[...], extra, outs)

    return pl.pallas_call(
        body, name=name, grid=(M // tm, N // tn, nk),
        in_specs=[a_spec, b_spec] + [s(tm, tn) for s in extra_specs],
        out_specs=[s(tm, tn) for s in out_specs],
        out_shape=list(out_shapes),
        scratch_shapes=[pltpu.VMEM((tm, tn), F32)],
        input_output_aliases=aliases,
        compiler_params=_cparams(("parallel", "parallel", "arbitrary")),
    )(a, b, *extras)


def _tile_spec(tm, tn):
    return pl.BlockSpec((tm, tn), lambda i, j, k: (i, j))


def _row_spec(width):
    return lambda tm, tn: pl.BlockSpec((tm, width), lambda i, j, k: (i, 0))


def _col_spec(off_cols):
    def make(tm, tn):
        assert off_cols % tn == 0, (off_cols, tn)
        return pl.BlockSpec((tm, tn), lambda i, j, k: (i, off_cols // tn + j))
    return make


def _rms_fwd(name, x, g):
    S, D = x.shape
    tm = _tile(S, 256, 8)

    def body(x_ref, g_ref, u_ref, r_ref):
        xv = x_ref[...]
        r = lax.rsqrt(jnp.mean(xv * xv, axis=-1, keepdims=True) + EPS)
        u_ref[...] = (xv * r * g_ref[...]).astype(BF16)
        r_ref[...] = r

    return pl.pallas_call(
        body, name=name, grid=(S // tm,),
        in_specs=[pl.BlockSpec((tm, D), lambda i: (i, 0)), pl.BlockSpec((1, D), lambda i: (0, 0))],
        out_specs=[pl.BlockSpec((tm, D), lambda i: (i, 0)), pl.BlockSpec((tm, 1), lambda i: (i, 0))],
        out_shape=[_sds((S, D), BF16), _sds((S, 1), F32)],
        compiler_params=_cparams(("parallel",)),
    )(x, g)


def _rms_bwd(name, x, r, g, du, dres):
    S, D = x.shape
    tm = _tile(S, 256, 8)

    def body(x_ref, r_ref, g_ref, du_ref, dres_ref, dx_ref, dxb_ref, dg_ref):
        rv = r_ref[...]
        xhat = x_ref[...] * rv
        duv = du_ref[...].astype(F32)
        dxh = duv * g_ref[...]
        m = jnp.mean(dxh * xhat, axis=-1, keepdims=True)
        dx = dres_ref[...] + rv * (dxh - xhat * m)
        dx_ref[...] = dx
        dxb_ref[...] = dx.astype(BF16)

        @pl.when(pl.program_id(0) == 0)
        def _():
            dg_ref[...] = jnp.zeros_like(dg_ref)

        dg_ref[...] += jnp.sum(duv * xhat, axis=0, keepdims=True)

    row = pl.BlockSpec((tm, D), lambda i: (i, 0))
    vec = pl.BlockSpec((1, D), lambda i: (0, 0))
    return pl.pallas_call(
        body, name=name, grid=(S // tm,),
        in_specs=[row, pl.BlockSpec((tm, 1), lambda i: (i, 0)), vec, row, row],
        out_specs=[row, row, vec],
        out_shape=[_sds((S, D), F32), _sds((S, D), BF16), _sds((1, D), F32)],
        compiler_params=_cparams(("arbitrary",)),
    )(x, r, g, du, dres)


def _final_loss(x, t, g):
    S, D = x.shape
    tm = _tile(S, 256, 8)

    def body(x_ref, t_ref, g_ref, loss_ref, dx_ref, dxb_ref, dg_ref):
        xv = x_ref[...]
        gv = g_ref[...]
        rv = lax.rsqrt(jnp.mean(xv * xv, axis=-1, keepdims=True) + EPS)
        xhat = xv * rv
        diff = xhat * gv - t_ref[...]
        dy = diff * (1.0 / D)
        dxh = dy * gv
        m = jnp.mean(dxh * xhat, axis=-1, keepdims=True)
        dx = rv * (dxh - xhat * m)
        dx_ref[...] = dx
        dxb_ref[...] = dx.astype(BF16)

        @pl.when(pl.program_id(0) == 0)
        def _():
            dg_ref[...] = jnp.zeros_like(dg_ref)
            loss_ref[...] = jnp.zeros_like(loss_ref)

        dg_ref[...] += jnp.sum(dy * xhat, axis=0, keepdims=True)
        per_row = jnp.mean(diff * diff, axis=-1, keepdims=True)
        loss_ref[...] += 0.5 * jnp.sum(per_row, axis=0, keepdims=True)

    row = pl.BlockSpec((tm, D), lambda i: (i, 0))
    vec = pl.BlockSpec((1, D), lambda i: (0, 0))
    return pl.pallas_call(
        body, name="final_loss", grid=(S // tm,),
        in_specs=[row, row, vec],
        out_specs=[pl.BlockSpec((1, 128), lambda i: (0, 0)), row, row, vec],
        out_shape=[_sds((1, 128), F32), _sds((S, D), F32), _sds((S, D), BF16), _sds((1, D), F32)],
        compiler_params=_cparams(("arbitrary",)),
    )(x, t, g)


def _rope128(v, cos_t, sin_t):
    lane = lax.broadcasted_iota(jnp.int32, v.shape, 1)
    up = pltpu.roll(v, 128 - ROPE // 2, 1)
    dn = pltpu.roll(v, ROPE // 2, 1)
    return v * cos_t + jnp.where(lane < ROPE // 2, up, dn) * sin_t


def _lat_fwd(l, proj, g_qa, g_kva, cos_t, sin_t, off_cq, off_ckv, off_kpe):
    S = proj.shape[0]
    QL, KL = g_qa.shape[1], g_kva.shape[1]
    tm = _tile(S, 512, 8)
    assert off_cq % QL == 0 and off_ckv % KL == 0 and off_kpe % 128 == 0

    def body(cq_ref, ckv_ref, kpe_ref, gq_ref, gkv_ref, cos_ref, sin_ref, nq_ref, nkv_ref, rq_ref, rkv_ref, kp_ref):
        for c_ref, g_ref, n_ref, r_ref in ((cq_ref, gq_ref, nq_ref, rq_ref), (ckv_ref, gkv_ref, nkv_ref, rkv_ref)):
            cv = c_ref[...].astype(F32)
            r = lax.rsqrt(jnp.mean(cv * cv, axis=-1, keepdims=True) + EPS)
            n_ref[...] = (cv * r * g_ref[...]).astype(BF16)
            r_ref[...] = r
        kp_ref[...] = _rope128(kpe_ref[...].astype(F32), cos_ref[...], sin_ref[...]).astype(BF16)

    col = lambda w, off: pl.BlockSpec((tm, w), lambda i: (i, off // w))
    row = lambda w: pl.BlockSpec((tm, w), lambda i: (i, 0))
    vec = lambda w: pl.BlockSpec((1, w), lambda i: (0, 0))
    return pl.pallas_call(
        body, name=f"lat_fwd_{l}", grid=(S // tm,),
        in_specs=[col(QL, off_cq), col(KL, off_ckv), col(128, off_kpe), vec(QL), vec(KL), row(128), row(128)],
        out_specs=[row(QL), row(KL), row(1), row(1), row(128)],
        out_shape=[_sds((S, QL), BF16), _sds((S, KL), BF16), _sds((S, 1), F32), _sds((S, 1), F32), _sds((S, 128), BF16)],
        compiler_params=_cparams(("parallel",)),
    )(proj, proj, proj, g_qa, g_kva, cos_t, sin_t)


def _lat_bwd(l, proj, rq, rkv, g_qa, g_kva, dnq, dnkv, dkp_h, cos_t, sin_t, off_cq, off_ckv):
    S = proj.shape[0]
    QL, KL = g_qa.shape[1], g_kva.shape[1]
    H = dkp_h.shape[0]
    tm = _tile(S, 512, 8)

    def body(cq_ref, ckv_ref, rq_ref, rkv_ref, gq_ref, gkv_ref, dnq_ref, dnkv_ref, dkp_ref, cos_ref, sin_ref,
             dcq_ref, dckv_ref, dkpe_ref, dgq_ref, dgkv_ref):
        first = pl.program_id(0) == 0
        for c_ref, r_ref, g_ref, dn_ref, dc_ref, dg_ref in (
                (cq_ref, rq_ref, gq_ref, dnq_ref, dcq_ref, dgq_ref),
                (ckv_ref, rkv_ref, gkv_ref, dnkv_ref, dckv_ref, dgkv_ref)):
            rv = r_ref[...]
            xhat = c_ref[...].astype(F32) * rv
            dn = dn_ref[...]
            dxh = dn * g_ref[...]
            m = jnp.mean(dxh * xhat, axis=-1, keepdims=True)
            dc_ref[...] = (rv * (dxh - xhat * m)).astype(BF16)

            @pl.when(first)
            def _():
                dg_ref[...] = jnp.zeros_like(dg_ref)

            dg_ref[...] += jnp.sum(dn * xhat, axis=0, keepdims=True)
        dkp = dkp_ref[0]
        for h in range(1, H):
            dkp = dkp + dkp_ref[h]
        dkpe_ref[...] = _rope128(dkp, cos_ref[...], -sin_ref[...]).astype(BF16)

    col = lambda w, off: pl.BlockSpec((tm, w), lambda i: (i, off // w))
    row = lambda w: pl.BlockSpec((tm, w), lambda i: (i, 0))
    vec = lambda w: pl.BlockSpec((1, w), lambda i: (0, 0))
    return pl.pallas_call(
        body, name=f"lat_bwd_{l}", grid=(S // tm,),
        in_specs=[col(QL, off_cq), col(KL, off_ckv), row(1), row(1), vec(QL), vec(KL), row(QL), row(KL),
                  pl.BlockSpec((H, tm, 128), lambda i: (0, i, 0)), row(128), row(128)],
        out_specs=[row(QL), row(KL), row(128), vec(QL), vec(KL)],
        out_shape=[_sds((S, QL), BF16), _sds((S, KL), BF16), _sds((S, 128), BF16), _sds((1, QL), F32), _sds((1, KL), F32)],
        compiler_params=_cparams(("arbitrary",)),
    )(proj, proj, rq, rkv, g_qa, g_kva, dnq, dnkv, dkp_h, cos_t, sin_t)


def _rope_bwd_q(l, dq, cos_t, sin_t, scale):
    S, W = dq.shape
    tm = _tile(S, 256, 8)
    nh = W // 256

    def body(dq_ref, cos_ref, sin_ref, out_ref):
        cv, sv = cos_ref[...] * scale, -sin_ref[...] * scale
        for h in range(nh):
            out_ref[:, h * 256:h * 256 + 128] = (dq_ref[:, h * 256:h * 256 + 128] * scale).astype(BF16)
            out_ref[:, h * 256 + 128:(h + 1) * 256] = _rope128(dq_ref[:, h * 256 + 128:(h + 1) * 256], cv, sv).astype(BF16)

    return pl.pallas_call(
        body, name=f"rope_bwd_q_{l}", grid=(S // tm,),
        in_specs=[pl.BlockSpec((tm, W), lambda i: (i, 0)), pl.BlockSpec((tm, 128), lambda i: (i, 0)),
                  pl.BlockSpec((tm, 128), lambda i: (i, 0))],
        out_specs=pl.BlockSpec((tm, W), lambda i: (i, 0)),
        out_shape=_sds((S, W), BF16),
        compiler_params=_cparams(("parallel",)),
    )(dq, cos_t, sin_t)


def _delta(l, o, do):
    S, W = o.shape
    H = W // HEAD
    tm = _tile(S, 1024, 8)

    def body(o_ref, do_ref, d_ref):
        d_ref[...] = jnp.sum(o_ref[...].astype(F32) * do_ref[...].astype(F32), axis=-1, keepdims=True)

    blk = pl.BlockSpec((tm, HEAD), lambda h, i: (i, h))
    return pl.pallas_call(
        body, name=f"delta_{l}", grid=(H, S // tm),
        in_specs=[blk, blk],
        out_specs=pl.BlockSpec((None, tm, 1), lambda h, i: (h, i, 0)),
        out_shape=_sds((H, S, 1), F32),
        compiler_params=_cparams(("parallel", "parallel")),
    )(o, do)


_NT = (((1,), (1,)), ((), ()))
_TN = (((0,), (0,)), ((), ()))


MLA_SUB = 256


def _mla_fwd(l, q, kv, kp):
    S = q.shape[0]
    H = q.shape[1] // 256
    tq, tk = _tile(S, 1024, 8), _tile(S, 512, 128)
    sub = min(MLA_SUB, tq)
    nk = S // tk

    def body(q_ref, kn_ref, v_ref, kp_ref, o_ref, lse_ref, m_sc, acc_sc):
        ki = pl.program_id(2)

        @pl.when(ki == 0)
        def _():
            m_sc[...] = jnp.full_like(m_sc, NEG)
            acc_sc[...] = jnp.zeros_like(acc_sc)

        kc = jnp.concatenate([kn_ref[...], kp_ref[...]], axis=1)
        vx = jnp.concatenate([v_ref[...], jnp.ones((tk, 128), BF16)], axis=1)
        for r in range(tq // sub):
            rows = slice(r * sub, (r + 1) * sub)
            s = lax.dot_general(q_ref[rows, :], kc, _NT, preferred_element_type=F32)
            m_prev = m_sc[rows, :]
            m_new = jnp.maximum(m_prev, jnp.max(s, axis=-1, keepdims=True))
            alpha = jnp.exp(m_prev - m_new)
            p = jnp.exp(s - jnp.tile(m_new, (1, tk // 128)))
            acc_sc[rows, :] = (jnp.tile(alpha, (1, 2)) * acc_sc[rows, :]
                               + jnp.dot(p.astype(BF16), vx, preferred_element_type=F32))
            m_sc[rows, :] = m_new

        @pl.when(ki == nk - 1)
        def _():
            l = acc_sc[:, HEAD:]
            o_ref[...] = (acc_sc[:, :HEAD] / l).astype(BF16)
            lse_ref[...] = m_sc[:, :1] + jnp.log(l[:, :1])

    return pl.pallas_call(
        body, name=f"mla_fwd_{l}", grid=(H, S // tq, nk),
        in_specs=[pl.BlockSpec((tq, 256), lambda h, i, k: (i, h)),
                  pl.BlockSpec((tk, HEAD), lambda h, i, k: (k, 2 * h)),
                  pl.BlockSpec((tk, HEAD), lambda h, i, k: (k, 2 * h + 1)),
                  pl.BlockSpec((tk, 128), lambda h, i, k: (k, 0))],
        out_specs=[pl.BlockSpec((tq, HEAD), lambda h, i, k: (i, h)),
                   pl.BlockSpec((None, tq, 1), lambda h, i, k: (h, i, 0))],
        out_shape=[_sds((S, H * HEAD), BF16), _sds((H, S, 1), F32)],
        scratch_shapes=[pltpu.VMEM((tq, 128), F32), pltpu.VMEM((tq, 2 * HEAD), F32)],
        compiler_params=_cparams(("parallel", "parallel", "arbitrary")),
    )(q, kv, kv, kp)


def _mla_bwd(l, q, kv, kp, do, lse, delta):
    S = q.shape[0]
    H = q.shape[1] // 256
    tq, tk = _tile(S, 1024, 8), _tile(S, 512, 128)
    sub = min(MLA_SUB, tq)
    nq = S // tq

    def body(q_ref, kn_ref, v_ref, kp_ref, do_ref, lse_ref, dl_ref, dq_ref, dkv_ref, dkp_ref, dkc_sc, dv_sc):
        ki, qi = pl.program_id(1), pl.program_id(2)
        kc = jnp.concatenate([kn_ref[...], kp_ref[...]], axis=1)
        vv = v_ref[...]
        dkc, dv, dq_tiles = None, None, []
        for r in range(tq // sub):
            rows = slice(r * sub, (r + 1) * sub)
            qv, dov = q_ref[rows, :], do_ref[rows, :]
            s = lax.dot_general(qv, kc, _NT, preferred_element_type=F32)
            p = jnp.exp(s - lse_ref[rows, :])
            dv_r = lax.dot_general(p.astype(BF16), dov, _TN, preferred_element_type=F32)
            dp = lax.dot_general(dov, vv, _NT, preferred_element_type=F32)
            ds = (p * (dp - dl_ref[rows, :])).astype(BF16)
            dkc_r = lax.dot_general(ds, qv, _TN, preferred_element_type=F32)
            dq_tiles.append(jnp.dot(ds, kc, preferred_element_type=F32))
            dkc = dkc_r if dkc is None else dkc + dkc_r
            dv = dv_r if dv is None else dv + dv_r
        dq_tile = jnp.concatenate(dq_tiles, axis=0) if len(dq_tiles) > 1 else dq_tiles[0]

        @pl.when(qi == 0)
        def _():
            dkc_sc[...] = dkc
            dv_sc[...] = dv

        @pl.when(qi > 0)
        def _():
            dkc_sc[...] += dkc
            dv_sc[...] += dv

        rows = pl.ds(pl.multiple_of(qi * tq, tq), tq)

        @pl.when(ki == 0)
        def _():
            dq_ref[rows, :] = dq_tile

        @pl.when(ki > 0)
        def _():
            dq_ref[rows, :] += dq_tile

        @pl.when(qi == nq - 1)
        def _():
            dkv_ref[:, :HEAD] = dkc_sc[:, :HEAD].astype(BF16)
            dkv_ref[:, HEAD:] = dv_sc[...].astype(BF16)
            dkp_ref[...] = dkc_sc[:, HEAD:]

    return pl.pallas_call(
        body, name=f"mla_bwd_{l}", grid=(H, S // tk, nq),
        in_specs=[pl.BlockSpec((tq, 256), lambda h, k, i: (i, h)),
                  pl.BlockSpec((tk, HEAD), lambda h, k, i: (k, 2 * h)),
                  pl.BlockSpec((tk, HEAD), lambda h, k, i: (k, 2 * h + 1)),
                  pl.BlockSpec((tk, 128), lambda h, k, i: (k, 0)),
                  pl.BlockSpec((tq, HEAD), lambda h, k, i: (i, h)),
                  pl.BlockSpec((None, tq, 1), lambda h, k, i: (h, i, 0)),
                  pl.BlockSpec((None, tq, 1), lambda h, k, i: (h, i, 0))],
        out_specs=[pl.BlockSpec((S, 256), lambda h, k, i: (0, h)),
                   pl.BlockSpec((tk, 256), lambda h, k, i: (k, h)),
                   pl.BlockSpec((None, tk, 128), lambda h, k, i: (h, k, 0))],
        out_shape=[_sds((S, H * 256), F32), _sds((S, H * 256), BF16), _sds((H, S, 128), F32)],
        scratch_shapes=[pltpu.VMEM((tk, 256), F32), pltpu.VMEM((tk, HEAD), F32)],
        compiler_params=_cparams(("parallel", "arbitrary", "arbitrary")),
    )(q, kv, kv, kp, do, lse, delta)


NA_RB = 8


def _na_bias_index(rows):
    j = np.arange(NA_KH)
    dy = j[None, :] - (np.arange(8)[:, None] - 4) + 3
    c = np.arange(GRID_W)
    col_start = np.clip(c - NA_KW // 2, 0, GRID_W - NA_KW)
    ok = (c[None, :] >= col_start[:, None]) & (c[None, :] < col_start[:, None] + NA_KW)
    dx = np.clip(c[None, :] - c[:, None], -(NA_KW - 1), NA_KW - 1) + (NA_KW - 1)
    dy_full = np.broadcast_to(dy[:, None, :, None], (8, GRID_W, NA_KH, GRID_W)).reshape(8, GRID_W, NA_KH * GRID_W)
    dx_full = np.broadcast_to(dx[None, :, None, :], (8, GRID_W, NA_KH, GRID_W)).reshape(8, GRID_W, NA_KH * GRID_W)
    ok_full = np.broadcast_to(ok[None, :, None, :], (8, GRID_W, NA_KH, GRID_W)).reshape(8, GRID_W, NA_KH * GRID_W)
    valid = ok_full & (dy_full >= 0) & (dy_full <= 2 * NA_KH - 2)
    return np.clip(dy_full, 0, 2 * NA_KH - 2), dx_full, valid


def _na_bias(rpb, dx_masked):
    L, H, NY, NX = rpb.shape
    nkeys = NA_KH * GRID_W

    def body(rpb_ref, dx_ref, out_ref):
        base = (pl.program_id(0) * H + pl.program_id(1)) * (NY * NX)
        dxv = dx_ref[...]
        key_row = lax.shift_right_logical(lax.broadcasted_iota(jnp.int32, (1, nkeys), 1), 6)

        def variant(o, carry):
            acc = jnp.full((GRID_W, nkeys), NEG, F32)
            for xx in range(NX):
                row = jnp.zeros((1, nkeys), F32)
                for j in range(NA_KH):
                    row = jnp.where(key_row == j, rpb_ref[base + (j - o + NA_KH - 1) * NX + xx], row)
                acc = jnp.where(dxv == xx, row, acc)
            out_ref[o] = acc
            return carry

        lax.fori_loop(0, 8, variant, 0)

    return pl.pallas_call(
        body, name="na_bias", grid=(L, H),
        in_specs=[pl.BlockSpec(memory_space=pltpu.SMEM), pl.BlockSpec((GRID_W, nkeys), lambda l, h: (0, 0))],
        out_specs=pl.BlockSpec((None, None, 8, GRID_W, nkeys), lambda l, h: (l, h, 0, 0, 0)),
        out_shape=_sds((L, H, 8, GRID_W, nkeys), F32),
        compiler_params=_cparams(("parallel", "parallel")),
    )(rpb.reshape(-1), dx_masked)


def _na_row_window(rb, i, rows):
    r = rb * NA_RB + i
    ks = jnp.clip(r - NA_KH // 2, 0, rows - NA_KH)
    variant = r - ks
    return pl.ds(pl.multiple_of(ks * GRID_W, GRID_W), NA_KH * GRID_W), variant


def _na_fwd(l, proj, brow, off_q, off_k, off_v, scale):
    S = proj.shape[0]
    H = brow.shape[1]
    rows = S // GRID_W
    assert rows % NA_RB == 0 and rows >= NA_KH
    tq = NA_RB * GRID_W

    def body(q_ref, k_ref, v_ref, b_ref, o_ref):
        rb = pl.program_id(1)
        for i in range(NA_RB):
            win, variant = _na_row_window(rb, i, rows)
            qs = slice(i * GRID_W, (i + 1) * GRID_W)
            s = lax.dot_general(q_ref[qs, :], k_ref[win, :], _NT, preferred_element_type=F32) * scale + b_ref[variant]
            e = jnp.exp(s - jnp.max(s, axis=-1, keepdims=True))
            p = e / jnp.sum(e, axis=-1, keepdims=True)
            o_ref[qs, :] = jnp.dot(p.astype(BF16), v_ref[win, :], preferred_element_type=F32).astype(BF16)

    return pl.pallas_call(
        body, name=f"na_fwd_{l}", grid=(H, rows // NA_RB),
        in_specs=[pl.BlockSpec((tq, HEAD), lambda h, r: (r, off_q // HEAD + h)),
                  pl.BlockSpec((S, HEAD), lambda h, r: (0, off_k // HEAD + h)),
                  pl.BlockSpec((S, HEAD), lambda h, r: (0, off_v // HEAD + h)),
                  pl.BlockSpec((None, None, 8, GRID_W, NA_KH * GRID_W), lambda h, r: (l, h, 0, 0, 0))],
        out_specs=pl.BlockSpec((tq, HEAD), lambda h, r: (r, h)),
        out_shape=_sds((S, H * HEAD), BF16),
        compiler_params=_cparams(("parallel", "arbitrary")),
    )(proj, proj, proj, brow)


def _na_bwd(l, proj, brow, o, do, off_q, off_k, off_v, scale):
    S = proj.shape[0]
    H = brow.shape[1]
    rows = S // GRID_W
    tq = NA_RB * GRID_W

    def body(q_ref, k_ref, v_ref, b_ref, o_ref, do_ref, dq_ref, dk_ref, dv_ref, db_ref):
        rb = pl.program_id(1)

        @pl.when(rb == 0)
        def _():
            dk_ref[...] = jnp.zeros_like(dk_ref)
            dv_ref[...] = jnp.zeros_like(dv_ref)
            db_ref[...] = jnp.zeros_like(db_ref)

        for i in range(NA_RB):
            win, variant = _na_row_window(rb, i, rows)
            qs = slice(i * GRID_W, (i + 1) * GRID_W)
            qv, kw, vw, dov = q_ref[qs, :], k_ref[win, :], v_ref[win, :], do_ref[qs, :]
            s = lax.dot_general(qv, kw, _NT, preferred_element_type=F32) * scale + b_ref[variant]
            e = jnp.exp(s - jnp.max(s, axis=-1, keepdims=True))
            p = e / jnp.sum(e, axis=-1, keepdims=True)
            dv_ref[win, :] += lax.dot_general(p.astype(BF16), dov, _TN, preferred_element_type=F32)
            dp = lax.dot_general(dov, vw, _NT, preferred_element_type=F32)
            dl = jnp.sum(dov.astype(F32) * o_ref[qs, :].astype(F32), axis=-1, keepdims=True)
            ds = p * (dp - dl)
            db_ref[variant] += ds
            dsb = (ds * scale).astype(BF16)
            dq_ref[qs, :] = jnp.dot(dsb, kw, preferred_element_type=F32).astype(BF16)
            dk_ref[win, :] += lax.dot_general(dsb, qv, _TN, preferred_element_type=F32)

    qblk = pl.BlockSpec((tq, HEAD), lambda h, r: (r, h))
    full = pl.BlockSpec((S, HEAD), lambda h, r: (0, h))
    bias = pl.BlockSpec((None, None, 8, GRID_W, NA_KH * GRID_W), lambda h, r: (l, h, 0, 0, 0))
    dbias = pl.BlockSpec((None, 8, GRID_W, NA_KH * GRID_W), lambda h, r: (h, 0, 0, 0))
    return pl.pallas_call(
        body, name=f"na_bwd_{l}", grid=(H, rows // NA_RB),
        in_specs=[pl.BlockSpec((tq, HEAD), lambda h, r: (r, off_q // HEAD + h)),
                  pl.BlockSpec((S, HEAD), lambda h, r: (0, off_k // HEAD + h)),
                  pl.BlockSpec((S, HEAD), lambda h, r: (0, off_v // HEAD + h)),
                  bias, qblk, qblk],
        out_specs=[qblk, full, full, dbias],
        out_shape=[_sds((S, H * HEAD), BF16), _sds((S, H * HEAD), F32), _sds((S, H * HEAD), F32),
                   _sds(brow.shape[1:], F32)],
        compiler_params=_cparams(("parallel", "arbitrary")),
    )(proj, proj, proj, brow, o, do)


def _place():
    return lax.axis_index("x"), lax.axis_index("y"), lax.axis_index("c")


def _full_shape(shard_shape, kind):
    L, A, B = shard_shape
    return {"col": (L, A, N_CHIPS * B), "row": (L, N_CHIPS * A, B), "slot": (N_CHIPS, L, A, B)}[kind]


def _shard_region(ref, kind, shard_shape, k, half=None):
    L, A, B = shard_shape
    lo, n = (0, A) if half is None else (pl.multiple_of(half * (A // 2), 16), A // 2)
    if kind == "col":
        return ref.at[:, pl.ds(lo, n), pl.ds(pl.multiple_of(k * B, 128), B)]
    if kind == "row":
        return ref.at[:, pl.ds(pl.multiple_of(k * A + lo, 16), n), :]
    return ref.at[k, :, pl.ds(lo, n), :]


def _weight_gather(shards, kinds):
    n = len(shards)
    shapes = [s.shape for s in shards]
    assert all(s[1] % 32 == 0 for s in shapes)

    def body(*refs):
        w, o = refs[:n], refs[n:2 * n]
        send_sems, recv_sems, local_sems = refs[2 * n:]
        x, y, c = _place()
        sibling = (x, y, 1 - c)
        chips = [(1 - x, y), (x, 1 - y), (1 - x, 1 - y)]

        def copy(k, src, dst, to):
            return pltpu.make_async_remote_copy(src_ref=src, dst_ref=dst, send_sem=send_sems.at[k],
                                                recv_sem=recv_sems.at[k], device_id=to, device_id_type=MESH)

        def region(i, cx, cy, half=None):
            return _shard_region(o[i], kinds[i], shapes[i], 2 * cx + cy, half)

        def my_half(i):
            A = shapes[i][1]
            return w[i].at[:, pl.ds(pl.multiple_of(c * (A // 2), 16), A // 2), :]

        local = [pltpu.make_async_copy(w[i], region(i, x, y), local_sems.at[i]) for i in range(n)]
        for cp in local:
            cp.start()
        first = [copy(6 * i + j, my_half(i), region(i, x, y, c), (*chip, c))
                 for i in range(n) for j, chip in enumerate(chips)]
        for cp in first:
            cp.start()
        passed = []
        for i in range(n):
            for j, chip in enumerate(chips):
                landed = region(i, *chip, c)
                copy(6 * i + j, landed, landed, (*chip, c)).wait_recv()
                passed.append(copy(6 * i + 3 + j, landed, landed, sibling))
                passed[-1].start()
        for i in range(n):
            for j, chip in enumerate(chips):
                other = region(i, *chip, 1 - c)
                copy(6 * i + 3 + j, other, other, sibling).wait_recv()
        for cp in first + passed:
            cp.wait_send()
        for cp in local:
            cp.wait()

    hbm = pl.BlockSpec(memory_space=pl.ANY)
    return pl.pallas_call(
        body, name="weight_gather",
        in_specs=[hbm] * n, out_specs=[hbm] * n,
        out_shape=[_sds(_full_shape(s.shape, k), s.dtype) for s, k in zip(shards, kinds)],
        scratch_shapes=[pltpu.SemaphoreType.DMA((6 * n,)), pltpu.SemaphoreType.DMA((6 * n,)),
                        pltpu.SemaphoreType.DMA((n,))],
    )(*shards)


def _grad_scatter(grads, kinds, shapes):
    n = len(grads)

    def body(*refs):
        g, outs = refs[:n], refs[n:3 * n]
        send_sems, recv_sems, local_sems = refs[3 * n:]
        own, got = outs[0::2], outs[1::2]
        x, y, c = _place()
        chips = [(1 - x, y), (x, 1 - y), (1 - x, 1 - y)]
        local = [pltpu.make_async_copy(_shard_region(g[i], kinds[i], shapes[i], 2 * x + y), own[i], local_sems.at[i])
                 for i in range(n)]
        for cp in local:
            cp.start()
        sends = [pltpu.make_async_remote_copy(
            src_ref=_shard_region(g[i], kinds[i], shapes[i], 2 * cx + cy), dst_ref=got[i].at[j],
            send_sem=send_sems.at[3 * i + j], recv_sem=recv_sems.at[3 * i + j],
            device_id=(cx, cy, c), device_id_type=MESH) for i in range(n) for j, (cx, cy) in enumerate(chips)]
        for cp in sends:
            cp.start()
        for cp in sends:
            cp.wait_recv()
        for cp in sends:
            cp.wait_send()
        for cp in local:
            cp.wait()

    hbm = pl.BlockSpec(memory_space=pl.ANY)
    out_shape = []
    for gr, s in zip(grads, shapes):
        out_shape += [_sds(tuple(s), gr.dtype), _sds((3,) + tuple(s), gr.dtype)]
    outs = pl.pallas_call(
        body, name="grad_scatter",
        in_specs=[hbm] * n, out_specs=[hbm] * (2 * n), out_shape=out_shape,
        scratch_shapes=[pltpu.SemaphoreType.DMA((3 * n,)), pltpu.SemaphoreType.DMA((3 * n,)),
                        pltpu.SemaphoreType.DMA((n,))],
    )(*grads)
    return outs[0::2], outs[1::2]


def _grad_partial(name, own, got):
    R, W = own.shape
    tm = _tile(R, max(16, (1 << 19) // W // 16 * 16), 16)

    def body(own_ref, got_ref, out_ref):
        acc = own_ref[...].astype(F32)
        for j in range(3):
            acc = acc + got_ref[j].astype(F32)
        out_ref[...] = acc

    return pl.pallas_call(
        body, name=name, grid=(R // tm,),
        in_specs=[pl.BlockSpec((tm, W), lambda i: (i, 0)), pl.BlockSpec((3, tm, W), lambda i: (0, i, 0))],
        out_specs=pl.BlockSpec((tm, W), lambda i: (i, 0)),
        out_shape=_sds((R, W), F32),
        compiler_params=_cparams(("parallel",)),
    )(own, got)


def _sibling_swap(parts):
    n = len(parts)

    def body(*refs):
        p, got = refs[:n], refs[n:2 * n]
        send_sems, recv_sems = refs[2 * n:]
        x, y, c = _place()
        copies = [pltpu.make_async_remote_copy(src_ref=p[i], dst_ref=got[i], send_sem=send_sems.at[i],
                                               recv_sem=recv_sems.at[i], device_id=(x, y, 1 - c), device_id_type=MESH)
                  for i in range(n)]
        for cp in copies:
            cp.start()
        for cp in copies:
            cp.wait()

    hbm = pl.BlockSpec(memory_space=pl.ANY)
    return pl.pallas_call(
        body, name="sibling_swap",
        in_specs=[hbm] * n, out_specs=[hbm] * n,
        out_shape=[_sds(p.shape, p.dtype) for p in parts],
        scratch_shapes=[pltpu.SemaphoreType.DMA((n,)), pltpu.SemaphoreType.DMA((n,))],
    )(*parts)


def _small_allreduce(vec):
    NR, W = vec.shape

    def body(v_ref, all_ref, sum_ref, send_sems, recv_sems):
        x, y, c = _place()
        me = 4 * x + 2 * y + c
        all_ref[me] = v_ref[...]
        copies = []
        for k in range(1, N_DEV):
            fx, fy, fc = (k >> 2) & 1, (k >> 1) & 1, k & 1
            peer = (x ^ fx, y ^ fy, c ^ fc)
            copies.append(pltpu.make_async_remote_copy(
                src_ref=v_ref, dst_ref=all_ref.at[me], send_sem=send_sems.at[k - 1], recv_sem=recv_sems.at[k - 1],
                device_id=peer, device_id_type=MESH))
        for cp in copies:
            cp.start()
        for cp in copies:
            cp.wait_recv()
        for cp in copies:
            cp.wait_send()
        acc = all_ref[0]
        for d in range(1, N_DEV):
            acc = acc + all_ref[d]
        sum_ref[...] = acc

    return pl.pallas_call(
        body, name="small_allreduce",
        in_specs=[pl.BlockSpec(memory_space=pltpu.VMEM)],
        out_specs=[pl.BlockSpec(memory_space=pltpu.VMEM), pl.BlockSpec(memory_space=pltpu.VMEM)],
        out_shape=[_sds((N_DEV, NR, W), F32), _sds((NR, W), F32)],
        scratch_shapes=[pltpu.SemaphoreType.DMA((N_DEV - 1,)), pltpu.SemaphoreType.DMA((N_DEV - 1,))],
    )(vec)[1]


def _adamw_math(g, w, m, v):
    m = ADAM_B1 * m + (1.0 - ADAM_B1) * g
    v = ADAM_B2 * v + (1.0 - ADAM_B2) * (g * g)
    m_hat = m / (1.0 - ADAM_B1 ** ADAM_STEP)
    v_hat = v / (1.0 - ADAM_B2 ** ADAM_STEP)
    delta = -ADAM_LR * (m_hat / (jnp.sqrt(v_hat) + ADAM_EPS) + ADAM_WD * w)
    return delta, m, v


def _adamw(name, ga, gb, w, m, v):
    rows, n = w.shape
    tm = _tile(rows, max(8, (1 << 18) // n // 8 * 8), 8)

    def body(ga_ref, gb_ref, w_ref, m_ref, v_ref, g_out, d_out, m_out, v_out):
        g = ga_ref[...] + gb_ref[...]
        delta, mn, vn = _adamw_math(g, w_ref[...], m_ref[...], v_ref[...])
        g_out[...] = g
        d_out[...] = delta
        m_out[...] = mn
        v_out[...] = vn

    blk = pl.BlockSpec((tm, n), lambda i: (i, 0))
    return pl.pallas_call(
        body, name=name, grid=(rows // tm,),
        in_specs=[blk] * 5, out_specs=[blk] * 4, out_shape=[_sds((rows, n), F32)] * 4,
        compiler_params=_cparams(("parallel",)),
    )(ga, gb, w, m, v)


def _pack(parts):
    flat = jnp.concatenate([p.reshape(-1) for p in parts])
    pad = (-flat.shape[0]) % 1024
    if pad:
        flat = jnp.concatenate([flat, jnp.zeros((pad,), flat.dtype)])
    return flat.reshape(-1, 128)


def _unpack(flat, shapes):
    flat = flat.reshape(-1)
    out, off = [], 0
    for s in shapes:
        n = int(np.prod(s))
        out.append(flat[off:off + n].reshape(s))
        off += n
    return out


def kernel(x, norm_mix, w_in, norm_qa, w_uq, norm_kva, w_ukv, rpb, w_o_mla, w_o_na, w_out, norm_mlp, w_ff1, w_ff2, norm_final, loss_target, m_norm_mix, m_w_in, m_norm_qa, m_w_uq, m_norm_kva, m_w_ukv, m_rpb, m_w_o_mla, m_w_o_na, m_w_out, m_norm_mlp, m_w_ff1, m_w_ff2, m_norm_final, v_norm_mix, v_w_in, v_norm_qa, v_w_uq, v_norm_kva, v_w_ukv, v_rpb, v_w_o_mla, v_w_o_na, v_w_out, v_norm_mlp, v_w_ff1, v_w_ff2, v_norm_final):
    wts = dict(norm_mix=norm_mix, w_in=w_in, norm_qa=norm_qa, w_uq=w_uq, norm_kva=norm_kva, w_ukv=w_ukv, rpb=rpb,
               w_o_mla=w_o_mla, w_o_na=w_o_na, w_out=w_out, norm_mlp=norm_mlp, w_ff1=w_ff1, w_ff2=w_ff2,
               norm_final=norm_final)
    mom = dict(norm_mix=m_norm_mix, w_in=m_w_in, norm_qa=m_norm_qa, w_uq=m_w_uq, norm_kva=m_norm_kva, w_ukv=m_w_ukv,
               rpb=m_rpb, w_o_mla=m_w_o_mla, w_o_na=m_w_o_na, w_out=m_w_out, norm_mlp=m_norm_mlp, w_ff1=m_w_ff1,
               w_ff2=m_w_ff2, norm_final=m_norm_final)
    var = dict(norm_mix=v_norm_mix, w_in=v_w_in, norm_qa=v_norm_qa, w_uq=v_w_uq, norm_kva=v_norm_kva, w_ukv=v_w_ukv,
               rpb=v_rpb, w_o_mla=v_w_o_mla, w_o_na=v_w_o_na, w_out=v_w_out, norm_mlp=v_norm_mlp, w_ff1=v_w_ff1,
               w_ff2=v_w_ff2, norm_final=v_norm_final)

    _, S, D = x.shape
    L = w_in.shape[0]
    QL, KL = norm_qa.shape[1], norm_kva.shape[1]
    H = w_uq.shape[2] * N_CHIPS // (HEAD + ROPE)
    NAW = w_o_na.shape[1]
    NH = NAW // HEAD
    rows = S // GRID_W
    x = x.reshape(S, D)
    target = loss_target.reshape(S, D)

    kinds = ["slot" if n == "w_in" else ("row" if n in ROW_SHARDED else "col") for n in BIG]
    shard_shapes = [wts[n].shape for n in BIG]
    full = dict(zip(BIG, _weight_gather([wts[n].astype(BF16) for n in BIG], kinds)))

    widths = (QL, KL, ROPE, NAW, NAW, NAW, D, D)
    starts = np.concatenate([[0], np.cumsum(widths)]).astype(int)
    order = (6, 7, 3, 4, 5, 0, 1, 2)
    nloc = w_in.shape[2]
    pieces = []
    for i in order:
        for k in range(N_CHIPS):
            lo, hi = max(int(starts[i]), k * nloc), min(int(starts[i + 1]), (k + 1) * nloc)
            if lo < hi:
                pieces.append(full["w_in"][k, :, :, lo - k * nloc:hi - k * nloc])
    w_in_p = jnp.concatenate(pieces + [jnp.zeros((L, D, 128 - ROPE), BF16)], axis=2)
    new_off = np.concatenate([[0], np.cumsum([widths[i] for i in order])]).astype(int)
    off_ga, off_gb, off_q, off_k, off_v, off_cq, off_ckv, off_kpe = (int(o) for o in new_off[:8])
    PW = w_in_p.shape[2]
    w_uq_p = jnp.pad(full["w_uq"].reshape(L, QL, H, HEAD + ROPE), ((0, 0), (0, 0), (0, 0), (0, 256 - HEAD - ROPE)))
    w_uq_p = w_uq_p.reshape(L, QL, H * 256)

    pos = jnp.arange(S, dtype=F32)
    inv_freq = 1.0 / (ROPE_THETA ** (jnp.arange(0, ROPE, 2, dtype=F32) / ROPE))
    ang = pos[:, None] * inv_freq[None, :]
    cos, sin, zero = jnp.cos(ang), jnp.sin(ang), jnp.zeros((S, 128 - ROPE), F32)
    cos_t = jnp.concatenate([cos, cos, zero], axis=1)
    sin_t = jnp.concatenate([-sin, sin, zero], axis=1)

    dy_idx, dx_idx, bias_ok = _na_bias_index(rows)
    brow = _na_bias(wts["rpb"], jnp.asarray(np.where(bias_ok[0], dx_idx[0], -1), jnp.int32))
    mla_scale = float((HEAD + ROPE) ** -0.5)
    na_scale = float(HEAD ** -0.5)

    def rope_q_epilogue(acc, extra, outs):
        cv, sv = extra[0][...] * mla_scale, extra[1][...] * mla_scale
        for hh in range(acc.shape[1] // 256):
            outs[0][:, hh * 256:hh * 256 + 128] = (acc[:, hh * 256:hh * 256 + 128] * mla_scale).astype(BF16)
            outs[0][:, hh * 256 + 128:(hh + 1) * 256] = _rope128(acc[:, hh * 256 + 128:(hh + 1) * 256], cv, sv).astype(BF16)

    def store_f32(acc, extra, outs):
        outs[0][...] = acc

    def merge_epilogue(acc, extra, outs):
        ga, gb, ya = extra[0][...].astype(F32), extra[1][...].astype(F32), extra[2][...]
        outs[0][...] = (jax.nn.sigmoid(ga) * ya + jax.nn.sigmoid(gb) * acc).astype(BF16)
        outs[1][...] = acc

    def residual_epilogue(acc, extra, outs):
        outs[0][...] = extra[0][...] + acc

    def ff1_epilogue(acc, extra, outs):
        outs[0][...] = acc.astype(BF16)
        outs[1][...] = jnp.square(jnp.maximum(acc, 0.0)).astype(BF16)

    def dff_epilogue(acc, extra, outs):
        outs[0][...] = (acc * (2.0 * jnp.maximum(extra[0][...].astype(F32), 0.0))).astype(BF16)

    def dmerge_epilogue(acc, extra, outs):
        ga, gb = extra[0][...].astype(F32), extra[1][...].astype(F32)
        ya, yb = extra[2][...], extra[3][...]
        sa, sb = jax.nn.sigmoid(ga), jax.nn.sigmoid(gb)
        outs[0][...] = (acc * sa).astype(BF16)
        outs[1][...] = (acc * sb).astype(BF16)
        outs[2][...] = (acc * ya * sa * (1.0 - sa)).astype(BF16)
        outs[3][...] = (acc * yb * sb * (1.0 - sb)).astype(BF16)

    saved = []
    for l in range(L):
        u, r1 = _rms_fwd(f"rms_mix_{l}", x, wts["norm_mix"][l][None])
        proj, = _mm(f"proj_{l}", u, w_in_p, "nn", [_sds((S, PW), BF16)], lb=l, tn_cap=2048, tk_cap=2048)
        nq, nkv, rq, rkv, kp = _lat_fwd(l, proj, wts["norm_qa"][l][None], wts["norm_kva"][l][None], cos_t, sin_t,
                                        off_cq, off_ckv, off_kpe)
        q, = _mm(f"q_up_{l}", nq, w_uq_p, "nn", [_sds((S, H * 256), BF16)], lb=l,
                 extras=(cos_t, sin_t), extra_specs=(_row_spec(128), _row_spec(128)), epilogue=rope_q_epilogue, tn_cap=512)
        kv, = _mm(f"kv_up_{l}", nkv, full["w_ukv"], "nn", [_sds((S, H * 256), BF16)], lb=l)
        o_a, lse = _mla_fwd(l, q, kv, kp)
        o_b = _na_fwd(l, proj, brow, off_q, off_k, off_v, na_scale)
        y_a, = _mm(f"o_mla_{l}", o_a, full["w_o_mla"], "nn", [_sds((S, D), F32)], lb=l, epilogue=store_f32,
                   tn_cap=512)
        merged, y_b = _mm(f"o_na_merge_{l}", o_b, full["w_o_na"], "nn", [_sds((S, D), BF16), _sds((S, D), F32)], lb=l,
                          extras=(proj, proj, y_a), extra_specs=(_col_spec(off_ga), _col_spec(off_gb), _tile_spec),
                          epilogue=merge_epilogue, tn_cap=512)
        x2, = _mm(f"w_out_{l}", merged, full["w_out"], "nn", [_sds((S, D), F32)], lb=l,
                  extras=(x,), extra_specs=(_tile_spec,), epilogue=residual_epilogue, tk_cap=2048)
        u2, r2 = _rms_fwd(f"rms_mlp_{l}", x2, wts["norm_mlp"][l][None])
        h, a = _mm(f"ff1_{l}", u2, full["w_ff1"], "nn", [_sds((S, 4 * D), BF16), _sds((S, 4 * D), BF16)], lb=l,
                   epilogue=ff1_epilogue, tk_cap=2048)
        x3, = _mm(f"ff2_{l}", a, full["w_ff2"], "nn", [_sds((S, D), F32)], lb=l,
                  extras=(x2,), extra_specs=(_tile_spec,), epilogue=residual_epilogue, tk_cap=2048)
        saved.append(dict(x=x, r1=r1, u=u, proj=proj, nq=nq, nkv=nkv, rq=rq, rkv=rkv, q=q, kv=kv, kp=kp, o_a=o_a,
                          lse=lse, o_b=o_b, y_a=y_a, y_b=y_b, merged=merged, x2=x2, r2=r2, u2=u2, h=h, a=a))
        x = x3

    loss_lanes, dx, dxb, dg_final = _final_loss(x, target, wts["norm_final"][None])
    loss = lax.psum(loss_lanes[0, 0], ("x", "y", "c"))

    gbig = {n: [None] * L for n in ("w_in", "w_uq")}
    gstack = {n: lax.empty(full[n].shape, BF16) for n in BIG if n not in gbig}
    gsmall = {n: [None] * L for n in SMALL if n != "norm_final"}
    oh_dy = jnp.asarray(dy_idx[:, 0, :, None] == np.arange(2 * NA_KH - 1), F32)
    oh_dx = jnp.asarray((dx_idx[0, :, :, None] == np.arange(2 * NA_KW - 1)) & bias_ok[0, :, :, None], F32)
    for l in reversed(range(L)):
        sv = saved[l]
        proj = sv["proj"]
        dh, = _mm(f"d_ff2_{l}", dxb, full["w_ff2"], "nt", [_sds((S, 4 * D), BF16)], lb=l,
                  extras=(sv["h"],), extra_specs=(_tile_spec,), epilogue=dff_epilogue, tk_cap=2048)
        gstack["w_ff2"], = _mm(f"g_ff2_{l}", sv["a"], dxb, "tn", [None], tn_cap=2048, into=(gstack["w_ff2"], l))
        du2, = _mm(f"d_ff1_{l}", dh, full["w_ff1"], "nt", [_sds((S, D), F32)], lb=l, epilogue=store_f32, tk_cap=2048)
        gstack["w_ff1"], = _mm(f"g_ff1_{l}", sv["u2"], dh, "tn", [None], tn_cap=2048, into=(gstack["w_ff1"], l))
        dx2, dx2b, gsmall["norm_mlp"][l] = _rms_bwd(f"rms_mlp_bwd_{l}", sv["x2"], sv["r2"], wts["norm_mlp"][l][None], du2, dx)
        dya, dyb, dga, dgb = _mm(
            f"d_w_out_{l}", dx2b, full["w_out"], "nt", [_sds((S, D), BF16)] * 4, lb=l,
            extras=(proj, proj, sv["y_a"], sv["y_b"]),
            extra_specs=(_col_spec(off_ga), _col_spec(off_gb), _tile_spec, _tile_spec),
            epilogue=dmerge_epilogue, tn_cap=512, tk_cap=2048)
        gstack["w_out"], = _mm(f"g_w_out_{l}", sv["merged"], dx2b, "tn", [None], tn_cap=2048, into=(gstack["w_out"], l))
        do_a, = _mm(f"d_o_mla_{l}", dya, full["w_o_mla"], "nt", [_sds((S, H * HEAD), BF16)], lb=l, tk_cap=2048)
        gstack["w_o_mla"], = _mm(f"g_o_mla_{l}", sv["o_a"], dya, "tn", [None], tn_cap=2048, into=(gstack["w_o_mla"], l))
        do_b, = _mm(f"d_o_na_{l}", dyb, full["w_o_na"], "nt", [_sds((S, NAW), BF16)], lb=l, tk_cap=2048)
        gstack["w_o_na"], = _mm(f"g_o_na_{l}", sv["o_b"], dyb, "tn", [None], tn_cap=2048, into=(gstack["w_o_na"], l))
        dq_na, dk_na, dv_na, dbrow = _na_bwd(l, proj, brow, sv["o_b"], do_b, off_q, off_k, off_v, na_scale)
        tmp = jnp.einsum("hoqn,qnx->honx", dbrow, oh_dx, precision=lax.Precision.HIGHEST)
        gsmall["rpb"][l] = jnp.einsum("honx,ony->hyx", tmp, oh_dy, precision=lax.Precision.HIGHEST)
        dl = _delta(l, sv["o_a"], do_a)
        dq_f, dkv, dkp_h = _mla_bwd(l, sv["q"], sv["kv"], sv["kp"], do_a, sv["lse"], dl)
        dq = _rope_bwd_q(l, dq_f, cos_t, sin_t, mla_scale)
        dnq, = _mm(f"d_q_up_{l}", dq, w_uq_p, "nt", [_sds((S, QL), F32)], lb=l, epilogue=store_f32, tk_cap=2048)
        g_uq, = _mm(f"g_q_up_{l}", sv["nq"], dq, "tn", [_sds((QL, H * 256), BF16)], tn_cap=2048)
        gbig["w_uq"][l] = g_uq.reshape(QL, H, 256)[:, :, :HEAD + ROPE].reshape(QL, H * (HEAD + ROPE))
        dnkv, = _mm(f"d_kv_up_{l}", dkv, full["w_ukv"], "nt", [_sds((S, KL), F32)], lb=l, epilogue=store_f32, tk_cap=2048)
        gstack["w_ukv"], = _mm(f"g_kv_up_{l}", sv["nkv"], dkv, "tn", [None], tn_cap=2048, into=(gstack["w_ukv"], l))
        dcq, dckv, dkpe, gsmall["norm_qa"][l], gsmall["norm_kva"][l] = _lat_bwd(
            l, proj, sv["rq"], sv["rkv"], wts["norm_qa"][l][None], wts["norm_kva"][l][None], dnq, dnkv, dkp_h,
            cos_t, sin_t, off_cq, off_ckv)
        dproj = jnp.concatenate([dga, dgb, dq_na, dk_na.astype(BF16), dv_na.astype(BF16), dcq, dckv, dkpe], axis=1)
        du, = _mm(f"d_proj_{l}", dproj, w_in_p, "nt", [_sds((S, D), F32)], lb=l, epilogue=store_f32, tk_cap=2048)
        g_in, = _mm(f"g_proj_{l}", sv["u"], dproj, "tn", [_sds((D, PW), BF16)], tn_cap=2048)
        back = [None] * 8
        for pos_new, i in enumerate(order):
            back[i] = g_in[:, new_off[pos_new]:new_off[pos_new] + widths[i]]
        g_orig = jnp.concatenate(back, axis=1)
        gbig["w_in"][l] = [g_orig[:, k * nloc:(k + 1) * nloc] for k in range(N_CHIPS)]
        dx, dxb, gsmall["norm_mix"][l] = _rms_bwd(f"rms_mix_bwd_{l}", sv["x"], sv["r1"], wts["norm_mix"][l][None], du, dx2)
    grad_x = dx.reshape(1, S, D)

    gstack["w_in"] = jnp.stack([jnp.stack([gbig["w_in"][l][k] for l in range(L)]) for k in range(N_CHIPS)])
    gstack["w_uq"] = jnp.stack(gbig["w_uq"])
    own, got = _grad_scatter([gstack[n] for n in BIG], kinds, shard_shapes)
    two = lambda t: t.reshape(t.shape[:-3] + (t.shape[-3] * t.shape[-2], t.shape[-1]))
    parts = [_grad_partial(f"grad_partial_{n}", two(o), two(g)) for n, o, g in zip(BIG, own, got)]
    others = _sibling_swap(parts)
    part_w, other_w = dict(zip(BIG, parts)), dict(zip(BIG, others))

    small_shapes = [wts[n].shape for n in SMALL]
    small_g = [jnp.stack([g.reshape(wts[n].shape[1:]) for g in gsmall[n]]) for n in SMALL if n != "norm_final"]
    small_g.append(dg_final.reshape(D))
    gsum = _small_allreduce(_pack(small_g))
    zeros = jnp.zeros_like(gsum)
    sg, sd, sm, svv = _adamw("adamw_small", gsum, zeros, _pack([wts[n] for n in SMALL]), _pack([mom[n] for n in SMALL]),
                             _pack([var[n] for n in SMALL]))
    res = {n: {} for n in WEIGHTS}
    for key, flat in (("g", sg), ("d", sd), ("m", sm), ("v", svv)):
        for n, arr in zip(SMALL, _unpack(flat, small_shapes)):
            res[n][key] = arr
    for n in BIG:
        shp = wts[n].shape
        outs = _adamw(f"adamw_{n}", part_w[n], other_w[n], two(wts[n]), two(mom[n]), two(var[n]))
        for key, arr in zip(("g", "d", "m", "v"), outs):
            res[n][key] = arr.reshape(shp)

    return (loss, grad_x, *[res[n]["g"] for n in WEIGHTS], *[res[n]["d"] for n in WEIGHTS],
            *[res[n]["m"] for n in WEIGHTS], *[res[n]["v"] for n in WEIGHTS])
```

```python
import functools

import numpy as np
import jax
import jax.numpy as jnp
from jax import lax
from jax.experimental import pallas as pl
from jax.experimental.pallas import tpu as pltpu

F32 = jnp.float32
BF16 = jnp.bfloat16
MESH = pl.DeviceIdType.MESH

EPS = 1e-6
ROPE_THETA = 10000.0
ROPE = 64
HEAD = 128
GRID_W = 64
NA_KH = 8
NA_KW = 16
N_CHIPS = 4
N_DEV = 8
NEG = -1e30
LOG2E = 1.4426950408889634

ADAM_LR = 0.001
ADAM_B1 = 0.9
ADAM_B2 = 0.999
ADAM_EPS = 1e-08
ADAM_WD = 0.01
ADAM_STEP = 10

VMEM_LIMIT = 56 * 1024 * 1024

BIG = ("w_in", "w_uq", "w_ukv", "w_o_mla", "w_o_na", "w_out", "w_ff1", "w_ff2")
ROW_SHARDED = ("w_out", "w_ff2")
SMALL = ("norm_mix", "norm_qa", "norm_kva", "rpb", "norm_mlp", "norm_final")
WEIGHTS = ("norm_mix", "w_in", "norm_qa", "w_uq", "norm_kva", "w_ukv", "rpb", "w_o_mla", "w_o_na",
           "w_out", "norm_mlp", "w_ff1", "w_ff2", "norm_final")


def _cparams(sem, **kw):
    return pltpu.CompilerParams(dimension_semantics=sem, vmem_limit_bytes=VMEM_LIMIT, **kw)


def _tile(n, cap, unit=128):
    if n <= cap:
        return n
    best = None
    for t in range(unit, cap + 1, unit):
        if n % t == 0:
            best = t
    assert best is not None, (n, cap, unit)
    return best


def _sds(shape, dtype):
    return jax.ShapeDtypeStruct(shape, dtype)


_DIMS = {"nn": (((1,), (0,)), ((), ())), "nt": (((1,), (1,)), ((), ())), "tn": (((0,), (0,)), ((), ()))}


def _store_cast(acc, extra, outs):
    outs[0][...] = acc.astype(outs[0].dtype)


def _mm(name, a, b, mode, out_shapes, *, lb=None, extras=(), extra_specs=(), out_specs=None,
        epilogue=_store_cast, tm_cap=1024, tn_cap=1024, tk_cap=1024):
    bshape = b.shape[1:] if lb is not None else b.shape
    if mode == "nn":
        (M, K), (K2, N) = a.shape, bshape
    elif mode == "nt":
        (M, K), (N, K2) = a.shape, bshape
    else:
        (K, M), (K2, N) = a.shape, bshape
    assert K == K2, (name, a.shape, b.shape)
    tm, tn, tk = _tile(M, tm_cap), _tile(N, tn_cap), _tile(K, tk_cap)
    nk = K // tk
    if mode == "tn":
        a_spec = pl.BlockSpec((tk, tm), lambda i, j, k: (k, i))
    else:
        a_spec = pl.BlockSpec((tm, tk), lambda i, j, k: (i, k))
    bblk, bidx = ((tn, tk), lambda i, j, k: (j, k)) if mode == "nt" else ((tk, tn), lambda i, j, k: (k, j))
    if lb is not None:
        b_spec = pl.BlockSpec((None,) + bblk, lambda i, j, k: (lb,) + bidx(i, j, k))
    else:
        b_spec = pl.BlockSpec(bblk, bidx)
    ne, no = len(extras), len(out_shapes)
    if out_specs is None:
        out_specs = [lambda tm, tn: pl.BlockSpec((tm, tn), lambda i, j, k: (i, j))] * no
    dims = _DIMS[mode]

    def body(*refs):
        a_ref, b_ref = refs[0], refs[1]
        extra, outs, acc = refs[2:2 + ne], refs[2 + ne:2 + ne + no], refs[-1]
        k = pl.program_id(2)

        @pl.when(k == 0)
        def _():
            acc[...] = jnp.zeros_like(acc)

        acc[...] += lax.dot_general(a_ref[...], b_ref[...], dims, preferred_element_type=F32)

        @pl.when(k == nk - 1)
        def _():
            epilogue(acc[...], extra, outs)

    return pl.pallas_call(
        body, name=name, grid=(M // tm, N // tn, nk),
        in_specs=[a_spec, b_spec] + [s(tm, tn) for s in extra_specs],
        out_specs=[s(tm, tn) for s in out_specs],
        out_shape=list(out_shapes),
        scratch_shapes=[pltpu.VMEM((tm, tn), F32)],
        compiler_params=_cparams(("parallel", "parallel", "arbitrary")),
    )(a, b, *extras)


def _tile_spec(tm, tn):
    return pl.BlockSpec((tm, tn), lambda i, j, k: (i, j))


def _row_spec(width):
    return lambda tm, tn: pl.BlockSpec((tm, width), lambda i, j, k: (i, 0))


def _col_spec(off_cols):
    def make(tm, tn):
        assert off_cols % tn == 0, (off_cols, tn)
        return pl.BlockSpec((tm, tn), lambda i, j, k: (i, off_cols // tn + j))
    return make


def _rms_fwd(name, x, g):
    S, D = x.shape
    tm = _tile(S, 256, 8)

    def body(x_ref, g_ref, u_ref, r_ref):
        xv = x_ref[...]
        r = lax.rsqrt(jnp.mean(xv * xv, axis=-1, keepdims=True) + EPS)
        u_ref[...] = (xv * r * g_ref[...]).astype(BF16)
        r_ref[...] = r

    return pl.pallas_call(
        body, name=name, grid=(S // tm,),
        in_specs=[pl.BlockSpec((tm, D), lambda i: (i, 0)), pl.BlockSpec((1, D), lambda i: (0, 0))],
        out_specs=[pl.BlockSpec((tm, D), lambda i: (i, 0)), pl.BlockSpec((tm, 1), lambda i: (i, 0))],
        out_shape=[_sds((S, D), BF16), _sds((S, 1), F32)],
        compiler_params=_cparams(("parallel",)),
    )(x, g)


def _rms_bwd(name, x, r, g, du, dres):
    S, D = x.shape
    tm = _tile(S, 256, 8)

    def body(x_ref, r_ref, g_ref, du_ref, dres_ref, dx_ref, dxb_ref, dg_ref):
        rv = r_ref[...]
        xhat = x_ref[...] * rv
        duv = du_ref[...].astype(F32)
        dxh = duv * g_ref[...]
        m = jnp.mean(dxh * xhat, axis=-1, keepdims=True)
        dx = dres_ref[...] + rv * (dxh - xhat * m)
        dx_ref[...] = dx
        dxb_ref[...] = dx.astype(BF16)

        @pl.when(pl.program_id(0) == 0)
        def _():
            dg_ref[...] = jnp.zeros_like(dg_ref)

        dg_ref[...] += jnp.sum(duv * xhat, axis=0, keepdims=True)

    row = pl.BlockSpec((tm, D), lambda i: (i, 0))
    vec = pl.BlockSpec((1, D), lambda i: (0, 0))
    return pl.pallas_call(
        body, name=name, grid=(S // tm,),
        in_specs=[row, pl.BlockSpec((tm, 1), lambda i: (i, 0)), vec, row, row],
        out_specs=[row, row, vec],
        out_shape=[_sds((S, D), F32), _sds((S, D), BF16), _sds((1, D), F32)],
        compiler_params=_cparams(("arbitrary",)),
    )(x, r, g, du, dres)


def _final_loss(x, t, g):
    S, D = x.shape
    tm = _tile(S, 256, 8)

    def body(x_ref, t_ref, g_ref, loss_ref, dx_ref, dxb_ref, dg_ref):
        xv = x_ref[...]
        gv = g_ref[...]
        rv = lax.rsqrt(jnp.mean(xv * xv, axis=-1, keepdims=True) + EPS)
        xhat = xv * rv
        diff = xhat * gv - t_ref[...]
        dy = diff * (1.0 / D)
        dxh = dy * gv
        m = jnp.mean(dxh * xhat, axis=-1, keepdims=True)
        dx = rv * (dxh - xhat * m)
        dx_ref[...] = dx
        dxb_ref[...] = dx.astype(BF16)

        @pl.when(pl.program_id(0) == 0)
        def _():
            dg_ref[...] = jnp.zeros_like(dg_ref)
            loss_ref[...] = jnp.zeros_like(loss_ref)

        dg_ref[...] += jnp.sum(dy * xhat, axis=0, keepdims=True)
        per_row = jnp.mean(diff * diff, axis=-1, keepdims=True)
        loss_ref[...] += 0.5 * jnp.sum(per_row, axis=0, keepdims=True)

    row = pl.BlockSpec((tm, D), lambda i: (i, 0))
    vec = pl.BlockSpec((1, D), lambda i: (0, 0))
    return pl.pallas_call(
        body, name="final_loss", grid=(S // tm,),
        in_specs=[row, row, vec],
        out_specs=[pl.BlockSpec((1, 128), lambda i: (0, 0)), row, row, vec],
        out_shape=[_sds((1, 128), F32), _sds((S, D), F32), _sds((S, D), BF16), _sds((1, D), F32)],
        compiler_params=_cparams(("arbitrary",)),
    )(x, t, g)


def _rope128(v, cos_t, sin_t):
    lane = lax.broadcasted_iota(jnp.int32, v.shape, 1)
    up = pltpu.roll(v, 128 - ROPE // 2, 1)
    dn = pltpu.roll(v, ROPE // 2, 1)
    return v * cos_t + jnp.where(lane < ROPE // 2, up, dn) * sin_t


def _lat_fwd(l, proj, g_qa, g_kva, cos_t, sin_t, off_cq, off_ckv, off_kpe):
    S = proj.shape[0]
    QL, KL = g_qa.shape[1], g_kva.shape[1]
    tm = _tile(S, 512, 8)
    assert off_cq % QL == 0 and off_ckv % KL == 0 and off_kpe % 128 == 0

    def body(cq_ref, ckv_ref, kpe_ref, gq_ref, gkv_ref, cos_ref, sin_ref, nq_ref, nkv_ref, rq_ref, rkv_ref, kp_ref):
        for c_ref, g_ref, n_ref, r_ref in ((cq_ref, gq_ref, nq_ref, rq_ref), (ckv_ref, gkv_ref, nkv_ref, rkv_ref)):
            cv = c_ref[...].astype(F32)
            r = lax.rsqrt(jnp.mean(cv * cv, axis=-1, keepdims=True) + EPS)
            n_ref[...] = (cv * r * g_ref[...]).astype(BF16)
            r_ref[...] = r
        kp_ref[...] = _rope128(kpe_ref[...].astype(F32), cos_ref[...], sin_ref[...]).astype(BF16)

    col = lambda w, off: pl.BlockSpec((tm, w), lambda i: (i, off // w))
    row = lambda w: pl.BlockSpec((tm, w), lambda i: (i, 0))
    vec = lambda w: pl.BlockSpec((1, w), lambda i: (0, 0))
    return pl.pallas_call(
        body, name=f"lat_fwd_{l}", grid=(S // tm,),
        in_specs=[col(QL, off_cq), col(KL, off_ckv), col(128, off_kpe), vec(QL), vec(KL), row(128), row(128)],
        out_specs=[row(QL), row(KL), row(1), row(1), row(128)],
        out_shape=[_sds((S, QL), BF16), _sds((S, KL), BF16), _sds((S, 1), F32), _sds((S, 1), F32), _sds((S, 128), BF16)],
        compiler_params=_cparams(("parallel",)),
    )(proj, proj, proj, g_qa, g_kva, cos_t, sin_t)


def _lat_bwd(l, proj, rq, rkv, g_qa, g_kva, dnq, dnkv, dkp_h, cos_t, sin_t, off_cq, off_ckv):
    S = proj.shape[0]
    QL, KL = g_qa.shape[1], g_kva.shape[1]
    H = dkp_h.shape[0]
    tm = _tile(S, 512, 8)

    def body(cq_ref, ckv_ref, rq_ref, rkv_ref, gq_ref, gkv_ref, dnq_ref, dnkv_ref, dkp_ref, cos_ref, sin_ref,
             dcq_ref, dckv_ref, dkpe_ref, dgq_ref, dgkv_ref):
        first = pl.program_id(0) == 0
        for c_ref, r_ref, g_ref, dn_ref, dc_ref, dg_ref in (
                (cq_ref, rq_ref, gq_ref, dnq_ref, dcq_ref, dgq_ref),
                (ckv_ref, rkv_ref, gkv_ref, dnkv_ref, dckv_ref, dgkv_ref)):
            rv = r_ref[...]
            xhat = c_ref[...].astype(F32) * rv
            dn = dn_ref[...]
            dxh = dn * g_ref[...]
            m = jnp.mean(dxh * xhat, axis=-1, keepdims=True)
            dc_ref[...] = (rv * (dxh - xhat * m)).astype(BF16)

            @pl.when(first)
            def _():
                dg_ref[...] = jnp.zeros_like(dg_ref)

            dg_ref[...] += jnp.sum(dn * xhat, axis=0, keepdims=True)
        dkp = dkp_ref[0]
        for h in range(1, H):
            dkp = dkp + dkp_ref[h]
        dkpe_ref[...] = _rope128(dkp, cos_ref[...], -sin_ref[...]).astype(BF16)

    col = lambda w, off: pl.BlockSpec((tm, w), lambda i: (i, off // w))
    row = lambda w: pl.BlockSpec((tm, w), lambda i: (i, 0))
    vec = lambda w: pl.BlockSpec((1, w), lambda i: (0, 0))
    return pl.pallas_call(
        body, name=f"lat_bwd_{l}", grid=(S // tm,),
        in_specs=[col(QL, off_cq), col(KL, off_ckv), row(1), row(1), vec(QL), vec(KL), row(QL), row(KL),
                  pl.BlockSpec((H, tm, 128), lambda i: (0, i, 0)), row(128), row(128)],
        out_specs=[row(QL), row(KL), row(128), vec(QL), vec(KL)],
        out_shape=[_sds((S, QL), BF16), _sds((S, KL), BF16), _sds((S, 128), BF16), _sds((1, QL), F32), _sds((1, KL), F32)],
        compiler_params=_cparams(("arbitrary",)),
    )(proj, proj, rq, rkv, g_qa, g_kva, dnq, dnkv, dkp_h, cos_t, sin_t)


def _rope_bwd_q(l, dq, cos_t, sin_t, scale):
    S, W = dq.shape
    tm = _tile(S, 256, 8)
    nh = W // 256

    def body(dq_ref, cos_ref, sin_ref, out_ref):
        cv, sv = cos_ref[...] * scale, -sin_ref[...] * scale
        for h in range(nh):
            out_ref[:, h * 256:h * 256 + 128] = (dq_ref[:, h * 256:h * 256 + 128] * scale).astype(BF16)
            out_ref[:, h * 256 + 128:(h + 1) * 256] = _rope128(dq_ref[:, h * 256 + 128:(h + 1) * 256], cv, sv).astype(BF16)

    return pl.pallas_call(
        body, name=f"rope_bwd_q_{l}", grid=(S // tm,),
        in_specs=[pl.BlockSpec((tm, W), lambda i: (i, 0)), pl.BlockSpec((tm, 128), lambda i: (i, 0)),
                  pl.BlockSpec((tm, 128), lambda i: (i, 0))],
        out_specs=pl.BlockSpec((tm, W), lambda i: (i, 0)),
        out_shape=_sds((S, W), BF16),
        compiler_params=_cparams(("parallel",)),
    )(dq, cos_t, sin_t)


def _delta(l, o, do):
    S, W = o.shape
    H = W // HEAD
    tm = _tile(S, 1024, 8)

    def body(o_ref, do_ref, d_ref):
        d_ref[...] = jnp.sum(o_ref[...].astype(F32) * do_ref[...].astype(F32), axis=-1, keepdims=True)

    blk = pl.BlockSpec((tm, HEAD), lambda h, i: (i, h))
    return pl.pallas_call(
        body, name=f"delta_{l}", grid=(H, S // tm),
        in_specs=[blk, blk],
        out_specs=pl.BlockSpec((None, tm, 1), lambda h, i: (h, i, 0)),
        out_shape=_sds((H, S, 1), F32),
        compiler_params=_cparams(("parallel", "parallel")),
    )(o, do)


_NT = (((1,), (1,)), ((), ()))
_TN = (((0,), (0,)), ((), ()))


MLA_SUB = 256


def _carry(comm, body, n_in, n_out, n_scratch, steps):
    if comm is None:
        return body, [], [], [], []
    ni, no = len(comm.inputs), len(comm.out_shapes)

    def carrying(*refs):
        a = n_in + ni
        b = a + n_out + no
        ins, cin = refs[:n_in], refs[n_in:a]
        outs, cout = refs[a:a + n_out], refs[a + n_out:b]
        scratch, csem = refs[b:b + n_scratch], refs[b + n_scratch:]
        first, middle, last = steps()

        @pl.when(first)
        def _():
            comm.start(cin, cout, csem)

        @pl.when(middle)
        def _():
            comm.mid(cin, cout, csem)

        body(*ins, *outs, *scratch)

        @pl.when(last)
        def _():
            comm.finish(cin, cout, csem)

    hbm = pl.BlockSpec(memory_space=pl.ANY)
    return carrying, [hbm] * ni, [hbm] * no, comm.out_shapes, comm.scratch


def _mla_fwd(l, q, kv, kp, comm=None):
    S = q.shape[0]
    H = q.shape[1] // 256
    tq, tk = _tile(S, 1024, 8), _tile(S, 512, 128)
    sub = min(MLA_SUB, tq)
    nq, nk = S // tq, S // tk
    assert H >= 2

    def steps():
        h, i, k = pl.program_id(0), pl.program_id(1), pl.program_id(2)
        origin = (i == 0) & (k == 0)
        return (h == 0) & origin, (h == max(1, 5 * H // 8)) & origin, (h == H - 1) & (i == nq - 1) & (k == nk - 1)

    def body(q_ref, kn_ref, v_ref, kp_ref, o_ref, lse_ref, m_sc, acc_sc):
        ki = pl.program_id(2)

        @pl.when(ki == 0)
        def _():
            m_sc[...] = jnp.full_like(m_sc, NEG)
            acc_sc[...] = jnp.zeros_like(acc_sc)

        kc = jnp.concatenate([kn_ref[...], kp_ref[...]], axis=1)
        vx = jnp.concatenate([v_ref[...], jnp.ones((tk, 128), BF16)], axis=1)
        for r in range(tq // sub):
            rows = slice(r * sub, (r + 1) * sub)
            s = lax.dot_general(q_ref[rows, :], kc, _NT, preferred_element_type=F32)
            m_prev = m_sc[rows, :]
            m_new = jnp.maximum(m_prev, jnp.max(s, axis=-1, keepdims=True))
            alpha = jnp.exp(m_prev - m_new)
            p = jnp.exp(s - jnp.tile(m_new, (1, tk // 128)))
            acc_sc[rows, :] = (jnp.tile(alpha, (1, 2)) * acc_sc[rows, :]
                               + jnp.dot(p.astype(BF16), vx, preferred_element_type=F32))
            m_sc[rows, :] = m_new

        @pl.when(ki == nk - 1)
        def _():
            l = acc_sc[:, HEAD:]
            o_ref[...] = (acc_sc[:, :HEAD] / l).astype(BF16)
            lse_ref[...] = m_sc[:, :1] + jnp.log(l[:, :1])

    body, cin_specs, cout_specs, cout_shapes, cscratch = _carry(comm, body, 4, 2, 2, steps)
    return pl.pallas_call(
        body, name=f"mla_fwd_{l}", grid=(H, nq, nk),
        in_specs=[pl.BlockSpec((tq, 256), lambda h, i, k: (i, h)),
                  pl.BlockSpec((tk, HEAD), lambda h, i, k: (k, 2 * h)),
                  pl.BlockSpec((tk, HEAD), lambda h, i, k: (k, 2 * h + 1)),
                  pl.BlockSpec((tk, 128), lambda h, i, k: (k, 0))] + cin_specs,
        out_specs=[pl.BlockSpec((tq, HEAD), lambda h, i, k: (i, h)),
                   pl.BlockSpec((None, tq, 1), lambda h, i, k: (h, i, 0))] + cout_specs,
        out_shape=[_sds((S, H * HEAD), BF16), _sds((H, S, 1), F32)] + cout_shapes,
        scratch_shapes=[pltpu.VMEM((tq, 128), F32), pltpu.VMEM((tq, 2 * HEAD), F32)] + cscratch,
        compiler_params=_cparams(("arbitrary", "arbitrary", "arbitrary")),
    )(q, kv, kv, kp, *(comm.inputs if comm else ()))


def _mla_bwd(l, q, kv, kp, do, lse, delta, comm=None):
    S = q.shape[0]
    H = q.shape[1] // 256
    tq, tk = _tile(S, 1024, 8), _tile(S, 512, 128)
    sub = min(MLA_SUB, tq)
    nq, nk = S // tq, S // tk

    def steps():
        h, k, i = pl.program_id(0), pl.program_id(1), pl.program_id(2)
        origin = (k == 0) & (i == 0)
        return (h == 0) & origin, (h == H // 2) & origin, (h == H - 1) & (k == nk - 1) & (i == nq - 1)

    def body(q_ref, kn_ref, v_ref, kp_ref, do_ref, lse_ref, dl_ref, dq_ref, dkv_ref, dkp_ref, dkc_sc, dv_sc):
        ki, qi = pl.program_id(1), pl.program_id(2)
        kc = jnp.concatenate([kn_ref[...], kp_ref[...]], axis=1)
        vv = v_ref[...]
        dkc, dv, dq_tiles = None, None, []
        for r in range(tq // sub):
            rows = slice(r * sub, (r + 1) * sub)
            qv, dov = q_ref[rows, :], do_ref[rows, :]
            s = lax.dot_general(qv, kc, _NT, preferred_element_type=F32)
            p = jnp.exp(s - lse_ref[rows, :])
            dv_r = lax.dot_general(p.astype(BF16), dov, _TN, preferred_element_type=F32)
            dp = lax.dot_general(dov, vv, _NT, preferred_element_type=F32)
            ds = (p * (dp - dl_ref[rows, :])).astype(BF16)
            dkc_r = lax.dot_general(ds, qv, _TN, preferred_element_type=F32)
            dq_tiles.append(jnp.dot(ds, kc, preferred_element_type=F32))
            dkc = dkc_r if dkc is None else dkc + dkc_r
            dv = dv_r if dv is None else dv + dv_r
        dq_tile = jnp.concatenate(dq_tiles, axis=0) if len(dq_tiles) > 1 else dq_tiles[0]

        @pl.when(qi == 0)
        def _():
            dkc_sc[...] = dkc
            dv_sc[...] = dv

        @pl.when(qi > 0)
        def _():
            dkc_sc[...] += dkc
            dv_sc[...] += dv

        rows = pl.ds(pl.multiple_of(qi * tq, tq), tq)

        @pl.when(ki == 0)
        def _():
            dq_ref[rows, :] = dq_tile

        @pl.when(ki > 0)
        def _():
            dq_ref[rows, :] += dq_tile

        @pl.when(qi == nq - 1)
        def _():
            dkv_ref[:, :HEAD] = dkc_sc[:, :HEAD].astype(BF16)
            dkv_ref[:, HEAD:] = dv_sc[...].astype(BF16)
            dkp_ref[...] = dkc_sc[:, HEAD:]

    body, cin_specs, cout_specs, cout_shapes, cscratch = _carry(comm, body, 7, 3, 2, steps)
    return pl.pallas_call(
        body, name=f"mla_bwd_{l}", grid=(H, nk, nq),
        in_specs=[pl.BlockSpec((tq, 256), lambda h, k, i: (i, h)),
                  pl.BlockSpec((tk, HEAD), lambda h, k, i: (k, 2 * h)),
                  pl.BlockSpec((tk, HEAD), lambda h, k, i: (k, 2 * h + 1)),
                  pl.BlockSpec((tk, 128), lambda h, k, i: (k, 0)),
                  pl.BlockSpec((tq, HEAD), lambda h, k, i: (i, h)),
                  pl.BlockSpec((None, tq, 1), lambda h, k, i: (h, i, 0)),
                  pl.BlockSpec((None, tq, 1), lambda h, k, i: (h, i, 0))] + cin_specs,
        out_specs=[pl.BlockSpec((S, 256), lambda h, k, i: (0, h)),
                   pl.BlockSpec((tk, 256), lambda h, k, i: (k, h)),
                   pl.BlockSpec((None, tk, 128), lambda h, k, i: (h, k, 0))] + cout_specs,
        out_shape=[_sds((S, H * 256), F32), _sds((S, H * 256), BF16), _sds((H, S, 128), F32)] + cout_shapes,
        scratch_shapes=[pltpu.VMEM((tk, 256), F32), pltpu.VMEM((tk, HEAD), F32)] + cscratch,
        compiler_params=_cparams(("arbitrary", "arbitrary", "arbitrary")),
    )(q, kv, kv, kp, do, lse, delta, *(comm.inputs if comm else ()))


NA_RB = 8


def _na_bias_index(rows):
    j = np.arange(NA_KH)
    dy = j[None, :] - (np.arange(8)[:, None] - 4) + 3
    c = np.arange(GRID_W)
    col_start = np.clip(c - NA_KW // 2, 0, GRID_W - NA_KW)
    ok = (c[None, :] >= col_start[:, None]) & (c[None, :] < col_start[:, None] + NA_KW)
    dx = np.clip(c[None, :] - c[:, None], -(NA_KW - 1), NA_KW - 1) + (NA_KW - 1)
    dy_full = np.broadcast_to(dy[:, None, :, None], (8, GRID_W, NA_KH, GRID_W)).reshape(8, GRID_W, NA_KH * GRID_W)
    dx_full = np.broadcast_to(dx[None, :, None, :], (8, GRID_W, NA_KH, GRID_W)).reshape(8, GRID_W, NA_KH * GRID_W)
    ok_full = np.broadcast_to(ok[None, :, None, :], (8, GRID_W, NA_KH, GRID_W)).reshape(8, GRID_W, NA_KH * GRID_W)
    valid = ok_full & (dy_full >= 0) & (dy_full <= 2 * NA_KH - 2)
    return np.clip(dy_full, 0, 2 * NA_KH - 2), dx_full, valid


def _na_bias(rpb, dx_masked):
    L, H, NY, NX = rpb.shape
    nkeys = NA_KH * GRID_W

    def body(rpb_ref, dx_ref, out_ref):
        base = (pl.program_id(0) * H + pl.program_id(1)) * (NY * NX)
        dxv = dx_ref[...]
        key_row = lax.shift_right_logical(lax.broadcasted_iota(jnp.int32, (1, nkeys), 1), 6)

        def variant(o, carry):
            acc = jnp.full((GRID_W, nkeys), NEG, F32)
            for xx in range(NX):
                row = jnp.zeros((1, nkeys), F32)
                for j in range(NA_KH):
                    row = jnp.where(key_row == j, rpb_ref[base + (j - o + NA_KH - 1) * NX + xx], row)
                acc = jnp.where(dxv == xx, row, acc)
            out_ref[o] = acc
            return carry

        lax.fori_loop(0, 8, variant, 0)

    return pl.pallas_call(
        body, name="na_bias", grid=(L, H),
        in_specs=[pl.BlockSpec(memory_space=pltpu.SMEM), pl.BlockSpec((GRID_W, nkeys), lambda l, h: (0, 0))],
        out_specs=pl.BlockSpec((None, None, 8, GRID_W, nkeys), lambda l, h: (l, h, 0, 0, 0)),
        out_shape=_sds((L, H, 8, GRID_W, nkeys), F32),
        compiler_params=_cparams(("parallel", "parallel")),
    )(rpb.reshape(-1), dx_masked)


def _na_row_window(rb, i, rows):
    r = rb * NA_RB + i
    ks = jnp.clip(r - NA_KH // 2, 0, rows - NA_KH)
    variant = r - ks
    return pl.ds(pl.multiple_of(ks * GRID_W, GRID_W), NA_KH * GRID_W), variant


def _na_fwd(l, proj, brow, off_q, off_k, off_v, scale):
    S = proj.shape[0]
    H = brow.shape[1]
    rows = S // GRID_W
    assert rows % NA_RB == 0 and rows >= NA_KH
    tq = NA_RB * GRID_W

    def body(q_ref, k_ref, v_ref, b_ref, o_ref):
        rb = pl.program_id(1)
        for i in range(NA_RB):
            win, variant = _na_row_window(rb, i, rows)
            qs = slice(i * GRID_W, (i + 1) * GRID_W)
            s = lax.dot_general(q_ref[qs, :], k_ref[win, :], _NT, preferred_element_type=F32) * scale + b_ref[variant]
            e = jnp.exp(s - jnp.max(s, axis=-1, keepdims=True))
            p = e / jnp.sum(e, axis=-1, keepdims=True)
            o_ref[qs, :] = jnp.dot(p.astype(BF16), v_ref[win, :], preferred_element_type=F32).astype(BF16)

    return pl.pallas_call(
        body, name=f"na_fwd_{l}", grid=(H, rows // NA_RB),
        in_specs=[pl.BlockSpec((tq, HEAD), lambda h, r: (r, off_q // HEAD + h)),
                  pl.BlockSpec((S, HEAD), lambda h, r: (0, off_k // HEAD + h)),
                  pl.BlockSpec((S, HEAD), lambda h, r: (0, off_v // HEAD + h)),
                  pl.BlockSpec((None, None, 8, GRID_W, NA_KH * GRID_W), lambda h, r: (l, h, 0, 0, 0))],
        out_specs=pl.BlockSpec((tq, HEAD), lambda h, r: (r, h)),
        out_shape=_sds((S, H * HEAD), BF16),
        compiler_params=_cparams(("parallel", "arbitrary")),
    )(proj, proj, proj, brow)


def _na_bwd(l, proj, brow, o, do, off_q, off_k, off_v, scale):
    S = proj.shape[0]
    H = brow.shape[1]
    rows = S // GRID_W
    tq = NA_RB * GRID_W

    def body(q_ref, k_ref, v_ref, b_ref, o_ref, do_ref, dq_ref, dk_ref, dv_ref, db_ref):
        rb = pl.program_id(1)

        @pl.when(rb == 0)
        def _():
            dk_ref[...] = jnp.zeros_like(dk_ref)
            dv_ref[...] = jnp.zeros_like(dv_ref)
            db_ref[...] = jnp.zeros_like(db_ref)

        for i in range(NA_RB):
            win, variant = _na_row_window(rb, i, rows)
            qs = slice(i * GRID_W, (i + 1) * GRID_W)
            qv, kw, vw, dov = q_ref[qs, :], k_ref[win, :], v_ref[win, :], do_ref[qs, :]
            s = lax.dot_general(qv, kw, _NT, preferred_element_type=F32) * scale + b_ref[variant]
            e = jnp.exp(s - jnp.max(s, axis=-1, keepdims=True))
            p = e / jnp.sum(e, axis=-1, keepdims=True)
            dv_ref[win, :] += lax.dot_general(p.astype(BF16), dov, _TN, preferred_element_type=F32)
            dp = lax.dot_general(dov, vw, _NT, preferred_element_type=F32)
            dl = jnp.sum(dov.astype(F32) * o_ref[qs, :].astype(F32), axis=-1, keepdims=True)
            ds = p * (dp - dl)
            db_ref[variant] += ds
            dsb = (ds * scale).astype(BF16)
            dq_ref[qs, :] = jnp.dot(dsb, kw, preferred_element_type=F32).astype(BF16)
            dk_ref[win, :] += lax.dot_general(dsb, qv, _TN, preferred_element_type=F32)

    qblk = pl.BlockSpec((tq, HEAD), lambda h, r: (r, h))
    full = pl.BlockSpec((S, HEAD), lambda h, r: (0, h))
    bias = pl.BlockSpec((None, None, 8, GRID_W, NA_KH * GRID_W), lambda h, r: (l, h, 0, 0, 0))
    dbias = pl.BlockSpec((None, 8, GRID_W, NA_KH * GRID_W), lambda h, r: (h, 0, 0, 0))
    return pl.pallas_call(
        body, name=f"na_bwd_{l}", grid=(H, rows // NA_RB),
        in_specs=[pl.BlockSpec((tq, HEAD), lambda h, r: (r, off_q // HEAD + h)),
                  pl.BlockSpec((S, HEAD), lambda h, r: (0, off_k // HEAD + h)),
                  pl.BlockSpec((S, HEAD), lambda h, r: (0, off_v // HEAD + h)),
                  bias, qblk, qblk],
        out_specs=[qblk, full, full, dbias],
        out_shape=[_sds((S, H * HEAD), BF16), _sds((S, H * HEAD), F32), _sds((S, H * HEAD), F32),
                   _sds(brow.shape[1:], F32)],
        compiler_params=_cparams(("parallel", "arbitrary")),
    )(proj, proj, proj, brow, o, do)


def _place():
    return lax.axis_index("x"), lax.axis_index("y"), lax.axis_index("c")


class _Comm:
    def __init__(self, inputs, out_shapes, scratch, start, mid, finish):
        self.inputs, self.out_shapes, self.scratch = list(inputs), list(out_shapes), list(scratch)
        self.start, self.mid, self.finish = start, mid, finish


def _run_comm(name, comm):
    ni, no = len(comm.inputs), len(comm.out_shapes)

    def body(*refs):
        parts = refs[:ni], refs[ni:ni + no], refs[ni + no:]
        comm.start(*parts)
        comm.mid(*parts)
        comm.finish(*parts)

    hbm = pl.BlockSpec(memory_space=pl.ANY)
    return pl.pallas_call(body, name=name, in_specs=[hbm] * ni, out_specs=[hbm] * no, out_shape=comm.out_shapes,
                          scratch_shapes=comm.scratch)(*comm.inputs)


def _full_shape(shard_shape, kind):
    A, B = shard_shape
    return {"col": (A, N_CHIPS * B), "row": (N_CHIPS * A, B), "slot": (N_CHIPS, A, B)}[kind]


def _shard_region(ref, kind, shard_shape, k, half=None):
    A, B = shard_shape
    lo, n = (0, A) if half is None else (pl.multiple_of(half * (A // 2), 16), A // 2)
    if kind == "col":
        return ref.at[pl.ds(lo, n), pl.ds(pl.multiple_of(k * B, 128), B)]
    if kind == "row":
        return ref.at[pl.ds(pl.multiple_of(k * A + lo, 16), n), :]
    return ref.at[k, pl.ds(lo, n), :]


def _gather_comm(shards, kinds):
    n = len(shards)
    shapes = [tuple(s.shape) for s in shards]
    assert all(s[0] % 32 == 0 for s in shapes)

    def copies(w, o, sems):
        send_sems, recv_sems, local_sems = sems
        x, y, c = _place()
        sibling = (x, y, 1 - c)
        chips = [(1 - x, y), (x, 1 - y), (1 - x, 1 - y)]

        def copy(k, src, dst, to):
            return pltpu.make_async_remote_copy(src_ref=src, dst_ref=dst, send_sem=send_sems.at[k],
                                                recv_sem=recv_sems.at[k], device_id=to, device_id_type=MESH)

        def region(i, cx, cy, half=None):
            return _shard_region(o[i], kinds[i], shapes[i], 2 * cx + cy, half)

        def my_half(i):
            A = shapes[i][0]
            return w[i].at[pl.ds(pl.multiple_of(c * (A // 2), 16), A // 2), :]

        pairs = [(i, j, chip) for i in range(n) for j, chip in enumerate(chips)]
        local = [pltpu.make_async_copy(w[i], region(i, x, y), local_sems.at[i]) for i in range(n)]
        first = [copy(6 * i + j, my_half(i), region(i, x, y, c), (*chip, c)) for i, j, chip in pairs]
        landed = [copy(6 * i + j, region(i, *chip, c), region(i, *chip, c), (*chip, c)) for i, j, chip in pairs]
        passed = [copy(6 * i + 3 + j, region(i, *chip, c), region(i, *chip, c), sibling) for i, j, chip in pairs]
        handed = [copy(6 * i + 3 + j, region(i, *chip, 1 - c), region(i, *chip, 1 - c), sibling) for i, j, chip in pairs]
        return local, first, landed, passed, handed

    def start(w, o, sems):
        local, first, _, _, _ = copies(w, o, sems)
        for cp in local + first:
            cp.start()

    def mid(w, o, sems):
        _, _, landed, passed, _ = copies(w, o, sems)
        for arrived, onward in zip(landed, passed):
            arrived.wait_recv()
            onward.start()

    def finish(w, o, sems):
        local, first, _, passed, handed = copies(w, o, sems)
        for cp in handed:
            cp.wait_recv()
        for cp in first + passed:
            cp.wait_send()
        for cp in local:
            cp.wait()

    return _Comm(shards, [_sds(_full_shape(s, k), BF16) for s, k in zip(shapes, kinds)],
                 [pltpu.SemaphoreType.DMA((6 * n,)), pltpu.SemaphoreType.DMA((6 * n,)), pltpu.SemaphoreType.DMA((n,))],
                 start, mid, finish)


def _scatter_comm(grads, kinds, shapes):
    n = len(grads)

    def copies(g, outs, sems):
        send_sems, recv_sems, local_sems = sems
        own, got = outs[0::2], outs[1::2]
        x, y, c = _place()
        chips = [(1 - x, y), (x, 1 - y), (1 - x, 1 - y)]
        local = [pltpu.make_async_copy(_shard_region(g[i], kinds[i], shapes[i], 2 * x + y), own[i], local_sems.at[i])
                 for i in range(n)]
        sends = [pltpu.make_async_remote_copy(
            src_ref=_shard_region(g[i], kinds[i], shapes[i], 2 * cx + cy), dst_ref=got[i].at[j],
            send_sem=send_sems.at[3 * i + j], recv_sem=recv_sems.at[3 * i + j],
            device_id=(cx, cy, c), device_id_type=MESH) for i in range(n) for j, (cx, cy) in enumerate(chips)]
        return local, sends

    def start(g, outs, sems):
        local, sends = copies(g, outs, sems)
        for cp in local + sends:
            cp.start()

    def mid(g, outs, sems):
        pass

    def finish(g, outs, sems):
        local, sends = copies(g, outs, sems)
        for cp in sends:
            cp.wait_recv()
        for cp in sends:
            cp.wait_send()
        for cp in local:
            cp.wait()

    out_shapes = []
    for s in shapes:
        out_shapes += [_sds(tuple(s), BF16), _sds((3,) + tuple(s), BF16)]
    return _Comm(grads, out_shapes,
                 [pltpu.SemaphoreType.DMA((3 * n,)), pltpu.SemaphoreType.DMA((3 * n,)), pltpu.SemaphoreType.DMA((n,))],
                 start, mid, finish)


def _grad_partial(name, own, got, stack, layer):
    A, W = own.shape
    tm = _tile(A, max(16, (1 << 19) // W // 16 * 16), 16)

    def body(own_ref, got_ref, stack_ref, out_ref):
        acc = own_ref[...].astype(F32)
        for j in range(3):
            acc = acc + got_ref[j].astype(F32)
        out_ref[...] = acc

    return pl.pallas_call(
        body, name=name, grid=(A // tm,),
        in_specs=[pl.BlockSpec((tm, W), lambda i: (i, 0)), pl.BlockSpec((3, tm, W), lambda i: (0, i, 0)),
                  pl.BlockSpec(memory_space=pl.ANY)],
        out_specs=pl.BlockSpec((None, tm, W), lambda i: (layer, i, 0)),
        out_shape=_sds(stack.shape, F32),
        input_output_aliases={2: 0},
        compiler_params=_cparams(("parallel",)),
    )(own, got, stack)


def _sibling_swap(parts):
    n = len(parts)

    def body(*refs):
        p, got = refs[:n], refs[n:2 * n]
        send_sems, recv_sems = refs[2 * n:]
        x, y, c = _place()
        copies = [pltpu.make_async_remote_copy(src_ref=p[i], dst_ref=got[i], send_sem=send_sems.at[i],
                                               recv_sem=recv_sems.at[i], device_id=(x, y, 1 - c), device_id_type=MESH)
                  for i in range(n)]
        for cp in copies:
            cp.start()
        for cp in copies:
            cp.wait()

    hbm = pl.BlockSpec(memory_space=pl.ANY)
    return pl.pallas_call(
        body, name="sibling_swap",
        in_specs=[hbm] * n, out_specs=[hbm] * n,
        out_shape=[_sds(p.shape, p.dtype) for p in parts],
        scratch_shapes=[pltpu.SemaphoreType.DMA((n,)), pltpu.SemaphoreType.DMA((n,))],
    )(*parts)


def _small_allreduce(vec):
    NR, W = vec.shape

    def body(v_ref, all_ref, sum_ref, send_sems, recv_sems):
        x, y, c = _place()
        me = 4 * x + 2 * y + c
        all_ref[me] = v_ref[...]
        copies = []
        for k in range(1, N_DEV):
            fx, fy, fc = (k >> 2) & 1, (k >> 1) & 1, k & 1
            peer = (x ^ fx, y ^ fy, c ^ fc)
            copies.append(pltpu.make_async_remote_copy(
                src_ref=v_ref, dst_ref=all_ref.at[me], send_sem=send_sems.at[k - 1], recv_sem=recv_sems.at[k - 1],
                device_id=peer, device_id_type=MESH))
        for cp in copies:
            cp.start()
        for cp in copies:
            cp.wait_recv()
        for cp in copies:
            cp.wait_send()
        acc = all_ref[0]
        for d in range(1, N_DEV):
            acc = acc + all_ref[d]
        sum_ref[...] = acc

    return pl.pallas_call(
        body, name="small_allreduce",
        in_specs=[pl.BlockSpec(memory_space=pltpu.VMEM)],
        out_specs=[pl.BlockSpec(memory_space=pltpu.VMEM), pl.BlockSpec(memory_space=pltpu.VMEM)],
        out_shape=[_sds((N_DEV, NR, W), F32), _sds((NR, W), F32)],
        scratch_shapes=[pltpu.SemaphoreType.DMA((N_DEV - 1,)), pltpu.SemaphoreType.DMA((N_DEV - 1,))],
    )(vec)[1]


def _adamw_math(g, w, m, v):
    m = ADAM_B1 * m + (1.0 - ADAM_B1) * g
    v = ADAM_B2 * v + (1.0 - ADAM_B2) * (g * g)
    m_hat = m / (1.0 - ADAM_B1 ** ADAM_STEP)
    v_hat = v / (1.0 - ADAM_B2 ** ADAM_STEP)
    delta = -ADAM_LR * (m_hat / (jnp.sqrt(v_hat) + ADAM_EPS) + ADAM_WD * w)
    return delta, m, v


def _adamw(name, ga, gb, w, m, v):
    rows, n = w.shape
    tm = _tile(rows, max(8, (1 << 18) // n // 8 * 8), 8)

    def body(ga_ref, gb_ref, w_ref, m_ref, v_ref, g_out, d_out, m_out, v_out):
        g = ga_ref[...] + gb_ref[...]
        delta, mn, vn = _adamw_math(g, w_ref[...], m_ref[...], v_ref[...])
        g_out[...] = g
        d_out[...] = delta
        m_out[...] = mn
        v_out[...] = vn

    blk = pl.BlockSpec((tm, n), lambda i: (i, 0))
    return pl.pallas_call(
        body, name=name, grid=(rows // tm,),
        in_specs=[blk] * 5, out_specs=[blk] * 4, out_shape=[_sds((rows, n), F32)] * 4,
        compiler_params=_cparams(("parallel",)),
    )(ga, gb, w, m, v)


def _pack(parts):
    flat = jnp.concatenate([p.reshape(-1) for p in parts])
    pad = (-flat.shape[0]) % 1024
    if pad:
        flat = jnp.concatenate([flat, jnp.zeros((pad,), flat.dtype)])
    return flat.reshape(-1, 128)


def _unpack(flat, shapes):
    flat = flat.reshape(-1)
    out, off = [], 0
    for s in shapes:
        n = int(np.prod(s))
        out.append(flat[off:off + n].reshape(s))
        off += n
    return out


def kernel(x, norm_mix, w_in, norm_qa, w_uq, norm_kva, w_ukv, rpb, w_o_mla, w_o_na, w_out, norm_mlp, w_ff1, w_ff2, norm_final, loss_target, m_norm_mix, m_w_in, m_norm_qa, m_w_uq, m_norm_kva, m_w_ukv, m_rpb, m_w_o_mla, m_w_o_na, m_w_out, m_norm_mlp, m_w_ff1, m_w_ff2, m_norm_final, v_norm_mix, v_w_in, v_norm_qa, v_w_uq, v_norm_kva, v_w_ukv, v_rpb, v_w_o_mla, v_w_o_na, v_w_out, v_norm_mlp, v_w_ff1, v_w_ff2, v_norm_final):
    wts = dict(norm_mix=norm_mix, w_in=w_in, norm_qa=norm_qa, w_uq=w_uq, norm_kva=norm_kva, w_ukv=w_ukv, rpb=rpb,
               w_o_mla=w_o_mla, w_o_na=w_o_na, w_out=w_out, norm_mlp=norm_mlp, w_ff1=w_ff1, w_ff2=w_ff2,
               norm_final=norm_final)
    mom = dict(norm_mix=m_norm_mix, w_in=m_w_in, norm_qa=m_norm_qa, w_uq=m_w_uq, norm_kva=m_norm_kva, w_ukv=m_w_ukv,
               rpb=m_rpb, w_o_mla=m_w_o_mla, w_o_na=m_w_o_na, w_out=m_w_out, norm_mlp=m_norm_mlp, w_ff1=m_w_ff1,
               w_ff2=m_w_ff2, norm_final=m_norm_final)
    var = dict(norm_mix=v_norm_mix, w_in=v_w_in, norm_qa=v_norm_qa, w_uq=v_w_uq, norm_kva=v_norm_kva, w_ukv=v_w_ukv,
               rpb=v_rpb, w_o_mla=v_w_o_mla, w_o_na=v_w_o_na, w_out=v_w_out, norm_mlp=v_norm_mlp, w_ff1=v_w_ff1,
               w_ff2=v_w_ff2, norm_final=v_norm_final)

    _, S, D = x.shape
    L = w_in.shape[0]
    QL, KL = norm_qa.shape[1], norm_kva.shape[1]
    H = w_uq.shape[2] * N_CHIPS // (HEAD + ROPE)
    NAW = w_o_na.shape[1]
    NH = NAW // HEAD
    rows = S // GRID_W
    x = x.reshape(S, D)
    target = loss_target.reshape(S, D)

    kinds = ["slot" if n == "w_in" else ("row" if n in ROW_SHARDED else "col") for n in BIG]
    shard_shapes = [wts[n].shape[1:] for n in BIG]

    def gather_of(layer):
        return _gather_comm([wts[n][layer].astype(BF16) for n in BIG], kinds)

    widths = (QL, KL, ROPE, NAW, NAW, NAW, D, D)
    starts = np.concatenate([[0], np.cumsum(widths)]).astype(int)
    order = (6, 7, 3, 4, 5, 0, 1, 2)
    nloc = w_in.shape[2]
    new_off = np.concatenate([[0], np.cumsum([widths[i] for i in order])]).astype(int)
    off_ga, off_gb, off_q, off_k, off_v, off_cq, off_ckv, off_kpe = (int(o) for o in new_off[:8])
    PW = int(new_off[-1]) + 128 - ROPE

    def prepared(gathered):
        full = dict(zip(BIG, gathered))
        pieces = []
        for i in order:
            for k in range(N_CHIPS):
                lo, hi = max(int(starts[i]), k * nloc), min(int(starts[i + 1]), (k + 1) * nloc)
                if lo < hi:
                    pieces.append(full["w_in"][k, :, lo - k * nloc:hi - k * nloc])
        full["w_in"] = jnp.concatenate(pieces + [jnp.zeros((D, 128 - ROPE), BF16)], axis=1)
        full["w_uq"] = jnp.pad(full["w_uq"].reshape(QL, H, HEAD + ROPE),
                               ((0, 0), (0, 0), (0, 256 - HEAD - ROPE))).reshape(QL, H * 256)
        return full

    pos = jnp.arange(S, dtype=F32)
    inv_freq = 1.0 / (ROPE_THETA ** (jnp.arange(0, ROPE, 2, dtype=F32) / ROPE))
    ang = pos[:, None] * inv_freq[None, :]
    cos, sin, zero = jnp.cos(ang), jnp.sin(ang), jnp.zeros((S, 128 - ROPE), F32)
    cos_t = jnp.concatenate([cos, cos, zero], axis=1)
    sin_t = jnp.concatenate([-sin, sin, zero], axis=1)

    dy_idx, dx_idx, bias_ok = _na_bias_index(rows)
    brow = _na_bias(wts["rpb"], jnp.asarray(np.where(bias_ok[0], dx_idx[0], -1), jnp.int32))
    mla_scale = float((HEAD + ROPE) ** -0.5)
    na_scale = float(HEAD ** -0.5)

    def rope_q_epilogue(acc, extra, outs):
        cv, sv = extra[0][...] * mla_scale, extra[1][...] * mla_scale
        for hh in range(acc.shape[1] // 256):
            outs[0][:, hh * 256:hh * 256 + 128] = (acc[:, hh * 256:hh * 256 + 128] * mla_scale).astype(BF16)
            outs[0][:, hh * 256 + 128:(hh + 1) * 256] = _rope128(acc[:, hh * 256 + 128:(hh + 1) * 256], cv, sv).astype(BF16)

    def store_f32(acc, extra, outs):
        outs[0][...] = acc

    def merge_epilogue(acc, extra, outs):
        ga, gb, ya = extra[0][...].astype(F32), extra[1][...].astype(F32), extra[2][...]
        outs[0][...] = (jax.nn.sigmoid(ga) * ya + jax.nn.sigmoid(gb) * acc).astype(BF16)
        outs[1][...] = acc

    def residual_epilogue(acc, extra, outs):
        outs[0][...] = extra[0][...] + acc

    def ff1_epilogue(acc, extra, outs):
        outs[0][...] = acc.astype(BF16)
        outs[1][...] = jnp.square(jnp.maximum(acc, 0.0)).astype(BF16)

    def dff_epilogue(acc, extra, outs):
        outs[0][...] = (acc * (2.0 * jnp.maximum(extra[0][...].astype(F32), 0.0))).astype(BF16)

    def dmerge_epilogue(acc, extra, outs):
        ga, gb = extra[0][...].astype(F32), extra[1][...].astype(F32)
        ya, yb = extra[2][...], extra[3][...]
        sa, sb = jax.nn.sigmoid(ga), jax.nn.sigmoid(gb)
        outs[0][...] = (acc * sa).astype(BF16)
        outs[1][...] = (acc * sb).astype(BF16)
        outs[2][...] = (acc * ya * sa * (1.0 - sa)).astype(BF16)
        outs[3][...] = (acc * yb * sb * (1.0 - sb)).astype(BF16)

    saved = []
    gathered = _run_comm("weight_gather_0", gather_of(0))
    for l in range(L):
        w = prepared(gathered)
        u, r1 = _rms_fwd(f"rms_mix_{l}", x, wts["norm_mix"][l][None])
        proj, = _mm(f"proj_{l}", u, w["w_in"], "nn", [_sds((S, PW), BF16)], tn_cap=2048, tk_cap=2048)
        nq, nkv, rq, rkv, kp = _lat_fwd(l, proj, wts["norm_qa"][l][None], wts["norm_kva"][l][None], cos_t, sin_t,
                                        off_cq, off_ckv, off_kpe)
        q, = _mm(f"q_up_{l}", nq, w["w_uq"], "nn", [_sds((S, H * 256), BF16)],
                 extras=(cos_t, sin_t), extra_specs=(_row_spec(128), _row_spec(128)), epilogue=rope_q_epilogue, tn_cap=512)
        kv, = _mm(f"kv_up_{l}", nkv, w["w_ukv"], "nn", [_sds((S, H * 256), BF16)])
        o_a, lse, *gathered = _mla_fwd(l, q, kv, kp, gather_of(l + 1) if l + 1 < L else None)
        o_b = _na_fwd(l, proj, brow, off_q, off_k, off_v, na_scale)
        y_a, = _mm(f"o_mla_{l}", o_a, w["w_o_mla"], "nn", [_sds((S, D), F32)], epilogue=store_f32, tn_cap=512)
        merged, y_b = _mm(f"o_na_merge_{l}", o_b, w["w_o_na"], "nn", [_sds((S, D), BF16), _sds((S, D), F32)],
                          extras=(proj, proj, y_a), extra_specs=(_col_spec(off_ga), _col_spec(off_gb), _tile_spec),
                          epilogue=merge_epilogue, tn_cap=512)
        x2, = _mm(f"w_out_{l}", merged, w["w_out"], "nn", [_sds((S, D), F32)],
                  extras=(x,), extra_specs=(_tile_spec,), epilogue=residual_epilogue, tk_cap=2048)
        u2, r2 = _rms_fwd(f"rms_mlp_{l}", x2, wts["norm_mlp"][l][None])
        h, a = _mm(f"ff1_{l}", u2, w["w_ff1"], "nn", [_sds((S, 4 * D), BF16), _sds((S, 4 * D), BF16)],
                   epilogue=ff1_epilogue, tk_cap=2048)
        x3, = _mm(f"ff2_{l}", a, w["w_ff2"], "nn", [_sds((S, D), F32)],
                  extras=(x2,), extra_specs=(_tile_spec,), epilogue=residual_epilogue, tk_cap=2048)
        saved.append(dict(w=w, x=x, r1=r1, u=u, proj=proj, nq=nq, nkv=nkv, rq=rq, rkv=rkv, q=q, kv=kv, kp=kp, o_a=o_a,
                          lse=lse, o_b=o_b, y_a=y_a, y_b=y_b, merged=merged, x2=x2, r2=r2, u2=u2, h=h, a=a))
        x = x3

    loss_lanes, dx, dxb, dg_final = _final_loss(x, target, wts["norm_final"][None])
    loss = lax.psum(loss_lanes[0, 0], ("x", "y", "c"))

    gsmall = {n: [None] * L for n in SMALL if n != "norm_final"}
    oh_dy = jnp.asarray(dy_idx[:, 0, :, None] == np.arange(2 * NA_KH - 1), F32)
    oh_dx = jnp.asarray((dx_idx[0, :, :, None] == np.arange(2 * NA_KW - 1)) & bias_ok[0, :, :, None], F32)
    scattered = [None] * L
    pending = None
    for l in reversed(range(L)):
        sv = saved[l]
        proj, w, g = sv["proj"], sv["w"], {}
        dh, = _mm(f"d_ff2_{l}", dxb, w["w_ff2"], "nt", [_sds((S, 4 * D), BF16)],
                  extras=(sv["h"],), extra_specs=(_tile_spec,), epilogue=dff_epilogue, tk_cap=2048)
        g["w_ff2"], = _mm(f"g_ff2_{l}", sv["a"], dxb, "tn", [_sds((4 * D, D), BF16)], tn_cap=2048)
        du2, = _mm(f"d_ff1_{l}", dh, w["w_ff1"], "nt", [_sds((S, D), F32)], epilogue=store_f32, tk_cap=2048)
        g["w_ff1"], = _mm(f"g_ff1_{l}", sv["u2"], dh, "tn", [_sds((D, 4 * D), BF16)], tn_cap=2048)
        dx2, dx2b, gsmall["norm_mlp"][l] = _rms_bwd(f"rms_mlp_bwd_{l}", sv["x2"], sv["r2"], wts["norm_mlp"][l][None], du2, dx)
        dya, dyb, dga, dgb = _mm(
            f"d_w_out_{l}", dx2b, w["w_out"], "nt", [_sds((S, D), BF16)] * 4,
            extras=(proj, proj, sv["y_a"], sv["y_b"]),
            extra_specs=(_col_spec(off_ga), _col_spec(off_gb), _tile_spec, _tile_spec),
            epilogue=dmerge_epilogue, tn_cap=512, tk_cap=2048)
        g["w_out"], = _mm(f"g_w_out_{l}", sv["merged"], dx2b, "tn", [_sds((D, D), BF16)], tn_cap=2048)
        do_a, = _mm(f"d_o_mla_{l}", dya, w["w_o_mla"], "nt", [_sds((S, H * HEAD), BF16)], tk_cap=2048)
        g["w_o_mla"], = _mm(f"g_o_mla_{l}", sv["o_a"], dya, "tn", [_sds((H * HEAD, D), BF16)], tn_cap=2048)
        do_b, = _mm(f"d_o_na_{l}", dyb, w["w_o_na"], "nt", [_sds((S, NAW), BF16)], tk_cap=2048)
        g["w_o_na"], = _mm(f"g_o_na_{l}", sv["o_b"], dyb, "tn", [_sds((NAW, D), BF16)], tn_cap=2048)
        dq_na, dk_na, dv_na, dbrow = _na_bwd(l, proj, brow, sv["o_b"], do_b, off_q, off_k, off_v, na_scale)
        tmp = jnp.einsum("hoqn,qnx->honx", dbrow, oh_dx, precision=lax.Precision.HIGHEST)
        gsmall["rpb"][l] = jnp.einsum("honx,ony->hyx", tmp, oh_dy, precision=lax.Precision.HIGHEST)
        dl = _delta(l, sv["o_a"], do_a)
        comm = _scatter_comm(pending, kinds, shard_shapes) if pending is not None else None
        dq_f, dkv, dkp_h, *landed = _mla_bwd(l, sv["q"], sv["kv"], sv["kp"], do_a, sv["lse"], dl, comm)
        if pending is not None:
            scattered[l + 1] = landed
        dq = _rope_bwd_q(l, dq_f, cos_t, sin_t, mla_scale)
        dnq, = _mm(f"d_q_up_{l}", dq, w["w_uq"], "nt", [_sds((S, QL), F32)], epilogue=store_f32, tk_cap=2048)
        g_uq, = _mm(f"g_q_up_{l}", sv["nq"], dq, "tn", [_sds((QL, H * 256), BF16)], tn_cap=2048)
        g["w_uq"] = g_uq.reshape(QL, H, 256)[:, :, :HEAD + ROPE].reshape(QL, H * (HEAD + ROPE))
        dnkv, = _mm(f"d_kv_up_{l}", dkv, w["w_ukv"], "nt", [_sds((S, KL), F32)], epilogue=store_f32, tk_cap=2048)
        g["w_ukv"], = _mm(f"g_kv_up_{l}", sv["nkv"], dkv, "tn", [_sds((KL, H * 256), BF16)], tn_cap=2048)
        dcq, dckv, dkpe, gsmall["norm_qa"][l], gsmall["norm_kva"][l] = _lat_bwd(
            l, proj, sv["rq"], sv["rkv"], wts["norm_qa"][l][None], wts["norm_kva"][l][None], dnq, dnkv, dkp_h,
            cos_t, sin_t, off_cq, off_ckv)
        dproj = jnp.concatenate([dga, dgb, dq_na, dk_na.astype(BF16), dv_na.astype(BF16), dcq, dckv, dkpe], axis=1)
        du, = _mm(f"d_proj_{l}", dproj, w["w_in"], "nt", [_sds((S, D), F32)], epilogue=store_f32, tk_cap=2048)
        g_in, = _mm(f"g_proj_{l}", sv["u"], dproj, "tn", [_sds((D, PW), BF16)], tn_cap=2048)
        back = [None] * 8
        for pos_new, i in enumerate(order):
            back[i] = g_in[:, new_off[pos_new]:new_off[pos_new] + widths[i]]
        g_orig = jnp.concatenate(back, axis=1)
        g["w_in"] = jnp.stack([g_orig[:, k * nloc:(k + 1) * nloc] for k in range(N_CHIPS)])
        pending = [g[n] for n in BIG]
        dx, dxb, gsmall["norm_mix"][l] = _rms_bwd(f"rms_mix_bwd_{l}", sv["x"], sv["r1"], wts["norm_mix"][l][None], du, dx2)
    grad_x = dx.reshape(1, S, D)
    scattered[0] = _run_comm("grad_scatter_0", _scatter_comm(pending, kinds, shard_shapes))

    parts = []
    for i, n in enumerate(BIG):
        stack = lax.empty((L,) + tuple(shard_shapes[i]), F32)
        for l in range(L):
            stack = _grad_partial(f"grad_partial_{n}_{l}", scattered[l][2 * i], scattered[l][2 * i + 1], stack, l)
        parts.append(stack.reshape(L * shard_shapes[i][0], shard_shapes[i][1]))
    others = _sibling_swap(parts)
    part_w, other_w = dict(zip(BIG, parts)), dict(zip(BIG, others))
    two = lambda t: t.reshape(t.shape[0] * t.shape[1], t.shape[2])

    small_shapes = [wts[n].shape for n in SMALL]
    small_g = [jnp.stack([g.reshape(wts[n].shape[1:]) for g in gsmall[n]]) for n in SMALL if n != "norm_final"]
    small_g.append(dg_final.reshape(D))
    gsum = _small_allreduce(_pack(small_g))
    zeros = jnp.zeros_like(gsum)
    sg, sd, sm, svv = _adamw("adamw_small", gsum, zeros, _pack([wts[n] for n in SMALL]), _pack([mom[n] for n in SMALL]),
                             _pack([var[n] for n in SMALL]))
    res = {n: {} for n in WEIGHTS}
    for key, flat in (("g", sg), ("d", sd), ("m", sm), ("v", svv)):
        for n, arr in zip(SMALL, _unpack(flat, small_shapes)):
            res[n][key] = arr
    for n in BIG:
        shp = wts[n].shape
        outs = _adamw(f"adamw_{n}", part_w[n], other_w[n], two(wts[n]), two(mom[n]), two(var[n]))
        for key, arr in zip(("g", "d", "m", "v"), outs):
            res[n][key] = arr.reshape(shp)

    return (loss, grad_x, *[res[n]["g"] for n in WEIGHTS], *[res[n]["d"] for n in WEIGHTS],
            *[res[n]["m"] for n in WEIGHTS], *[res[n]["v"] for n in WEIGHTS])
```

```python
import functools

import numpy as np
import jax
import jax.numpy as jnp
from jax import lax
from jax.experimental import pallas as pl
from jax.experimental.pallas import tpu as pltpu

F32 = jnp.float32
BF16 = jnp.bfloat16
MESH = pl.DeviceIdType.MESH

EPS = 1e-6
ROPE_THETA = 10000.0
ROPE = 64
HEAD = 128
GRID_W = 64
NA_KH = 8
NA_KW = 16
N_CHIPS = 4
N_DEV = 8
NEG = -1e30
LOG2E = 1.4426950408889634

ADAM_LR = 0.001
ADAM_B1 = 0.9
ADAM_B2 = 0.999
ADAM_EPS = 1e-08
ADAM_WD = 0.01
ADAM_STEP = 10

VMEM_LIMIT = 56 * 1024 * 1024

BIG = ("w_in", "w_uq", "w_ukv", "w_o_mla", "w_o_na", "w_out", "w_ff1", "w_ff2")
EARLY = BIG[:3]
LATE = BIG[3:]
ROW_SHARDED = ("w_out", "w_ff2")
SMALL = ("norm_mix", "norm_qa", "norm_kva", "rpb", "norm_mlp", "norm_final")
WEIGHTS = ("norm_mix", "w_in", "norm_qa", "w_uq", "norm_kva", "w_ukv", "rpb", "w_o_mla", "w_o_na",
           "w_out", "norm_mlp", "w_ff1", "w_ff2", "norm_final")


def _cparams(sem, **kw):
    return pltpu.CompilerParams(dimension_semantics=sem, vmem_limit_bytes=VMEM_LIMIT, **kw)


def _tile(n, cap, unit=128):
    if n <= cap:
        return n
    best = None
    for t in range(unit, cap + 1, unit):
        if n % t == 0:
            best = t
    assert best is not None, (n, cap, unit)
    return best


def _sds(shape, dtype):
    return jax.ShapeDtypeStruct(shape, dtype)


_DIMS = {"nn": (((1,), (0,)), ((), ())), "nt": (((1,), (1,)), ((), ())), "tn": (((0,), (0,)), ((), ()))}


def _store_cast(acc, extra, outs):
    outs[0][...] = acc.astype(outs[0].dtype)


def _mm(name, a, b, mode, out_shapes, *, lb=None, extras=(), extra_specs=(), out_specs=None,
        epilogue=_store_cast, tm_cap=1024, tn_cap=1024, tk_cap=1024):
    bshape = b.shape[1:] if lb is not None else b.shape
    if mode == "nn":
        (M, K), (K2, N) = a.shape, bshape
    elif mode == "nt":
        (M, K), (N, K2) = a.shape, bshape
    else:
        (K, M), (K2, N) = a.shape, bshape
    assert K == K2, (name, a.shape, b.shape)
    tm, tn, tk = _tile(M, tm_cap), _tile(N, tn_cap), _tile(K, tk_cap)
    nk = K // tk
    if mode == "tn":
        a_spec = pl.BlockSpec((tk, tm), lambda i, j, k: (k, i))
    else:
        a_spec = pl.BlockSpec((tm, tk), lambda i, j, k: (i, k))
    bblk, bidx = ((tn, tk), lambda i, j, k: (j, k)) if mode == "nt" else ((tk, tn), lambda i, j, k: (k, j))
    if lb is not None:
        b_spec = pl.BlockSpec((None,) + bblk, lambda i, j, k: (lb,) + bidx(i, j, k))
    else:
        b_spec = pl.BlockSpec(bblk, bidx)
    ne, no = len(extras), len(out_shapes)
    if out_specs is None:
        out_specs = [lambda tm, tn: pl.BlockSpec((tm, tn), lambda i, j, k: (i, j))] * no
    dims = _DIMS[mode]

    def body(*refs):
        a_ref, b_ref = refs[0], refs[1]
        extra, outs, acc = refs[2:2 + ne], refs[2 + ne:2 + ne + no], refs[-1]
        k = pl.program_id(2)

        @pl.when(k == 0)
        def _():
            acc[...] = jnp.zeros_like(acc)

        acc[...] += lax.dot_general(a_ref[...], b_ref[...], dims, preferred_element_type=F32)

        @pl.when(k == nk - 1)
        def _():
            epilogue(acc[...], extra, outs)

    return pl.pallas_call(
        body, name=name, grid=(M // tm, N // tn, nk),
        in_specs=[a_spec, b_spec] + [s(tm, tn) for s in extra_specs],
        out_specs=[s(tm, tn) for s in out_specs],
        out_shape=list(out_shapes),
        scratch_shapes=[pltpu.VMEM((tm, tn), F32)],
        compiler_params=_cparams(("parallel", "parallel", "arbitrary")),
    )(a, b, *extras)


def _tile_spec(tm, tn):
    return pl.BlockSpec((tm, tn), lambda i, j, k: (i, j))


def _row_spec(width):
    return lambda tm, tn: pl.BlockSpec((tm, width), lambda i, j, k: (i, 0))


def _col_spec(off_cols):
    def make(tm, tn):
        assert off_cols % tn == 0, (off_cols, tn)
        return pl.BlockSpec((tm, tn), lambda i, j, k: (i, off_cols // tn + j))
    return make


def _rms_fwd(name, x, g):
    S, D = x.shape
    tm = _tile(S, 256, 8)

    def body(x_ref, g_ref, u_ref, r_ref):
        xv = x_ref[...]
        r = lax.rsqrt(jnp.mean(xv * xv, axis=-1, keepdims=True) + EPS)
        u_ref[...] = (xv * r * g_ref[...]).astype(BF16)
        r_ref[...] = r

    return pl.pallas_call(
        body, name=name, grid=(S // tm,),
        in_specs=[pl.BlockSpec((tm, D), lambda i: (i, 0)), pl.BlockSpec((1, D), lambda i: (0, 0))],
        out_specs=[pl.BlockSpec((tm, D), lambda i: (i, 0)), pl.BlockSpec((tm, 1), lambda i: (i, 0))],
        out_shape=[_sds((S, D), BF16), _sds((S, 1), F32)],
        compiler_params=_cparams(("parallel",)),
    )(x, g)


def _rms_bwd(name, x, r, g, du, dres):
    S, D = x.shape
    tm = _tile(S, 256, 8)

    def body(x_ref, r_ref, g_ref, du_ref, dres_ref, dx_ref, dxb_ref, dg_ref):
        rv = r_ref[...]
        xhat = x_ref[...] * rv
        duv = du_ref[...].astype(F32)
        dxh = duv * g_ref[...]
        m = jnp.mean(dxh * xhat, axis=-1, keepdims=True)
        dx = dres_ref[...] + rv * (dxh - xhat * m)
        dx_ref[...] = dx
        dxb_ref[...] = dx.astype(BF16)

        @pl.when(pl.program_id(0) == 0)
        def _():
            dg_ref[...] = jnp.zeros_like(dg_ref)

        dg_ref[...] += jnp.sum(duv * xhat, axis=0, keepdims=True)

    row = pl.BlockSpec((tm, D), lambda i: (i, 0))
    vec = pl.BlockSpec((1, D), lambda i: (0, 0))
    return pl.pallas_call(
        body, name=name, grid=(S // tm,),
        in_specs=[row, pl.BlockSpec((tm, 1), lambda i: (i, 0)), vec, row, row],
        out_specs=[row, row, vec],
        out_shape=[_sds((S, D), F32), _sds((S, D), BF16), _sds((1, D), F32)],
        compiler_params=_cparams(("arbitrary",)),
    )(x, r, g, du, dres)


def _final_loss(x, t, g):
    S, D = x.shape
    tm = _tile(S, 256, 8)

    def body(x_ref, t_ref, g_ref, loss_ref, dx_ref, dxb_ref, dg_ref):
        xv = x_ref[...]
        gv = g_ref[...]
        rv = lax.rsqrt(jnp.mean(xv * xv, axis=-1, keepdims=True) + EPS)
        xhat = xv * rv
        diff = xhat * gv - t_ref[...]
        dy = diff * (1.0 / D)
        dxh = dy * gv
        m = jnp.mean(dxh * xhat, axis=-1, keepdims=True)
        dx = rv * (dxh - xhat * m)
        dx_ref[...] = dx
        dxb_ref[...] = dx.astype(BF16)

        @pl.when(pl.program_id(0) == 0)
        def _():
            dg_ref[...] = jnp.zeros_like(dg_ref)
            loss_ref[...] = jnp.zeros_like(loss_ref)

        dg_ref[...] += jnp.sum(dy * xhat, axis=0, keepdims=True)
        per_row = jnp.mean(diff * diff, axis=-1, keepdims=True)
        loss_ref[...] += 0.5 * jnp.sum(per_row, axis=0, keepdims=True)

    row = pl.BlockSpec((tm, D), lambda i: (i, 0))
    vec = pl.BlockSpec((1, D), lambda i: (0, 0))
    return pl.pallas_call(
        body, name="final_loss", grid=(S // tm,),
        in_specs=[row, row, vec],
        out_specs=[pl.BlockSpec((1, 128), lambda i: (0, 0)), row, row, vec],
        out_shape=[_sds((1, 128), F32), _sds((S, D), F32), _sds((S, D), BF16), _sds((1, D), F32)],
        compiler_params=_cparams(("arbitrary",)),
    )(x, t, g)


def _rope128(v, cos_t, sin_t):
    lane = lax.broadcasted_iota(jnp.int32, v.shape, 1)
    up = pltpu.roll(v, 128 - ROPE // 2, 1)
    dn = pltpu.roll(v, ROPE // 2, 1)
    return v * cos_t + jnp.where(lane < ROPE // 2, up, dn) * sin_t


def _lat_fwd(l, proj, g_qa, g_kva, cos_t, sin_t, off_cq, off_ckv, off_kpe):
    S = proj.shape[0]
    QL, KL = g_qa.shape[1], g_kva.shape[1]
    tm = _tile(S, 512, 8)
    assert off_cq % QL == 0 and off_ckv % KL == 0 and off_kpe % 128 == 0

    def body(cq_ref, ckv_ref, kpe_ref, gq_ref, gkv_ref, cos_ref, sin_ref, nq_ref, nkv_ref, rq_ref, rkv_ref, kp_ref):
        for c_ref, g_ref, n_ref, r_ref in ((cq_ref, gq_ref, nq_ref, rq_ref), (ckv_ref, gkv_ref, nkv_ref, rkv_ref)):
            cv = c_ref[...].astype(F32)
            r = lax.rsqrt(jnp.mean(cv * cv, axis=-1, keepdims=True) + EPS)
            n_ref[...] = (cv * r * g_ref[...]).astype(BF16)
            r_ref[...] = r
        kp_ref[...] = _rope128(kpe_ref[...].astype(F32), cos_ref[...], sin_ref[...]).astype(BF16)

    col = lambda w, off: pl.BlockSpec((tm, w), lambda i: (i, off // w))
    row = lambda w: pl.BlockSpec((tm, w), lambda i: (i, 0))
    vec = lambda w: pl.BlockSpec((1, w), lambda i: (0, 0))
    return pl.pallas_call(
        body, name=f"lat_fwd_{l}", grid=(S // tm,),
        in_specs=[col(QL, off_cq), col(KL, off_ckv), col(128, off_kpe), vec(QL), vec(KL), row(128), row(128)],
        out_specs=[row(QL), row(KL), row(1), row(1), row(128)],
        out_shape=[_sds((S, QL), BF16), _sds((S, KL), BF16), _sds((S, 1), F32), _sds((S, 1), F32), _sds((S, 128), BF16)],
        compiler_params=_cparams(("parallel",)),
    )(proj, proj, proj, g_qa, g_kva, cos_t, sin_t)


def _lat_bwd(l, proj, rq, rkv, g_qa, g_kva, dnq, dnkv, dkp_h, cos_t, sin_t, off_cq, off_ckv):
    S = proj.shape[0]
    QL, KL = g_qa.shape[1], g_kva.shape[1]
    H = dkp_h.shape[0]
    tm = _tile(S, 512, 8)

    def body(cq_ref, ckv_ref, rq_ref, rkv_ref, gq_ref, gkv_ref, dnq_ref, dnkv_ref, dkp_ref, cos_ref, sin_ref,
             dcq_ref, dckv_ref, dkpe_ref, dgq_ref, dgkv_ref):
        first = pl.program_id(0) == 0
        for c_ref, r_ref, g_ref, dn_ref, dc_ref, dg_ref in (
                (cq_ref, rq_ref, gq_ref, dnq_ref, dcq_ref, dgq_ref),
                (ckv_ref, rkv_ref, gkv_ref, dnkv_ref, dckv_ref, dgkv_ref)):
            rv = r_ref[...]
            xhat = c_ref[...].astype(F32) * rv
            dn = dn_ref[...]
            dxh = dn * g_ref[...]
            m = jnp.mean(dxh * xhat, axis=-1, keepdims=True)
            dc_ref[...] = (rv * (dxh - xhat * m)).astype(BF16)

            @pl.when(first)
            def _():
                dg_ref[...] = jnp.zeros_like(dg_ref)

            dg_ref[...] += jnp.sum(dn * xhat, axis=0, keepdims=True)
        dkp = dkp_ref[0]
        for h in range(1, H):
            dkp = dkp + dkp_ref[h]
        dkpe_ref[...] = _rope128(dkp, cos_ref[...], -sin_ref[...]).astype(BF16)

    col = lambda w, off: pl.BlockSpec((tm, w), lambda i: (i, off // w))
    row = lambda w: pl.BlockSpec((tm, w), lambda i: (i, 0))
    vec = lambda w: pl.BlockSpec((1, w), lambda i: (0, 0))
    return pl.pallas_call(
        body, name=f"lat_bwd_{l}", grid=(S // tm,),
        in_specs=[col(QL, off_cq), col(KL, off_ckv), row(1), row(1), vec(QL), vec(KL), row(QL), row(KL),
                  pl.BlockSpec((H, tm, 128), lambda i: (0, i, 0)), row(128), row(128)],
        out_specs=[row(QL), row(KL), row(128), vec(QL), vec(KL)],
        out_shape=[_sds((S, QL), BF16), _sds((S, KL), BF16), _sds((S, 128), BF16), _sds((1, QL), F32), _sds((1, KL), F32)],
        compiler_params=_cparams(("arbitrary",)),
    )(proj, proj, rq, rkv, g_qa, g_kva, dnq, dnkv, dkp_h, cos_t, sin_t)


def _rope_bwd_q(l, dq, cos_t, sin_t, scale):
    S, W = dq.shape
    tm = _tile(S, 256, 8)
    nh = W // 256

    def body(dq_ref, cos_ref, sin_ref, out_ref):
        cv, sv = cos_ref[...] * scale, -sin_ref[...] * scale
        for h in range(nh):
            out_ref[:, h * 256:h * 256 + 128] = (dq_ref[:, h * 256:h * 256 + 128] * scale).astype(BF16)
            out_ref[:, h * 256 + 128:(h + 1) * 256] = _rope128(dq_ref[:, h * 256 + 128:(h + 1) * 256], cv, sv).astype(BF16)

    return pl.pallas_call(
        body, name=f"rope_bwd_q_{l}", grid=(S // tm,),
        in_specs=[pl.BlockSpec((tm, W), lambda i: (i, 0)), pl.BlockSpec((tm, 128), lambda i: (i, 0)),
                  pl.BlockSpec((tm, 128), lambda i: (i, 0))],
        out_specs=pl.BlockSpec((tm, W), lambda i: (i, 0)),
        out_shape=_sds((S, W), BF16),
        compiler_params=_cparams(("parallel",)),
    )(dq, cos_t, sin_t)


def _delta(l, o, do):
    S, W = o.shape
    H = W // HEAD
    tm = _tile(S, 1024, 8)

    def body(o_ref, do_ref, d_ref):
        d_ref[...] = jnp.sum(o_ref[...].astype(F32) * do_ref[...].astype(F32), axis=-1, keepdims=True)

    blk = pl.BlockSpec((tm, HEAD), lambda h, i: (i, h))
    return pl.pallas_call(
        body, name=f"delta_{l}", grid=(H, S // tm),
        in_specs=[blk, blk],
        out_specs=pl.BlockSpec((None, tm, 1), lambda h, i: (h, i, 0)),
        out_shape=_sds((H, S, 1), F32),
        compiler_params=_cparams(("parallel", "parallel")),
    )(o, do)


_NT = (((1,), (1,)), ((), ()))
_TN = (((0,), (0,)), ((), ()))


MLA_SUB = 256


def _carry(comm, body, n_in, n_out, n_scratch, steps):
    if comm is None:
        return body, [], [], [], []
    ni, no = len(comm.inputs), len(comm.out_shapes)

    def carrying(*refs):
        a = n_in + ni
        b = a + n_out + no
        ins, cin = refs[:n_in], refs[n_in:a]
        outs, cout = refs[a:a + n_out], refs[a + n_out:b]
        scratch, csem = refs[b:b + n_scratch], refs[b + n_scratch:]
        first, middle, last = steps()

        @pl.when(first)
        def _():
            comm.start(cin, cout, csem)

        @pl.when(middle)
        def _():
            comm.mid(cin, cout, csem)

        body(*ins, *outs, *scratch)

        @pl.when(last)
        def _():
            comm.finish(cin, cout, csem)

    hbm = pl.BlockSpec(memory_space=pl.ANY)
    return carrying, [hbm] * ni, [hbm] * no, comm.out_shapes, comm.scratch


def _mla_fwd(l, q, kv, kp, comm=None):
    S = q.shape[0]
    H = q.shape[1] // 256
    tq, tk = _tile(S, 1024, 8), _tile(S, 512, 128)
    sub = min(MLA_SUB, tq)
    nq, nk = S // tq, S // tk
    assert H >= 2

    def steps():
        h, i, k = pl.program_id(0), pl.program_id(1), pl.program_id(2)
        origin = (i == 0) & (k == 0)
        return (h == 0) & origin, (h == max(1, 5 * H // 8)) & origin, (h == H - 1) & (i == nq - 1) & (k == nk - 1)

    def body(q_ref, kn_ref, v_ref, kp_ref, o_ref, lse_ref, m_sc, acc_sc):
        ki = pl.program_id(2)

        @pl.when(ki == 0)
        def _():
            m_sc[...] = jnp.full_like(m_sc, NEG)
            acc_sc[...] = jnp.zeros_like(acc_sc)

        kc = jnp.concatenate([kn_ref[...], kp_ref[...]], axis=1)
        vx = jnp.concatenate([v_ref[...], jnp.ones((tk, 128), BF16)], axis=1)
        for r in range(tq // sub):
            rows = slice(r * sub, (r + 1) * sub)
            s = lax.dot_general(q_ref[rows, :], kc, _NT, preferred_element_type=F32)
            m_prev = m_sc[rows, :]
            m_new = jnp.maximum(m_prev, jnp.max(s, axis=-1, keepdims=True))
            alpha = jnp.exp(m_prev - m_new)
            p = jnp.exp(s - jnp.tile(m_new, (1, tk // 128)))
            acc_sc[rows, :] = (jnp.tile(alpha, (1, 2)) * acc_sc[rows, :]
                               + jnp.dot(p.astype(BF16), vx, preferred_element_type=F32))
            m_sc[rows, :] = m_new

        @pl.when(ki == nk - 1)
        def _():
            l = acc_sc[:, HEAD:]
            o_ref[...] = (acc_sc[:, :HEAD] / l).astype(BF16)
            lse_ref[...] = m_sc[:, :1] + jnp.log(l[:, :1])

    body, cin_specs, cout_specs, cout_shapes, cscratch = _carry(comm, body, 4, 2, 2, steps)
    return pl.pallas_call(
        body, name=f"mla_fwd_{l}", grid=(H, nq, nk),
        in_specs=[pl.BlockSpec((tq, 256), lambda h, i, k: (i, h)),
                  pl.BlockSpec((tk, HEAD), lambda h, i, k: (k, 2 * h)),
                  pl.BlockSpec((tk, HEAD), lambda h, i, k: (k, 2 * h + 1)),
                  pl.BlockSpec((tk, 128), lambda h, i, k: (k, 0))] + cin_specs,
        out_specs=[pl.BlockSpec((tq, HEAD), lambda h, i, k: (i, h)),
                   pl.BlockSpec((None, tq, 1), lambda h, i, k: (h, i, 0))] + cout_specs,
        out_shape=[_sds((S, H * HEAD), BF16), _sds((H, S, 1), F32)] + cout_shapes,
        scratch_shapes=[pltpu.VMEM((tq, 128), F32), pltpu.VMEM((tq, 2 * HEAD), F32)] + cscratch,
        compiler_params=_cparams(("arbitrary", "arbitrary", "arbitrary")),
    )(q, kv, kv, kp, *(comm.inputs if comm else ()))


def _mla_bwd(l, q, kv, kp, do, lse, delta, comm=None):
    S = q.shape[0]
    H = q.shape[1] // 256
    tq, tk = _tile(S, 1024, 8), _tile(S, 1024, 128)
    sub = min(MLA_SUB, tq)
    nq, nk = S // tq, S // tk

    def steps():
        h, k, i = pl.program_id(0), pl.program_id(1), pl.program_id(2)
        origin = (k == 0) & (i == 0)
        return (h == 0) & origin, (h == H // 2) & origin, (h == H - 1) & (k == nk - 1) & (i == nq - 1)

    def body(q_ref, kn_ref, v_ref, kp_ref, do_ref, lse_ref, dl_ref, dq_ref, dkv_ref, dkp_ref, dkc_sc, dv_sc):
        ki, qi = pl.program_id(1), pl.program_id(2)
        kc = jnp.concatenate([kn_ref[...], kp_ref[...]], axis=1)
        vv = v_ref[...]
        dkc, dv, dq_tiles = None, None, []
        for r in range(tq // sub):
            rows = slice(r * sub, (r + 1) * sub)
            qv, dov = q_ref[rows, :], do_ref[rows, :]
            s = lax.dot_general(qv, kc, _NT, preferred_element_type=F32)
            p = jnp.exp(s - lse_ref[rows, :])
            dv_r = lax.dot_general(p.astype(BF16), dov, _TN, preferred_element_type=F32)
            dp = lax.dot_general(dov, vv, _NT, preferred_element_type=F32)
            ds = (p * (dp - dl_ref[rows, :])).astype(BF16)
            dkc_r = lax.dot_general(ds, qv, _TN, preferred_element_type=F32)
            dq_tiles.append(jnp.dot(ds, kc, preferred_element_type=F32))
            dkc = dkc_r if dkc is None else dkc + dkc_r
            dv = dv_r if dv is None else dv + dv_r
        dq_tile = jnp.concatenate(dq_tiles, axis=0) if len(dq_tiles) > 1 else dq_tiles[0]

        @pl.when(qi == 0)
        def _():
            dkc_sc[...] = dkc
            dv_sc[...] = dv

        @pl.when(qi > 0)
        def _():
            dkc_sc[...] += dkc
            dv_sc[...] += dv

        rows = pl.ds(pl.multiple_of(qi * tq, tq), tq)

        @pl.when(ki == 0)
        def _():
            dq_ref[rows, :] = dq_tile

        @pl.when(ki > 0)
        def _():
            dq_ref[rows, :] += dq_tile

        @pl.when(qi == nq - 1)
        def _():
            dkv_ref[:, :HEAD] = dkc_sc[:, :HEAD].astype(BF16)
            dkv_ref[:, HEAD:] = dv_sc[...].astype(BF16)
            dkp_ref[...] = dkc_sc[:, HEAD:]

    body, cin_specs, cout_specs, cout_shapes, cscratch = _carry(comm, body, 7, 3, 2, steps)
    return pl.pallas_call(
        body, name=f"mla_bwd_{l}", grid=(H, nk, nq),
        in_specs=[pl.BlockSpec((tq, 256), lambda h, k, i: (i, h)),
                  pl.BlockSpec((tk, HEAD), lambda h, k, i: (k, 2 * h)),
                  pl.BlockSpec((tk, HEAD), lambda h, k, i: (k, 2 * h + 1)),
                  pl.BlockSpec((tk, 128), lambda h, k, i: (k, 0)),
                  pl.BlockSpec((tq, HEAD), lambda h, k, i: (i, h)),
                  pl.BlockSpec((None, tq, 1), lambda h, k, i: (h, i, 0)),
                  pl.BlockSpec((None, tq, 1), lambda h, k, i: (h, i, 0))] + cin_specs,
        out_specs=[pl.BlockSpec((S, 256), lambda h, k, i: (0, h)),
                   pl.BlockSpec((tk, 256), lambda h, k, i: (k, h)),
                   pl.BlockSpec((None, tk, 128), lambda h, k, i: (h, k, 0))] + cout_specs,
        out_shape=[_sds((S, H * 256), F32), _sds((S, H * 256), BF16), _sds((H, S, 128), F32)] + cout_shapes,
        scratch_shapes=[pltpu.VMEM((tk, 256), F32), pltpu.VMEM((tk, HEAD), F32)] + cscratch,
        compiler_params=_cparams(("arbitrary", "arbitrary", "arbitrary")),
    )(q, kv, kv, kp, do, lse, delta, *(comm.inputs if comm else ()))


NA_RB = 8


def _na_bias_index(rows):
    j = np.arange(NA_KH)
    dy = j[None, :] - (np.arange(8)[:, None] - 4) + 3
    c = np.arange(GRID_W)
    col_start = np.clip(c - NA_KW // 2, 0, GRID_W - NA_KW)
    ok = (c[None, :] >= col_start[:, None]) & (c[None, :] < col_start[:, None] + NA_KW)
    dx = np.clip(c[None, :] - c[:, None], -(NA_KW - 1), NA_KW - 1) + (NA_KW - 1)
    dy_full = np.broadcast_to(dy[:, None, :, None], (8, GRID_W, NA_KH, GRID_W)).reshape(8, GRID_W, NA_KH * GRID_W)
    dx_full = np.broadcast_to(dx[None, :, None, :], (8, GRID_W, NA_KH, GRID_W)).reshape(8, GRID_W, NA_KH * GRID_W)
    ok_full = np.broadcast_to(ok[None, :, None, :], (8, GRID_W, NA_KH, GRID_W)).reshape(8, GRID_W, NA_KH * GRID_W)
    valid = ok_full & (dy_full >= 0) & (dy_full <= 2 * NA_KH - 2)
    return np.clip(dy_full, 0, 2 * NA_KH - 2), dx_full, valid


def _na_bias(rpb, dx_masked):
    L, H, NY, NX = rpb.shape
    nkeys = NA_KH * GRID_W

    def body(rpb_ref, dx_ref, out_ref):
        base = (pl.program_id(0) * H + pl.program_id(1)) * (NY * NX)
        dxv = dx_ref[...]
        key_row = lax.shift_right_logical(lax.broadcasted_iota(jnp.int32, (1, nkeys), 1), 6)

        def variant(o, carry):
            acc = jnp.full((GRID_W, nkeys), NEG, F32)
            for xx in range(NX):
                row = jnp.zeros((1, nkeys), F32)
                for j in range(NA_KH):
                    row = jnp.where(key_row == j, rpb_ref[base + (j - o + NA_KH - 1) * NX + xx], row)
                acc = jnp.where(dxv == xx, row, acc)
            out_ref[o] = acc
            return carry

        lax.fori_loop(0, 8, variant, 0)

    return pl.pallas_call(
        body, name="na_bias", grid=(L, H),
        in_specs=[pl.BlockSpec(memory_space=pltpu.SMEM), pl.BlockSpec((GRID_W, nkeys), lambda l, h: (0, 0))],
        out_specs=pl.BlockSpec((None, None, 8, GRID_W, nkeys), lambda l, h: (l, h, 0, 0, 0)),
        out_shape=_sds((L, H, 8, GRID_W, nkeys), F32),
        compiler_params=_cparams(("parallel", "parallel")),
    )(rpb.reshape(-1), dx_masked)


def _na_row_window(rb, i, rows):
    r = rb * NA_RB + i
    ks = jnp.clip(r - NA_KH // 2, 0, rows - NA_KH)
    variant = r - ks
    return pl.ds(pl.multiple_of(ks * GRID_W, GRID_W), NA_KH * GRID_W), variant


def _na_fwd(l, proj, brow, off_q, off_k, off_v, scale):
    S = proj.shape[0]
    H = brow.shape[1]
    rows = S // GRID_W
    assert rows % NA_RB == 0 and rows >= NA_KH
    tq = NA_RB * GRID_W

    def body(q_ref, k_ref, v_ref, b_ref, o_ref):
        rb = pl.program_id(1)
        for i in range(NA_RB):
            win, variant = _na_row_window(rb, i, rows)
            qs = slice(i * GRID_W, (i + 1) * GRID_W)
            s = lax.dot_general(q_ref[qs, :], k_ref[win, :], _NT, preferred_element_type=F32) * scale + b_ref[variant]
            e = jnp.exp(s - jnp.max(s, axis=-1, keepdims=True))
            p = e / jnp.sum(e, axis=-1, keepdims=True)
            o_ref[qs, :] = jnp.dot(p.astype(BF16), v_ref[win, :], preferred_element_type=F32).astype(BF16)

    return pl.pallas_call(
        body, name=f"na_fwd_{l}", grid=(H, rows // NA_RB),
        in_specs=[pl.BlockSpec((tq, HEAD), lambda h, r: (r, off_q // HEAD + h)),
                  pl.BlockSpec((S, HEAD), lambda h, r: (0, off_k // HEAD + h)),
                  pl.BlockSpec((S, HEAD), lambda h, r: (0, off_v // HEAD + h)),
                  pl.BlockSpec((None, None, 8, GRID_W, NA_KH * GRID_W), lambda h, r: (l, h, 0, 0, 0))],
        out_specs=pl.BlockSpec((tq, HEAD), lambda h, r: (r, h)),
        out_shape=_sds((S, H * HEAD), BF16),
        compiler_params=_cparams(("parallel", "arbitrary")),
    )(proj, proj, proj, brow)


def _na_bwd(l, proj, brow, o, do, off_q, off_k, off_v, scale):
    S = proj.shape[0]
    H = brow.shape[1]
    rows = S // GRID_W
    tq = NA_RB * GRID_W

    def body(q_ref, k_ref, v_ref, b_ref, o_ref, do_ref, dq_ref, dk_ref, dv_ref, db_ref):
        rb = pl.program_id(1)

        @pl.when(rb == 0)
        def _():
            dk_ref[...] = jnp.zeros_like(dk_ref)
            dv_ref[...] = jnp.zeros_like(dv_ref)
            db_ref[...] = jnp.zeros_like(db_ref)

        for i in range(NA_RB):
            win, variant = _na_row_window(rb, i, rows)
            qs = slice(i * GRID_W, (i + 1) * GRID_W)
            qv, kw, vw, dov = q_ref[qs, :], k_ref[win, :], v_ref[win, :], do_ref[qs, :]
            s = lax.dot_general(qv, kw, _NT, preferred_element_type=F32) * scale + b_ref[variant]
            e = jnp.exp(s - jnp.max(s, axis=-1, keepdims=True))
            p = e / jnp.sum(e, axis=-1, keepdims=True)
            dv_ref[win, :] += lax.dot_general(p.astype(BF16), dov, _TN, preferred_element_type=F32)
            dp = lax.dot_general(dov, vw, _NT, preferred_element_type=F32)
            dl = jnp.sum(dov.astype(F32) * o_ref[qs, :].astype(F32), axis=-1, keepdims=True)
            ds = p * (dp - dl)
            db_ref[variant] += ds
            dsb = (ds * scale).astype(BF16)
            dq_ref[qs, :] = jnp.dot(dsb, kw, preferred_element_type=F32).astype(BF16)
            dk_ref[win, :] += lax.dot_general(dsb, qv, _TN, preferred_element_type=F32)

    qblk = pl.BlockSpec((tq, HEAD), lambda h, r: (r, h))
    full = pl.BlockSpec((S, HEAD), lambda h, r: (0, h))
    bias = pl.BlockSpec((None, None, 8, GRID_W, NA_KH * GRID_W), lambda h, r: (l, h, 0, 0, 0))
    dbias = pl.BlockSpec((None, 8, GRID_W, NA_KH * GRID_W), lambda h, r: (h, 0, 0, 0))
    return pl.pallas_call(
        body, name=f"na_bwd_{l}", grid=(H, rows // NA_RB),
        in_specs=[pl.BlockSpec((tq, HEAD), lambda h, r: (r, off_q // HEAD + h)),
                  pl.BlockSpec((S, HEAD), lambda h, r: (0, off_k // HEAD + h)),
                  pl.BlockSpec((S, HEAD), lambda h, r: (0, off_v // HEAD + h)),
                  bias, qblk, qblk],
        out_specs=[qblk, full, full, dbias],
        out_shape=[_sds((S, H * HEAD), BF16), _sds((S, H * HEAD), F32), _sds((S, H * HEAD), F32),
                   _sds(brow.shape[1:], F32)],
        compiler_params=_cparams(("parallel", "arbitrary")),
    )(proj, proj, proj, brow, o, do)


def _place():
    return lax.axis_index("x"), lax.axis_index("y"), lax.axis_index("c")


class _Comm:
    def __init__(self, inputs, out_shapes, scratch, start, mid, finish):
        self.inputs, self.out_shapes, self.scratch = list(inputs), list(out_shapes), list(scratch)
        self.start, self.mid, self.finish = start, mid, finish


def _run_comm(name, comm):
    ni, no = len(comm.inputs), len(comm.out_shapes)

    def body(*refs):
        parts = refs[:ni], refs[ni:ni + no], refs[ni + no:]
        comm.start(*parts)
        comm.mid(*parts)
        comm.finish(*parts)

    hbm = pl.BlockSpec(memory_space=pl.ANY)
    return pl.pallas_call(body, name=name, in_specs=[hbm] * ni, out_specs=[hbm] * no, out_shape=comm.out_shapes,
                          scratch_shapes=comm.scratch)(*comm.inputs)


def _full_shape(shard_shape, kind):
    A, B = shard_shape
    return {"col": (A, N_CHIPS * B), "row": (N_CHIPS * A, B), "slot": (N_CHIPS, A, B)}[kind]


def _shard_region(ref, kind, shard_shape, k, half=None):
    A, B = shard_shape
    lo, n = (0, A) if half is None else (pl.multiple_of(half * (A // 2), 16), A // 2)
    if kind == "col":
        return ref.at[pl.ds(lo, n), pl.ds(pl.multiple_of(k * B, 128), B)]
    if kind == "row":
        return ref.at[pl.ds(pl.multiple_of(k * A + lo, 16), n), :]
    return ref.at[k, pl.ds(lo, n), :]


def _gather_comm(shards, kinds):
    n = len(shards)
    shapes = [tuple(s.shape) for s in shards]
    assert all(s[0] % 32 == 0 for s in shapes)

    def copies(w, o, sems):
        send_sems, recv_sems, local_sems = sems
        x, y, c = _place()
        sibling = (x, y, 1 - c)
        chips = [(1 - x, y), (x, 1 - y), (1 - x, 1 - y)]

        def copy(k, src, dst, to):
            return pltpu.make_async_remote_copy(src_ref=src, dst_ref=dst, send_sem=send_sems.at[k],
                                                recv_sem=recv_sems.at[k], device_id=to, device_id_type=MESH)

        def region(i, cx, cy, half=None):
            return _shard_region(o[i], kinds[i], shapes[i], 2 * cx + cy, half)

        def my_half(i):
            A = shapes[i][0]
            return w[i].at[pl.ds(pl.multiple_of(c * (A // 2), 16), A // 2), :]

        pairs = [(i, j, chip) for i in range(n) for j, chip in enumerate(chips)]
        return dict(
            local=lambda: [pltpu.make_async_copy(w[i], region(i, x, y), local_sems.at[i]) for i in range(n)],
            first=lambda: [copy(6 * i + j, my_half(i), region(i, x, y, c), (*chip, c)) for i, j, chip in pairs],
            landed=lambda: [copy(6 * i + j, region(i, *chip, c), region(i, *chip, c), (*chip, c)) for i, j, chip in pairs],
            passed=lambda: [copy(6 * i + 3 + j, region(i, *chip, c), region(i, *chip, c), sibling) for i, j, chip in pairs],
            handed=lambda: [copy(6 * i + 3 + j, region(i, *chip, 1 - c), region(i, *chip, 1 - c), sibling)
                            for i, j, chip in pairs])

    def start(w, o, sems):
        cps = copies(w, o, sems)
        for cp in cps["local"]() + cps["first"]():
            cp.start()

    def mid(w, o, sems):
        cps = copies(w, o, sems)
        for arrived, onward in zip(cps["landed"](), cps["passed"]()):
            arrived.wait_recv()
            onward.start()

    def finish(w, o, sems):
        cps = copies(w, o, sems)
        for cp in cps["handed"]():
            cp.wait_recv()
        for cp in cps["first"]() + cps["passed"]():
            cp.wait_send()
        for cp in cps["local"]():
            cp.wait()

    return _Comm(shards, [_sds(_full_shape(s, k), BF16) for s, k in zip(shapes, kinds)],
                 [pltpu.SemaphoreType.DMA((6 * n,)), pltpu.SemaphoreType.DMA((6 * n,)), pltpu.SemaphoreType.DMA((n,))],
                 start, mid, finish)


def _scatter_comm(grads, kinds, shapes):
    n = len(grads)

    def copies(g, outs, sems):
        send_sems, recv_sems, local_sems = sems
        own, got = outs[0::2], outs[1::2]
        x, y, c = _place()
        chips = [(1 - x, y), (x, 1 - y), (1 - x, 1 - y)]
        local = [pltpu.make_async_copy(_shard_region(g[i], kinds[i], shapes[i], 2 * x + y), own[i], local_sems.at[i])
                 for i in range(n)]
        sends = [pltpu.make_async_remote_copy(
            src_ref=_shard_region(g[i], kinds[i], shapes[i], 2 * cx + cy), dst_ref=got[i].at[j],
            send_sem=send_sems.at[3 * i + j], recv_sem=recv_sems.at[3 * i + j],
            device_id=(cx, cy, c), device_id_type=MESH) for i in range(n) for j, (cx, cy) in enumerate(chips)]
        return local, sends

    def start(g, outs, sems):
        local, sends = copies(g, outs, sems)
        for cp in local + sends:
            cp.start()

    def mid(g, outs, sems):
        pass

    def finish(g, outs, sems):
        local, sends = copies(g, outs, sems)
        for cp in sends:
            cp.wait_recv()
        for cp in sends:
            cp.wait_send()
        for cp in local:
            cp.wait()

    out_shapes = []
    for s in shapes:
        out_shapes += [_sds(tuple(s), BF16), _sds((3,) + tuple(s), BF16)]
    return _Comm(grads, out_shapes,
                 [pltpu.SemaphoreType.DMA((3 * n,)), pltpu.SemaphoreType.DMA((3 * n,)), pltpu.SemaphoreType.DMA((n,))],
                 start, mid, finish)


def _grad_partial(name, own, got, stack, layer):
    A, W = own.shape
    tm = _tile(A, max(16, (1 << 19) // W // 16 * 16), 16)

    def body(own_ref, got_ref, stack_ref, out_ref):
        acc = own_ref[...].astype(F32)
        for j in range(3):
            acc = acc + got_ref[j].astype(F32)
        out_ref[...] = acc

    return pl.pallas_call(
        body, name=name, grid=(A // tm,),
        in_specs=[pl.BlockSpec((tm, W), lambda i: (i, 0)), pl.BlockSpec((3, tm, W), lambda i: (0, i, 0)),
                  pl.BlockSpec(memory_space=pl.ANY)],
        out_specs=pl.BlockSpec((None, tm, W), lambda i: (layer, i, 0)),
        out_shape=_sds(stack.shape, F32),
        input_output_aliases={2: 0},
        compiler_params=_cparams(("parallel",)),
    )(own, got, stack)


def _sibling_swap(parts):
    n = len(parts)

    def body(*refs):
        p, got = refs[:n], refs[n:2 * n]
        send_sems, recv_sems = refs[2 * n:]
        x, y, c = _place()
        copies = [pltpu.make_async_remote_copy(src_ref=p[i], dst_ref=got[i], send_sem=send_sems.at[i],
                                               recv_sem=recv_sems.at[i], device_id=(x, y, 1 - c), device_id_type=MESH)
                  for i in range(n)]
        for cp in copies:
            cp.start()
        for cp in copies:
            cp.wait()

    hbm = pl.BlockSpec(memory_space=pl.ANY)
    return pl.pallas_call(
        body, name="sibling_swap",
        in_specs=[hbm] * n, out_specs=[hbm] * n,
        out_shape=[_sds(p.shape, p.dtype) for p in parts],
        scratch_shapes=[pltpu.SemaphoreType.DMA((n,)), pltpu.SemaphoreType.DMA((n,))],
    )(*parts)


def _small_allreduce(vec):
    NR, W = vec.shape

    def body(v_ref, all_ref, sum_ref, send_sems, recv_sems):
        x, y, c = _place()
        me = 4 * x + 2 * y + c
        all_ref[me] = v_ref[...]
        copies = []
        for k in range(1, N_DEV):
            fx, fy, fc = (k >> 2) & 1, (k >> 1) & 1, k & 1
            peer = (x ^ fx, y ^ fy, c ^ fc)
            copies.append(pltpu.make_async_remote_copy(
                src_ref=v_ref, dst_ref=all_ref.at[me], send_sem=send_sems.at[k - 1], recv_sem=recv_sems.at[k - 1],
                device_id=peer, device_id_type=MESH))
        for cp in copies:
            cp.start()
        for cp in copies:
            cp.wait_recv()
        for cp in copies:
            cp.wait_send()
        acc = all_ref[0]
        for d in range(1, N_DEV):
            acc = acc + all_ref[d]
        sum_ref[...] = acc

    return pl.pallas_call(
        body, name="small_allreduce",
        in_specs=[pl.BlockSpec(memory_space=pltpu.VMEM)],
        out_specs=[pl.BlockSpec(memory_space=pltpu.VMEM), pl.BlockSpec(memory_space=pltpu.VMEM)],
        out_shape=[_sds((N_DEV, NR, W), F32), _sds((NR, W), F32)],
        scratch_shapes=[pltpu.SemaphoreType.DMA((N_DEV - 1,)), pltpu.SemaphoreType.DMA((N_DEV - 1,))],
    )(vec)[1]


def _adamw_math(g, w, m, v):
    m = ADAM_B1 * m + (1.0 - ADAM_B1) * g
    v = ADAM_B2 * v + (1.0 - ADAM_B2) * (g * g)
    m_hat = m / (1.0 - ADAM_B1 ** ADAM_STEP)
    v_hat = v / (1.0 - ADAM_B2 ** ADAM_STEP)
    delta = -ADAM_LR * (m_hat / (jnp.sqrt(v_hat) + ADAM_EPS) + ADAM_WD * w)
    return delta, m, v


def _adamw(name, ga, gb, w, m, v):
    rows, n = w.shape
    tm = _tile(rows, max(8, (1 << 18) // n // 8 * 8), 8)

    def body(ga_ref, gb_ref, w_ref, m_ref, v_ref, g_out, d_out, m_out, v_out):
        g = ga_ref[...] + gb_ref[...]
        delta, mn, vn = _adamw_math(g, w_ref[...], m_ref[...], v_ref[...])
        g_out[...] = g
        d_out[...] = delta
        m_out[...] = mn
        v_out[...] = vn

    blk = pl.BlockSpec((tm, n), lambda i: (i, 0))
    return pl.pallas_call(
        body, name=name, grid=(rows // tm,),
        in_specs=[blk] * 5, out_specs=[blk] * 4, out_shape=[_sds((rows, n), F32)] * 4,
        compiler_params=_cparams(("parallel",)),
    )(ga, gb, w, m, v)


def _pack(parts):
    flat = jnp.concatenate([p.reshape(-1) for p in parts])
    pad = (-flat.shape[0]) % 1024
    if pad:
        flat = jnp.concatenate([flat, jnp.zeros((pad,), flat.dtype)])
    return flat.reshape(-1, 128)


def _unpack(flat, shapes):
    flat = flat.reshape(-1)
    out, off = [], 0
    for s in shapes:
        n = int(np.prod(s))
        out.append(flat[off:off + n].reshape(s))
        off += n
    return out


def kernel(x, norm_mix, w_in, norm_qa, w_uq, norm_kva, w_ukv, rpb, w_o_mla, w_o_na, w_out, norm_mlp, w_ff1, w_ff2, norm_final, loss_target, m_norm_mix, m_w_in, m_norm_qa, m_w_uq, m_norm_kva, m_w_ukv, m_rpb, m_w_o_mla, m_w_o_na, m_w_out, m_norm_mlp, m_w_ff1, m_w_ff2, m_norm_final, v_norm_mix, v_w_in, v_norm_qa, v_w_uq, v_norm_kva, v_w_ukv, v_rpb, v_w_o_mla, v_w_o_na, v_w_out, v_norm_mlp, v_w_ff1, v_w_ff2, v_norm_final):
    wts = dict(norm_mix=norm_mix, w_in=w_in, norm_qa=norm_qa, w_uq=w_uq, norm_kva=norm_kva, w_ukv=w_ukv, rpb=rpb,
               w_o_mla=w_o_mla, w_o_na=w_o_na, w_out=w_out, norm_mlp=norm_mlp, w_ff1=w_ff1, w_ff2=w_ff2,
               norm_final=norm_final)
    mom = dict(norm_mix=m_norm_mix, w_in=m_w_in, norm_qa=m_norm_qa, w_uq=m_w_uq, norm_kva=m_norm_kva, w_ukv=m_w_ukv,
               rpb=m_rpb, w_o_mla=m_w_o_mla, w_o_na=m_w_o_na, w_out=m_w_out, norm_mlp=m_norm_mlp, w_ff1=m_w_ff1,
               w_ff2=m_w_ff2, norm_final=m_norm_final)
    var = dict(norm_mix=v_norm_mix, w_in=v_w_in, norm_qa=v_norm_qa, w_uq=v_w_uq, norm_kva=v_norm_kva, w_ukv=v_w_ukv,
               rpb=v_rpb, w_o_mla=v_w_o_mla, w_o_na=v_w_o_na, w_out=v_w_out, norm_mlp=v_norm_mlp, w_ff1=v_w_ff1,
               w_ff2=v_w_ff2, norm_final=v_norm_final)

    _, S, D = x.shape
    L = w_in.shape[0]
    QL, KL = norm_qa.shape[1], norm_kva.shape[1]
    H = w_uq.shape[2] * N_CHIPS // (HEAD + ROPE)
    NAW = w_o_na.shape[1]
    NH = NAW // HEAD
    rows = S // GRID_W
    x = x.reshape(S, D)
    target = loss_target.reshape(S, D)

    kind_of = {n: "slot" if n == "w_in" else ("row" if n in ROW_SHARDED else "col") for n in BIG}
    shape_of = {n: tuple(wts[n].shape[1:]) for n in BIG}

    def gather_of(items):
        return _gather_comm([wts[n][layer].astype(BF16) for n, layer in items], [kind_of[n] for n, _ in items])

    widths = (QL, KL, ROPE, NAW, NAW, NAW, D, D)
    starts = np.concatenate([[0], np.cumsum(widths)]).astype(int)
    order = (6, 7, 3, 4, 5, 0, 1, 2)
    nloc = w_in.shape[2]
    new_off = np.concatenate([[0], np.cumsum([widths[i] for i in order])]).astype(int)
    off_ga, off_gb, off_q, off_k, off_v, off_cq, off_ckv, off_kpe = (int(o) for o in new_off[:8])
    PW = int(new_off[-1]) + 128 - ROPE

    def prepared(full):
        pieces = []
        for i in order:
            for k in range(N_CHIPS):
                lo, hi = max(int(starts[i]), k * nloc), min(int(starts[i + 1]), (k + 1) * nloc)
                if lo < hi:
                    pieces.append(full["w_in"][k, :, lo - k * nloc:hi - k * nloc])
        full["w_in"] = jnp.concatenate(pieces + [jnp.zeros((D, 128 - ROPE), BF16)], axis=1)
        full["w_uq"] = jnp.pad(full["w_uq"].reshape(QL, H, HEAD + ROPE),
                               ((0, 0), (0, 0), (0, 256 - HEAD - ROPE))).reshape(QL, H * 256)
        return full

    pos = jnp.arange(S, dtype=F32)
    inv_freq = 1.0 / (ROPE_THETA ** (jnp.arange(0, ROPE, 2, dtype=F32) / ROPE))
    ang = pos[:, None] * inv_freq[None, :]
    cos, sin, zero = jnp.cos(ang), jnp.sin(ang), jnp.zeros((S, 128 - ROPE), F32)
    cos_t = jnp.concatenate([cos, cos, zero], axis=1)
    sin_t = jnp.concatenate([-sin, sin, zero], axis=1)

    dy_idx, dx_idx, bias_ok = _na_bias_index(rows)
    brow = _na_bias(wts["rpb"], jnp.asarray(np.where(bias_ok[0], dx_idx[0], -1), jnp.int32))
    mla_scale = float((HEAD + ROPE) ** -0.5)
    na_scale = float(HEAD ** -0.5)

    def rope_q_epilogue(acc, extra, outs):
        cv, sv = extra[0][...] * mla_scale, extra[1][...] * mla_scale
        for hh in range(acc.shape[1] // 256):
            outs[0][:, hh * 256:hh * 256 + 128] = (acc[:, hh * 256:hh * 256 + 128] * mla_scale).astype(BF16)
            outs[0][:, hh * 256 + 128:(hh + 1) * 256] = _rope128(acc[:, hh * 256 + 128:(hh + 1) * 256], cv, sv).astype(BF16)

    def store_f32(acc, extra, outs):
        outs[0][...] = acc

    def merge_epilogue(acc, extra, outs):
        ga, gb, ya = extra[0][...].astype(F32), extra[1][...].astype(F32), extra[2][...]
        outs[0][...] = (jax.nn.sigmoid(ga) * ya + jax.nn.sigmoid(gb) * acc).astype(BF16)
        outs[1][...] = acc

    def residual_epilogue(acc, extra, outs):
        outs[0][...] = extra[0][...] + acc

    def ff1_epilogue(acc, extra, outs):
        outs[0][...] = acc.astype(BF16)
        outs[1][...] = jnp.square(jnp.maximum(acc, 0.0)).astype(BF16)

    def dff_epilogue(acc, extra, outs):
        outs[0][...] = (acc * (2.0 * jnp.maximum(extra[0][...].astype(F32), 0.0))).astype(BF16)

    def dmerge_epilogue(acc, extra, outs):
        ga, gb = extra[0][...].astype(F32), extra[1][...].astype(F32)
        ya, yb = extra[2][...], extra[3][...]
        sa, sb = jax.nn.sigmoid(ga), jax.nn.sigmoid(gb)
        outs[0][...] = (acc * sa).astype(BF16)
        outs[1][...] = (acc * sb).astype(BF16)
        outs[2][...] = (acc * ya * sa * (1.0 - sa)).astype(BF16)
        outs[3][...] = (acc * yb * sb * (1.0 - sb)).astype(BF16)

    saved = []
    items = [(n, 0) for n in EARLY]
    w = prepared(dict(zip(EARLY, _run_comm("weight_gather_0", gather_of(items)))))
    for l in range(L):
        u, r1 = _rms_fwd(f"rms_mix_{l}", x, wts["norm_mix"][l][None])
        proj, = _mm(f"proj_{l}", u, w["w_in"], "nn", [_sds((S, PW), BF16)], tn_cap=2048, tk_cap=2048)
        nq, nkv, rq, rkv, kp = _lat_fwd(l, proj, wts["norm_qa"][l][None], wts["norm_kva"][l][None], cos_t, sin_t,
                                        off_cq, off_ckv, off_kpe)
        q, = _mm(f"q_up_{l}", nq, w["w_uq"], "nn", [_sds((S, H * 256), BF16)],
                 extras=(cos_t, sin_t), extra_specs=(_row_spec(128), _row_spec(128)), epilogue=rope_q_epilogue, tn_cap=512)
        kv, = _mm(f"kv_up_{l}", nkv, w["w_ukv"], "nn", [_sds((S, H * 256), BF16)])
        items = [(n, l) for n in LATE] + ([(n, l + 1) for n in EARLY] if l + 1 < L else [])
        o_a, lse, *gathered = _mla_fwd(l, q, kv, kp, gather_of(items))
        w.update(zip(LATE, gathered[:len(LATE)]))
        w_next = prepared(dict(zip(EARLY, gathered[len(LATE):]))) if l + 1 < L else None
        o_b = _na_fwd(l, proj, brow, off_q, off_k, off_v, na_scale)
        y_a, = _mm(f"o_mla_{l}", o_a, w["w_o_mla"], "nn", [_sds((S, D), F32)], epilogue=store_f32, tn_cap=512)
        merged, y_b = _mm(f"o_na_merge_{l}", o_b, w["w_o_na"], "nn", [_sds((S, D), BF16), _sds((S, D), F32)],
                          extras=(proj, proj, y_a), extra_specs=(_col_spec(off_ga), _col_spec(off_gb), _tile_spec),
                          epilogue=merge_epilogue, tn_cap=512)
        x2, = _mm(f"w_out_{l}", merged, w["w_out"], "nn", [_sds((S, D), F32)],
                  extras=(x,), extra_specs=(_tile_spec,), epilogue=residual_epilogue, tk_cap=2048)
        u2, r2 = _rms_fwd(f"rms_mlp_{l}", x2, wts["norm_mlp"][l][None])
        h, a = _mm(f"ff1_{l}", u2, w["w_ff1"], "nn", [_sds((S, 4 * D), BF16), _sds((S, 4 * D), BF16)],
                   epilogue=ff1_epilogue, tk_cap=2048)
        x3, = _mm(f"ff2_{l}", a, w["w_ff2"], "nn", [_sds((S, D), F32)],
                  extras=(x2,), extra_specs=(_tile_spec,), epilogue=residual_epilogue, tk_cap=2048)
        saved.append(dict(w=w, x=x, r1=r1, u=u, proj=proj, nq=nq, nkv=nkv, rq=rq, rkv=rkv, q=q, kv=kv, kp=kp, o_a=o_a,
                          lse=lse, o_b=o_b, y_a=y_a, y_b=y_b, merged=merged, x2=x2, r2=r2, u2=u2, h=h, a=a))
        x, w = x3, w_next

    loss_lanes, dx, dxb, dg_final = _final_loss(x, target, wts["norm_final"][None])
    loss = lax.psum(loss_lanes[0, 0], ("x", "y", "c"))

    gsmall = {n: [None] * L for n in SMALL if n != "norm_final"}
    oh_dy = jnp.asarray(dy_idx[:, 0, :, None] == np.arange(2 * NA_KH - 1), F32)
    oh_dx = jnp.asarray((dx_idx[0, :, :, None] == np.arange(2 * NA_KW - 1)) & bias_ok[0, :, :, None], F32)
    scattered = {}

    def scatter_of(items, grads):
        names = [n for n, _ in items]
        return _scatter_comm(grads, [kind_of[n] for n in names], [shape_of[n] for n in names])

    def record(items, landed):
        for j, item in enumerate(items):
            scattered[item] = (landed[2 * j], landed[2 * j + 1])

    pending = []
    for l in reversed(range(L)):
        sv = saved[l]
        proj, w, g = sv["proj"], sv["w"], {}
        dh, = _mm(f"d_ff2_{l}", dxb, w["w_ff2"], "nt", [_sds((S, 4 * D), BF16)],
                  extras=(sv["h"],), extra_specs=(_tile_spec,), epilogue=dff_epilogue, tk_cap=2048)
        g["w_ff2"], = _mm(f"g_ff2_{l}", sv["a"], dxb, "tn", [_sds((4 * D, D), BF16)], tn_cap=2048)
        du2, = _mm(f"d_ff1_{l}", dh, w["w_ff1"], "nt", [_sds((S, D), F32)], epilogue=store_f32, tk_cap=2048)
        g["w_ff1"], = _mm(f"g_ff1_{l}", sv["u2"], dh, "tn", [_sds((D, 4 * D), BF16)], tn_cap=2048)
        dx2, dx2b, gsmall["norm_mlp"][l] = _rms_bwd(f"rms_mlp_bwd_{l}", sv["x2"], sv["r2"], wts["norm_mlp"][l][None], du2, dx)
        dya, dyb, dga, dgb = _mm(
            f"d_w_out_{l}", dx2b, w["w_out"], "nt", [_sds((S, D), BF16)] * 4,
            extras=(proj, proj, sv["y_a"], sv["y_b"]),
            extra_specs=(_col_spec(off_ga), _col_spec(off_gb), _tile_spec, _tile_spec),
            epilogue=dmerge_epilogue, tn_cap=512, tk_cap=2048)
        g["w_out"], = _mm(f"g_w_out_{l}", sv["merged"], dx2b, "tn", [_sds((D, D), BF16)], tn_cap=2048)
        do_a, = _mm(f"d_o_mla_{l}", dya, w["w_o_mla"], "nt", [_sds((S, H * HEAD), BF16)], tk_cap=2048)
        g["w_o_mla"], = _mm(f"g_o_mla_{l}", sv["o_a"], dya, "tn", [_sds((H * HEAD, D), BF16)], tn_cap=2048)
        do_b, = _mm(f"d_o_na_{l}", dyb, w["w_o_na"], "nt", [_sds((S, NAW), BF16)], tk_cap=2048)
        g["w_o_na"], = _mm(f"g_o_na_{l}", sv["o_b"], dyb, "tn", [_sds((NAW, D), BF16)], tn_cap=2048)
        dq_na, dk_na, dv_na, dbrow = _na_bwd(l, proj, brow, sv["o_b"], do_b, off_q, off_k, off_v, na_scale)
        tmp = jnp.einsum("hoqn,qnx->honx", dbrow, oh_dx, precision=lax.Precision.HIGHEST)
        gsmall["rpb"][l] = jnp.einsum("honx,ony->hyx", tmp, oh_dy, precision=lax.Precision.HIGHEST)
        dl = _delta(l, sv["o_a"], do_a)
        pending += [(n, l, g[n]) for n in LATE]
        items = [(n, layer) for n, layer, _ in pending]
        dq_f, dkv, dkp_h, *landed = _mla_bwd(l, sv["q"], sv["kv"], sv["kp"], do_a, sv["lse"], dl,
                                             scatter_of(items, [arr for _, _, arr in pending]))
        record(items, landed)
        dq =_rope_bwd_q(l, dq_f, cos_t, sin_t, mla_scale)
        dnq, = _mm(f"d_q_up_{l}", dq, w["w_uq"], "nt", [_sds((S, QL), F32)], epilogue=store_f32, tk_cap=2048)
        g_uq, = _mm(f"g_q_up_{l}", sv["nq"], dq, "tn", [_sds((QL, H * 256), BF16)], tn_cap=2048)
        g["w_uq"] = g_uq.reshape(QL, H, 256)[:, :, :HEAD + ROPE].reshape(QL, H * (HEAD + ROPE))
        dnkv, = _mm(f"d_kv_up_{l}", dkv, w["w_ukv"], "nt", [_sds((S, KL), F32)], epilogue=store_f32, tk_cap=2048)
        g["w_ukv"], = _mm(f"g_kv_up_{l}", sv["nkv"], dkv, "tn", [_sds((KL, H * 256), BF16)], tn_cap=2048)
        dcq, dckv, dkpe, gsmall["norm_qa"][l], gsmall["norm_kva"][l] = _lat_bwd(
            l, proj, sv["rq"], sv["rkv"], wts["norm_qa"][l][None], wts["norm_kva"][l][None], dnq, dnkv, dkp_h,
            cos_t, sin_t, off_cq, off_ckv)
        dproj = jnp.concatenate([dga, dgb, dq_na, dk_na.astype(BF16), dv_na.astype(BF16), dcq, dckv, dkpe], axis=1)
        du, = _mm(f"d_proj_{l}", dproj, w["w_in"], "nt", [_sds((S, D), F32)], epilogue=store_f32, tk_cap=2048)
        g_in, = _mm(f"g_proj_{l}", sv["u"], dproj, "tn", [_sds((D, PW), BF16)], tn_cap=2048)
        back = [None] * 8
        for pos_new, i in enumerate(order):
            back[i] = g_in[:, new_off[pos_new]:new_off[pos_new] + widths[i]]
        g_orig = jnp.concatenate(back, axis=1)
        g["w_in"] = jnp.stack([g_orig[:, k * nloc:(k + 1) * nloc] for k in range(N_CHIPS)])
        pending = [(n, l, g[n]) for n in EARLY]
        dx, dxb, gsmall["norm_mix"][l] = _rms_bwd(f"rms_mix_bwd_{l}", sv["x"], sv["r1"], wts["norm_mix"][l][None], du, dx2)
    grad_x = dx.reshape(1, S, D)
    items = [(n, layer) for n, layer, _ in pending]
    record(items, _run_comm("grad_scatter_0", scatter_of(items, [arr for _, _, arr in pending])))

    parts = []
    for n in BIG:
        stack = lax.empty((L,) + shape_of[n], F32)
        for l in range(L):
            stack = _grad_partial(f"grad_partial_{n}_{l}", *scattered[(n, l)], stack, l)
        parts.append(stack.reshape(L * shape_of[n][0], shape_of[n][1]))
    others = _sibling_swap(parts)
    part_w, other_w = dict(zip(BIG, parts)), dict(zip(BIG, others))
    two = lambda t: t.reshape(t.shape[0] * t.shape[1], t.shape[2])

    small_shapes = [wts[n].shape for n in SMALL]
    small_g = [jnp.stack([g.reshape(wts[n].shape[1:]) for g in gsmall[n]]) for n in SMALL if n != "norm_final"]
    small_g.append(dg_final.reshape(D))
    gsum = _small_allreduce(_pack(small_g))
    zeros = jnp.zeros_like(gsum)
    sg, sd, sm, svv = _adamw("adamw_small", gsum, zeros, _pack([wts[n] for n in SMALL]), _pack([mom[n] for n in SMALL]),
                             _pack([var[n] for n in SMALL]))
    res = {n: {} for n in WEIGHTS}
    for key, flat in (("g", sg), ("d", sd), ("m", sm), ("v", svv)):
        for n, arr in zip(SMALL, _unpack(flat, small_shapes)):
            res[n][key] = arr
    for n in BIG:
        shp = wts[n].shape
        outs = _adamw(f"adamw_{n}", part_w[n], other_w[n], two(wts[n]), two(mom[n]), two(var[n]))
        for key, arr in zip(("g", "d", "m", "v"), outs):
            res[n][key] = arr.reshape(shp)

    return (loss, grad_x, *[res[n]["g"] for n in WEIGHTS], *[res[n]["d"] for n in WEIGHTS],
            *[res[n]["m"] for n in WEIGHTS], *[res[n]["v"] for n in WEIGHTS])
```

```python
import functools

import numpy as np
import jax
import jax.numpy as jnp
from jax import lax
from jax.experimental import pallas as pl
from jax.experimental.pallas import tpu as pltpu

F32 = jnp.float32
BF16 = jnp.bfloat16
MESH = pl.DeviceIdType.MESH

EPS = 1e-6
ROPE_THETA = 10000.0
ROPE = 64
HEAD = 128
GRID_W = 64
NA_KH = 8
NA_KW = 16
N_CHIPS = 4
N_DEV = 8
NEG = -1e30
LOG2E = 1.4426950408889634

ADAM_LR = 0.001
ADAM_B1 = 0.9
ADAM_B2 = 0.999
ADAM_EPS = 1e-08
ADAM_WD = 0.01
ADAM_STEP = 10

VMEM_LIMIT = 56 * 1024 * 1024

BIG = ("w_in", "w_uq", "w_ukv", "w_o_mla", "w_o_na", "w_out", "w_ff1", "w_ff2")
EARLY = BIG[:3]
LATE = BIG[3:]
ROW_SHARDED = ("w_out", "w_ff2")
SMALL = ("norm_mix", "norm_qa", "norm_kva", "rpb", "norm_mlp", "norm_final")
WEIGHTS = ("norm_mix", "w_in", "norm_qa", "w_uq", "norm_kva", "w_ukv", "rpb", "w_o_mla", "w_o_na",
           "w_out", "norm_mlp", "w_ff1", "w_ff2", "norm_final")


def _cparams(sem, **kw):
    return pltpu.CompilerParams(dimension_semantics=sem, vmem_limit_bytes=VMEM_LIMIT, **kw)


def _tile(n, cap, unit=128):
    if n <= cap:
        return n
    best = None
    for t in range(unit, cap + 1, unit):
        if n % t == 0:
            best = t
    assert best is not None, (n, cap, unit)
    return best


def _sds(shape, dtype):
    return jax.ShapeDtypeStruct(shape, dtype)


_DIMS = {"nn": (((1,), (0,)), ((), ())), "nt": (((1,), (1,)), ((), ())), "tn": (((0,), (0,)), ((), ()))}


def _store_cast(acc, extra, outs):
    outs[0][...] = acc.astype(outs[0].dtype)


def _mm(name, a, b, mode, out_shapes, *, lb=None, extras=(), extra_specs=(), out_specs=None,
        epilogue=_store_cast, tm_cap=1024, tn_cap=1024, tk_cap=1024):
    bshape = b.shape[1:] if lb is not None else b.shape
    if mode == "nn":
        (M, K), (K2, N) = a.shape, bshape
    elif mode == "nt":
        (M, K), (N, K2) = a.shape, bshape
    else:
        (K, M), (K2, N) = a.shape, bshape
    assert K == K2, (name, a.shape, b.shape)
    tm, tn, tk = _tile(M, tm_cap), _tile(N, tn_cap), _tile(K, tk_cap)
    nk = K // tk
    if mode == "tn":
        a_spec = pl.BlockSpec((tk, tm), lambda i, j, k: (k, i))
    else:
        a_spec = pl.BlockSpec((tm, tk), lambda i, j, k: (i, k))
    bblk, bidx = ((tn, tk), lambda i, j, k: (j, k)) if mode == "nt" else ((tk, tn), lambda i, j, k: (k, j))
    if lb is not None:
        b_spec = pl.BlockSpec((None,) + bblk, lambda i, j, k: (lb,) + bidx(i, j, k))
    else:
        b_spec = pl.BlockSpec(bblk, bidx)
    ne, no = len(extras), len(out_shapes)
    if out_specs is None:
        out_specs = [lambda tm, tn: pl.BlockSpec((tm, tn), lambda i, j, k: (i, j))] * no
    dims = _DIMS[mode]

    def body(*refs):
        a_ref, b_ref = refs[0], refs[1]
        extra, outs, acc = refs[2:2 + ne], refs[2 + ne:2 + ne + no], refs[-1]
        k = pl.program_id(2)

        @pl.when(k == 0)
        def _():
            acc[...] = jnp.zeros_like(acc)

        acc[...] += lax.dot_general(a_ref[...], b_ref[...], dims, preferred_element_type=F32)

        @pl.when(k == nk - 1)
        def _():
            epilogue(acc[...], extra, outs)

    return pl.pallas_call(
        body, name=name, grid=(M // tm, N // tn, nk),
        in_specs=[a_spec, b_spec] + [s(tm, tn) for s in extra_specs],
        out_specs=[s(tm, tn) for s in out_specs],
        out_shape=list(out_shapes),
        scratch_shapes=[pltpu.VMEM((tm, tn), F32)],
        compiler_params=_cparams(("parallel", "parallel", "arbitrary")),
    )(a, b, *extras)


def _tile_spec(tm, tn):
    return pl.BlockSpec((tm, tn), lambda i, j, k: (i, j))


def _row_spec(width):
    return lambda tm, tn: pl.BlockSpec((tm, width), lambda i, j, k: (i, 0))


def _col_spec(off_cols):
    def make(tm, tn):
        assert off_cols % tn == 0, (off_cols, tn)
        return pl.BlockSpec((tm, tn), lambda i, j, k: (i, off_cols // tn + j))
    return make


def _rms_fwd(name, x, g):
    S, D = x.shape
    tm = _tile(S, 256, 8)

    def body(x_ref, g_ref, u_ref, r_ref):
        xv = x_ref[...]
        r = lax.rsqrt(jnp.mean(xv * xv, axis=-1, keepdims=True) + EPS)
        u_ref[...] = (xv * r * g_ref[...]).astype(BF16)
        r_ref[...] = r

    return pl.pallas_call(
        body, name=name, grid=(S // tm,),
        in_specs=[pl.BlockSpec((tm, D), lambda i: (i, 0)), pl.BlockSpec((1, D), lambda i: (0, 0))],
        out_specs=[pl.BlockSpec((tm, D), lambda i: (i, 0)), pl.BlockSpec((tm, 1), lambda i: (i, 0))],
        out_shape=[_sds((S, D), BF16), _sds((S, 1), F32)],
        compiler_params=_cparams(("parallel",)),
    )(x, g)


def _rms_bwd(name, x, r, g, du, dres):
    S, D = x.shape
    tm = _tile(S, 256, 8)

    def body(x_ref, r_ref, g_ref, du_ref, dres_ref, dx_ref, dxb_ref, dg_ref):
        rv = r_ref[...]
        xhat = x_ref[...] * rv
        duv = du_ref[...].astype(F32)
        dxh = duv * g_ref[...]
        m = jnp.mean(dxh * xhat, axis=-1, keepdims=True)
        dx = dres_ref[...] + rv * (dxh - xhat * m)
        dx_ref[...] = dx
        dxb_ref[...] = dx.astype(BF16)

        @pl.when(pl.program_id(0) == 0)
        def _():
            dg_ref[...] = jnp.zeros_like(dg_ref)

        dg_ref[...] += jnp.sum(duv * xhat, axis=0, keepdims=True)

    row = pl.BlockSpec((tm, D), lambda i: (i, 0))
    vec = pl.BlockSpec((1, D), lambda i: (0, 0))
    return pl.pallas_call(
        body, name=name, grid=(S // tm,),
        in_specs=[row, pl.BlockSpec((tm, 1), lambda i: (i, 0)), vec, row, row],
        out_specs=[row, row, vec],
        out_shape=[_sds((S, D), F32), _sds((S, D), BF16), _sds((1, D), F32)],
        compiler_params=_cparams(("arbitrary",)),
    )(x, r, g, du, dres)


def _final_loss(x, t, g):
    S, D = x.shape
    tm = _tile(S, 256, 8)

    def body(x_ref, t_ref, g_ref, loss_ref, dx_ref, dxb_ref, dg_ref):
        xv = x_ref[...]
        gv = g_ref[...]
        rv = lax.rsqrt(jnp.mean(xv * xv, axis=-1, keepdims=True) + EPS)
        xhat = xv * rv
        diff = xhat * gv - t_ref[...]
        dy = diff * (1.0 / D)
        dxh = dy * gv
        m = jnp.mean(dxh * xhat, axis=-1, keepdims=True)
        dx = rv * (dxh - xhat * m)
        dx_ref[...] = dx
        dxb_ref[...] = dx.astype(BF16)

        @pl.when(pl.program_id(0) == 0)
        def _():
            dg_ref[...] = jnp.zeros_like(dg_ref)
            loss_ref[...] = jnp.zeros_like(loss_ref)

        dg_ref[...] += jnp.sum(dy * xhat, axis=0, keepdims=True)
        per_row = jnp.mean(diff * diff, axis=-1, keepdims=True)
        loss_ref[...] += 0.5 * jnp.sum(per_row, axis=0, keepdims=True)

    row = pl.BlockSpec((tm, D), lambda i: (i, 0))
    vec = pl.BlockSpec((1, D), lambda i: (0, 0))
    return pl.pallas_call(
        body, name="final_loss", grid=(S // tm,),
        in_specs=[row, row, vec],
        out_specs=[pl.BlockSpec((1, 128), lambda i: (0, 0)), row, row, vec],
        out_shape=[_sds((1, 128), F32), _sds((S, D), F32), _sds((S, D), BF16), _sds((1, D), F32)],
        compiler_params=_cparams(("arbitrary",)),
    )(x, t, g)


def _rope128(v, cos_t, sin_t):
    lane = lax.broadcasted_iota(jnp.int32, v.shape, 1)
    up = pltpu.roll(v, 128 - ROPE // 2, 1)
    dn = pltpu.roll(v, ROPE // 2, 1)
    return v * cos_t + jnp.where(lane < ROPE // 2, up, dn) * sin_t


def _lat_fwd(l, proj, g_qa, g_kva, cos_t, sin_t, off_cq, off_ckv, off_kpe):
    S = proj.shape[0]
    QL, KL = g_qa.shape[1], g_kva.shape[1]
    tm = _tile(S, 512, 8)
    assert off_cq % QL == 0 and off_ckv % KL == 0 and off_kpe % 128 == 0

    def body(cq_ref, ckv_ref, kpe_ref, gq_ref, gkv_ref, cos_ref, sin_ref, nq_ref, nkv_ref, rq_ref, rkv_ref, kp_ref):
        for c_ref, g_ref, n_ref, r_ref in ((cq_ref, gq_ref, nq_ref, rq_ref), (ckv_ref, gkv_ref, nkv_ref, rkv_ref)):
            cv = c_ref[...].astype(F32)
            r = lax.rsqrt(jnp.mean(cv * cv, axis=-1, keepdims=True) + EPS)
            n_ref[...] = (cv * r * g_ref[...]).astype(BF16)
            r_ref[...] = r
        kp_ref[...] = _rope128(kpe_ref[...].astype(F32), cos_ref[...], sin_ref[...]).astype(BF16)

    col = lambda w, off: pl.BlockSpec((tm, w), lambda i: (i, off // w))
    row = lambda w: pl.BlockSpec((tm, w), lambda i: (i, 0))
    vec = lambda w: pl.BlockSpec((1, w), lambda i: (0, 0))
    return pl.pallas_call(
        body, name=f"lat_fwd_{l}", grid=(S // tm,),
        in_specs=[col(QL, off_cq), col(KL, off_ckv), col(128, off_kpe), vec(QL), vec(KL), row(128), row(128)],
        out_specs=[row(QL), row(KL), row(1), row(1), row(128)],
        out_shape=[_sds((S, QL), BF16), _sds((S, KL), BF16), _sds((S, 1), F32), _sds((S, 1), F32), _sds((S, 128), BF16)],
        compiler_params=_cparams(("parallel",)),
    )(proj, proj, proj, g_qa, g_kva, cos_t, sin_t)


def _lat_bwd(l, proj, rq, rkv, g_qa, g_kva, dnq, dnkv, dkp_h, cos_t, sin_t, off_cq, off_ckv):
    S = proj.shape[0]
    QL, KL = g_qa.shape[1], g_kva.shape[1]
    H = dkp_h.shape[0]
    tm = _tile(S, 512, 8)

    def body(cq_ref, ckv_ref, rq_ref, rkv_ref, gq_ref, gkv_ref, dnq_ref, dnkv_ref, dkp_ref, cos_ref, sin_ref,
             dcq_ref, dckv_ref, dkpe_ref, dgq_ref, dgkv_ref):
        first = pl.program_id(0) == 0
        for c_ref, r_ref, g_ref, dn_ref, dc_ref, dg_ref in (
                (cq_ref, rq_ref, gq_ref, dnq_ref, dcq_ref, dgq_ref),
                (ckv_ref, rkv_ref, gkv_ref, dnkv_ref, dckv_ref, dgkv_ref)):
            rv = r_ref[...]
            xhat = c_ref[...].astype(F32) * rv
            dn = dn_ref[...]
            dxh = dn * g_ref[...]
            m = jnp.mean(dxh * xhat, axis=-1, keepdims=True)
            dc_ref[...] = (rv * (dxh - xhat * m)).astype(BF16)

            @pl.when(first)
            def _():
                dg_ref[...] = jnp.zeros_like(dg_ref)

            dg_ref[...] += jnp.sum(dn * xhat, axis=0, keepdims=True)
        dkp = dkp_ref[0]
        for h in range(1, H):
            dkp = dkp + dkp_ref[h]
        dkpe_ref[...] = _rope128(dkp, cos_ref[...], -sin_ref[...]).astype(BF16)

    col = lambda w, off: pl.BlockSpec((tm, w), lambda i: (i, off // w))
    row = lambda w: pl.BlockSpec((tm, w), lambda i: (i, 0))
    vec = lambda w: pl.BlockSpec((1, w), lambda i: (0, 0))
    return pl.pallas_call(
        body, name=f"lat_bwd_{l}", grid=(S // tm,),
        in_specs=[col(QL, off_cq), col(KL, off_ckv), row(1), row(1), vec(QL), vec(KL), row(QL), row(KL),
                  pl.BlockSpec((H, tm, 128), lambda i: (0, i, 0)), row(128), row(128)],
        out_specs=[row(QL), row(KL), row(128), vec(QL), vec(KL)],
        out_shape=[_sds((S, QL), BF16), _sds((S, KL), BF16), _sds((S, 128), BF16), _sds((1, QL), F32), _sds((1, KL), F32)],
        compiler_params=_cparams(("arbitrary",)),
    )(proj, proj, rq, rkv, g_qa, g_kva, dnq, dnkv, dkp_h, cos_t, sin_t)


def _rope_bwd_q(l, dq, cos_t, sin_t, scale):
    S, W = dq.shape
    tm = _tile(S, 256, 8)
    nh = W // 256

    def body(dq_ref, cos_ref, sin_ref, out_ref):
        cv, sv = cos_ref[...] * scale, -sin_ref[...] * scale
        for h in range(nh):
            out_ref[:, h * 256:h * 256 + 128] = (dq_ref[:, h * 256:h * 256 + 128] * scale).astype(BF16)
            out_ref[:, h * 256 + 128:(h + 1) * 256] = _rope128(dq_ref[:, h * 256 + 128:(h + 1) * 256], cv, sv).astype(BF16)

    return pl.pallas_call(
        body, name=f"rope_bwd_q_{l}", grid=(S // tm,),
        in_specs=[pl.BlockSpec((tm, W), lambda i: (i, 0)), pl.BlockSpec((tm, 128), lambda i: (i, 0)),
                  pl.BlockSpec((tm, 128), lambda i: (i, 0))],
        out_specs=pl.BlockSpec((tm, W), lambda i: (i, 0)),
        out_shape=_sds((S, W), BF16),
        compiler_params=_cparams(("parallel",)),
    )(dq, cos_t, sin_t)


def _delta(l, o, do):
    S, W = o.shape
    H = W // HEAD
    tm = _tile(S, 1024, 8)

    def body(o_ref, do_ref, d_ref):
        d_ref[...] = jnp.sum(o_ref[...].astype(F32) * do_ref[...].astype(F32), axis=-1, keepdims=True)

    blk = pl.BlockSpec((tm, HEAD), lambda h, i: (i, h))
    return pl.pallas_call(
        body, name=f"delta_{l}", grid=(H, S // tm),
        in_specs=[blk, blk],
        out_specs=pl.BlockSpec((None, tm, 1), lambda h, i: (h, i, 0)),
        out_shape=_sds((H, S, 1), F32),
        compiler_params=_cparams(("parallel", "parallel")),
    )(o, do)


_NT = (((1,), (1,)), ((), ()))
_TN = (((0,), (0,)), ((), ()))


MLA_SUB = 256


def _carry(comm, body, n_in, n_out, n_scratch, steps):
    if comm is None:
        return body, [], [], [], []
    ni, no = len(comm.inputs), len(comm.out_shapes)

    def carrying(*refs):
        a = n_in + ni
        b = a + n_out + no
        ins, cin = refs[:n_in], refs[n_in:a]
        outs, cout = refs[a:a + n_out], refs[a + n_out:b]
        scratch, csem = refs[b:b + n_scratch], refs[b + n_scratch:]
        first, middle, last = steps()

        @pl.when(first)
        def _():
            comm.start(cin, cout, csem)

        @pl.when(middle)
        def _():
            comm.mid(cin, cout, csem)

        body(*ins, *outs, *scratch)

        @pl.when(last)
        def _():
            comm.finish(cin, cout, csem)

    hbm = pl.BlockSpec(memory_space=pl.ANY)
    return carrying, [hbm] * ni, [hbm] * no, comm.out_shapes, comm.scratch


def _mla_fwd(l, q, kv, kp, comm=None):
    S = q.shape[0]
    H = q.shape[1] // 256
    tq, tk = _tile(S, 1024, 8), _tile(S, 1024, 128)
    sub = min(MLA_SUB, tq)
    nq, nk = S // tq, S // tk
    assert H >= 2

    def steps():
        h, i, k = pl.program_id(0), pl.program_id(1), pl.program_id(2)
        origin = (i == 0) & (k == 0)
        return (h == 0) & origin, (h == max(1, 5 * H // 8)) & origin, (h == H - 1) & (i == nq - 1) & (k == nk - 1)

    def body(q_ref, kn_ref, v_ref, kp_ref, o_ref, lse_ref, m_sc, acc_sc):
        ki = pl.program_id(2)

        @pl.when(ki == 0)
        def _():
            m_sc[...] = jnp.full_like(m_sc, NEG)
            acc_sc[...] = jnp.zeros_like(acc_sc)

        kc = jnp.concatenate([kn_ref[...], kp_ref[...]], axis=1)
        vx = jnp.concatenate([v_ref[...], jnp.ones((tk, 128), BF16)], axis=1)
        for r in range(tq // sub):
            rows = slice(r * sub, (r + 1) * sub)
            s = lax.dot_general(q_ref[rows, :], kc, _NT, preferred_element_type=F32)
            m_prev = m_sc[rows, :]
            m_new = jnp.maximum(m_prev, jnp.max(s, axis=-1, keepdims=True))
            alpha = jnp.exp(m_prev - m_new)
            p = jnp.exp(s - jnp.tile(m_new, (1, tk // 128)))
            acc_sc[rows, :] = (jnp.tile(alpha, (1, 2)) * acc_sc[rows, :]
                               + jnp.dot(p.astype(BF16), vx, preferred_element_type=F32))
            m_sc[rows, :] = m_new

        @pl.when(ki == nk - 1)
        def _():
            l = acc_sc[:, HEAD:]
            o_ref[...] = (acc_sc[:, :HEAD] / l).astype(BF16)
            lse_ref[...] = m_sc[:, :1] + jnp.log(l[:, :1])

    body, cin_specs, cout_specs, cout_shapes, cscratch = _carry(comm, body, 4, 2, 2, steps)
    return pl.pallas_call(
        body, name=f"mla_fwd_{l}", grid=(H, nq, nk),
        in_specs=[pl.BlockSpec((tq, 256), lambda h, i, k: (i, h)),
                  pl.BlockSpec((tk, HEAD), lambda h, i, k: (k, 2 * h)),
                  pl.BlockSpec((tk, HEAD), lambda h, i, k: (k, 2 * h + 1)),
                  pl.BlockSpec((tk, 128), lambda h, i, k: (k, 0))] + cin_specs,
        out_specs=[pl.BlockSpec((tq, HEAD), lambda h, i, k: (i, h)),
                   pl.BlockSpec((None, tq, 1), lambda h, i, k: (h, i, 0))] + cout_specs,
        out_shape=[_sds((S, H * HEAD), BF16), _sds((H, S, 1), F32)] + cout_shapes,
        scratch_shapes=[pltpu.VMEM((tq, 128), F32), pltpu.VMEM((tq, 2 * HEAD), F32)] + cscratch,
        compiler_params=_cparams(("arbitrary", "arbitrary", "arbitrary")),
    )(q, kv, kv, kp, *(comm.inputs if comm else ()))


def _mla_bwd(l, q, kv, kp, do, lse, delta, comm=None):
    S = q.shape[0]
    H = q.shape[1] // 256
    tq, tk = _tile(S, 1024, 8), _tile(S, 1024, 128)
    sub = min(MLA_SUB, tq)
    nq, nk = S // tq, S // tk

    def steps():
        h, k, i = pl.program_id(0), pl.program_id(1), pl.program_id(2)
        origin = (k == 0) & (i == 0)
        return (h == 0) & origin, (h == H // 2) & origin, (h == H - 1) & (k == nk - 1) & (i == nq - 1)

    def body(q_ref, kn_ref, v_ref, kp_ref, do_ref, lse_ref, dl_ref, dq_ref, dkv_ref, dkp_ref, dkc_sc, dv_sc):
        ki, qi = pl.program_id(1), pl.program_id(2)
        kc = jnp.concatenate([kn_ref[...], kp_ref[...]], axis=1)
        vv = v_ref[...]
        dkc, dv, dq_tiles = None, None, []
        for r in range(tq // sub):
            rows = slice(r * sub, (r + 1) * sub)
            qv, dov = q_ref[rows, :], do_ref[rows, :]
            s = lax.dot_general(qv, kc, _NT, preferred_element_type=F32)
            p = jnp.exp(s - lse_ref[rows, :])
            dv_r = lax.dot_general(p.astype(BF16), dov, _TN, preferred_element_type=F32)
            dp = lax.dot_general(dov, vv, _NT, preferred_element_type=F32)
            ds = (p * (dp - dl_ref[rows, :])).astype(BF16)
            dkc_r = lax.dot_general(ds, qv, _TN, preferred_element_type=F32)
            dq_tiles.append(jnp.dot(ds, kc, preferred_element_type=F32))
            dkc = dkc_r if dkc is None else dkc + dkc_r
            dv = dv_r if dv is None else dv + dv_r
        dq_tile = jnp.concatenate(dq_tiles, axis=0) if len(dq_tiles) > 1 else dq_tiles[0]

        @pl.when(qi == 0)
        def _():
            dkc_sc[...] = dkc
            dv_sc[...] = dv

        @pl.when(qi > 0)
        def _():
            dkc_sc[...] += dkc
            dv_sc[...] += dv

        rows = pl.ds(pl.multiple_of(qi * tq, tq), tq)

        @pl.when(ki == 0)
        def _():
            dq_ref[rows, :] = dq_tile

        @pl.when(ki > 0)
        def _():
            dq_ref[rows, :] += dq_tile

        @pl.when(qi == nq - 1)
        def _():
            dkv_ref[:, :HEAD] = dkc_sc[:, :HEAD].astype(BF16)
            dkv_ref[:, HEAD:] = dv_sc[...].astype(BF16)
            dkp_ref[...] = dkc_sc[:, HEAD:]

    body, cin_specs, cout_specs, cout_shapes, cscratch = _carry(comm, body, 7, 3, 2, steps)
    return pl.pallas_call(
        body, name=f"mla_bwd_{l}", grid=(H, nk, nq),
        in_specs=[pl.BlockSpec((tq, 256), lambda h, k, i: (i, h)),
                  pl.BlockSpec((tk, HEAD), lambda h, k, i: (k, 2 * h)),
                  pl.BlockSpec((tk, HEAD), lambda h, k, i: (k, 2 * h + 1)),
                  pl.BlockSpec((tk, 128), lambda h, k, i: (k, 0)),
                  pl.BlockSpec((tq, HEAD), lambda h, k, i: (i, h)),
                  pl.BlockSpec((None, tq, 1), lambda h, k, i: (h, i, 0)),
                  pl.BlockSpec((None, tq, 1), lambda h, k, i: (h, i, 0))] + cin_specs,
        out_specs=[pl.BlockSpec((S, 256), lambda h, k, i: (0, h)),
                   pl.BlockSpec((tk, 256), lambda h, k, i: (k, h)),
                   pl.BlockSpec((None, tk, 128), lambda h, k, i: (h, k, 0))] + cout_specs,
        out_shape=[_sds((S, H * 256), F32), _sds((S, H * 256), BF16), _sds((H, S, 128), F32)] + cout_shapes,
        scratch_shapes=[pltpu.VMEM((tk, 256), F32), pltpu.VMEM((tk, HEAD), F32)] + cscratch,
        compiler_params=_cparams(("arbitrary", "arbitrary", "arbitrary")),
    )(q, kv, kv, kp, do, lse, delta, *(comm.inputs if comm else ()))


def _na_bias_index(rows):
    j = np.arange(NA_KH)
    dy = j[None, :] - (np.arange(8)[:, None] - 4) + 3
    c = np.arange(GRID_W)
    col_start = np.clip(c - NA_KW // 2, 0, GRID_W - NA_KW)
    ok = (c[None, :] >= col_start[:, None]) & (c[None, :] < col_start[:, None] + NA_KW)
    dx = np.clip(c[None, :] - c[:, None], -(NA_KW - 1), NA_KW - 1) + (NA_KW - 1)
    dy_full = np.broadcast_to(dy[:, None, :, None], (8, GRID_W, NA_KH, GRID_W)).reshape(8, GRID_W, NA_KH * GRID_W)
    dx_full = np.broadcast_to(dx[None, :, None, :], (8, GRID_W, NA_KH, GRID_W)).reshape(8, GRID_W, NA_KH * GRID_W)
    ok_full = np.broadcast_to(ok[None, :, None, :], (8, GRID_W, NA_KH, GRID_W)).reshape(8, GRID_W, NA_KH * GRID_W)
    valid = ok_full & (dy_full >= 0) & (dy_full <= 2 * NA_KH - 2)
    return np.clip(dy_full, 0, 2 * NA_KH - 2), dx_full, valid


def _na_bias(rpb, dx_masked):
    L, H, NY, NX = rpb.shape
    nkeys = NA_KH * GRID_W

    def body(rpb_ref, dx_ref, out_ref):
        base = (pl.program_id(0) * H + pl.program_id(1)) * (NY * NX)
        dxv = dx_ref[...]
        key_row = lax.shift_right_logical(lax.broadcasted_iota(jnp.int32, (1, nkeys), 1), 6)

        def variant(o, carry):
            acc = jnp.full((GRID_W, nkeys), NEG, F32)
            for xx in range(NX):
                row = jnp.zeros((1, nkeys), F32)
                for j in range(NA_KH):
                    row = jnp.where(key_row == j, rpb_ref[base + (j - o + NA_KH - 1) * NX + xx], row)
                acc = jnp.where(dxv == xx, row, acc)
            out_ref[o] = acc
            return carry

        lax.fori_loop(0, 8, variant, 0)

    return pl.pallas_call(
        body, name="na_bias", grid=(L, H),
        in_specs=[pl.BlockSpec(memory_space=pltpu.SMEM), pl.BlockSpec((GRID_W, nkeys), lambda l, h: (0, 0))],
        out_specs=pl.BlockSpec((None, None, 8, GRID_W, nkeys), lambda l, h: (l, h, 0, 0, 0)),
        out_shape=_sds((L, H, 8, GRID_W, nkeys), F32),
        compiler_params=_cparams(("parallel", "parallel")),
    )(rpb.reshape(-1), dx_masked)


NA_RB = 4
NA_WIN = NA_RB + NA_KH
NA_SUB = 128


def _na_block_plan():
    first = [(i, 0) for i in range(NA_RB)]
    interior = [(NA_KH // 2, i) for i in range(NA_RB)]
    last = [(NA_KH // 2 + i, NA_WIN - NA_KH) for i in range(NA_RB)]
    return first, interior, last


def _na_block_bias(brow):
    L, H = brow.shape[:2]
    neg = lambda n: jnp.full((L, H, GRID_W, n * GRID_W), NEG, F32)
    kinds = []
    for plan in _na_block_plan():
        rows_ = []
        for variant, joff in plan:
            parts = [neg(joff)] if joff else []
            parts.append(brow[:, :, variant])
            if NA_WIN - NA_KH - joff:
                parts.append(neg(NA_WIN - NA_KH - joff))
            rows_.append(jnp.concatenate(parts, axis=-1))
        kinds.append(jnp.concatenate(rows_, axis=-2))
    return jnp.stack(kinds, axis=2)


def _na_unblock(dblk):
    out = [None] * 8
    for kind, plan in enumerate(_na_block_plan()):
        for i, (variant, joff) in enumerate(plan):
            piece = dblk[:, kind, i * GRID_W:(i + 1) * GRID_W, joff * GRID_W:(joff + NA_KH) * GRID_W]
            out[variant] = piece if out[variant] is None else out[variant] + piece
    return jnp.stack(out, axis=1)


def _na_block(rb, rows):
    nrb = rows // NA_RB
    ks = jnp.clip(rb * NA_RB - NA_KH // 2, 0, rows - NA_WIN)
    kind = jnp.where(rb == 0, 0, jnp.where(rb == nrb - 1, 2, 1))
    return pl.ds(pl.multiple_of(ks * GRID_W, GRID_W), NA_WIN * GRID_W), kind


def _na_fwd(l, proj, bblk, off_q, off_k, off_v, scale):
    S = proj.shape[0]
    H = bblk.shape[1]
    rows = S // GRID_W
    assert rows % NA_RB == 0 and rows >= NA_WIN
    tq, nkeys = NA_RB * GRID_W, NA_WIN * GRID_W

    def body(q_ref, k_ref, v_ref, b_ref, o_ref):
        win, kind = _na_block(pl.program_id(1), rows)
        kw = k_ref[win, :]
        vx = jnp.concatenate([v_ref[win, :], jnp.ones((nkeys, 128), BF16)], axis=1)
        for c in range(tq // NA_SUB):
            qs = slice(c * NA_SUB, (c + 1) * NA_SUB)
            s = lax.dot_general(q_ref[qs, :], kw, _NT, preferred_element_type=F32) * scale + b_ref[kind, qs, :]
            e = jnp.exp(s - jnp.max(s, axis=-1, keepdims=True))
            ov = jnp.dot(e.astype(BF16), vx, preferred_element_type=F32)
            o_ref[qs, :] = (ov[:, :HEAD] / ov[:, HEAD:]).astype(BF16)

    return pl.pallas_call(
        body, name=f"na_fwd_{l}", grid=(H, rows // NA_RB),
        in_specs=[pl.BlockSpec((tq, HEAD), lambda h, r: (r, off_q // HEAD + h)),
                  pl.BlockSpec((S, HEAD), lambda h, r: (0, off_k // HEAD + h)),
                  pl.BlockSpec((S, HEAD), lambda h, r: (0, off_v // HEAD + h)),
                  pl.BlockSpec((None, None, 3, tq, nkeys), lambda h, r: (l, h, 0, 0, 0))],
        out_specs=pl.BlockSpec((tq, HEAD), lambda h, r: (r, h)),
        out_shape=_sds((S, H * HEAD), BF16),
        compiler_params=_cparams(("parallel", "arbitrary")),
    )(proj, proj, proj, bblk)


def _na_bwd(l, proj, bblk, o, do, off_q, off_k, off_v, scale):
    S = proj.shape[0]
    H = bblk.shape[1]
    rows = S // GRID_W
    tq, nkeys = NA_RB * GRID_W, NA_WIN * GRID_W

    def body(q_ref, k_ref, v_ref, b_ref, o_ref, do_ref, dq_ref, dk_ref, dv_ref, db_ref):
        rb = pl.program_id(1)

        @pl.when(rb == 0)
        def _():
            dk_ref[...] = jnp.zeros_like(dk_ref)
            dv_ref[...] = jnp.zeros_like(dv_ref)
            db_ref[...] = jnp.zeros_like(db_ref)

        win, kind = _na_block(rb, rows)
        kw, vw = k_ref[win, :], v_ref[win, :]
        dk, dv = None, None
        for c in range(tq // NA_SUB):
            qs = slice(c * NA_SUB, (c + 1) * NA_SUB)
            qv, dov = q_ref[qs, :], do_ref[qs, :]
            s = lax.dot_general(qv, kw, _NT, preferred_element_type=F32) * scale + b_ref[kind, qs, :]
            e = jnp.exp(s - jnp.max(s, axis=-1, keepdims=True))
            p = e * (1.0 / jnp.sum(e, axis=-1, keepdims=True))
            dv_c = lax.dot_general(p.astype(BF16), dov, _TN, preferred_element_type=F32)
            dp = lax.dot_general(dov, vw, _NT, preferred_element_type=F32)
            dl = jnp.sum(dov.astype(F32) * o_ref[qs, :].astype(F32), axis=-1, keepdims=True)
            ds = p * (dp - dl)
            db_ref[kind, qs, :] += ds
            dsb = (ds * scale).astype(BF16)
            dq_ref[qs, :] = jnp.dot(dsb, kw, preferred_element_type=F32).astype(BF16)
            dk_c = lax.dot_general(dsb, qv, _TN, preferred_element_type=F32)
            dk = dk_c if dk is None else dk + dk_c
            dv = dv_c if dv is None else dv + dv_c
        dk_ref[win, :] += dk
        dv_ref[win, :] += dv

    qblk = pl.BlockSpec((tq, HEAD), lambda h, r: (r, h))
    full = pl.BlockSpec((S, HEAD), lambda h, r: (0, h))
    bias = pl.BlockSpec((None, None, 3, tq, nkeys), lambda h, r: (l, h, 0, 0, 0))
    dbias = pl.BlockSpec((None, 3, tq, nkeys), lambda h, r: (h, 0, 0, 0))
    return pl.pallas_call(
        body, name=f"na_bwd_{l}", grid=(H, rows // NA_RB),
        in_specs=[pl.BlockSpec((tq, HEAD), lambda h, r: (r, off_q // HEAD + h)),
                  pl.BlockSpec((S, HEAD), lambda h, r: (0, off_k // HEAD + h)),
                  pl.BlockSpec((S, HEAD), lambda h, r: (0, off_v // HEAD + h)),
                  bias, qblk, qblk],
        out_specs=[qblk, full, full, dbias],
        out_shape=[_sds((S, H * HEAD), BF16), _sds((S, H * HEAD), F32), _sds((S, H * HEAD), F32),
                   _sds(bblk.shape[1:], F32)],
        compiler_params=_cparams(("parallel", "arbitrary")),
    )(proj, proj, proj, bblk, o, do)


def _place():
    return lax.axis_index("x"), lax.axis_index("y"), lax.axis_index("c")


class _Comm:
    def __init__(self, inputs, out_shapes, scratch, start, mid, finish):
        self.inputs, self.out_shapes, self.scratch = list(inputs), list(out_shapes), list(scratch)
        self.start, self.mid, self.finish = start, mid, finish


def _run_comm(name, comm):
    ni, no = len(comm.inputs), len(comm.out_shapes)

    def body(*refs):
        parts = refs[:ni], refs[ni:ni + no], refs[ni + no:]
        comm.start(*parts)
        comm.mid(*parts)
        comm.finish(*parts)

    hbm = pl.BlockSpec(memory_space=pl.ANY)
    return pl.pallas_call(body, name=name, in_specs=[hbm] * ni, out_specs=[hbm] * no, out_shape=comm.out_shapes,
                          scratch_shapes=comm.scratch)(*comm.inputs)


def _full_shape(shard_shape, kind):
    A, B = shard_shape
    return {"col": (A, N_CHIPS * B), "row": (N_CHIPS * A, B), "slot": (N_CHIPS, A, B)}[kind]


def _shard_region(ref, kind, shard_shape, k, half=None):
    A, B = shard_shape
    lo, n = (0, A) if half is None else (pl.multiple_of(half * (A // 2), 16), A // 2)
    if kind == "col":
        return ref.at[pl.ds(lo, n), pl.ds(pl.multiple_of(k * B, 128), B)]
    if kind == "row":
        return ref.at[pl.ds(pl.multiple_of(k * A + lo, 16), n), :]
    return ref.at[k, pl.ds(lo, n), :]


def _gather_comm(shards, kinds):
    n = len(shards)
    shapes = [tuple(s.shape) for s in shards]
    assert all(s[0] % 32 == 0 for s in shapes)

    def copies(w, o, sems):
        send_sems, recv_sems, local_sems = sems
        x, y, c = _place()
        sibling = (x, y, 1 - c)
        chips = [(1 - x, y), (x, 1 - y), (1 - x, 1 - y)]

        def copy(k, src, dst, to):
            return pltpu.make_async_remote_copy(src_ref=src, dst_ref=dst, send_sem=send_sems.at[k],
                                                recv_sem=recv_sems.at[k], device_id=to, device_id_type=MESH)

        def region(i, cx, cy, half=None):
            return _shard_region(o[i], kinds[i], shapes[i], 2 * cx + cy, half)

        def my_half(i):
            A = shapes[i][0]
            return w[i].at[pl.ds(pl.multiple_of(c * (A // 2), 16), A // 2), :]

        pairs = [(i, j, chip) for i in range(n) for j, chip in enumerate(chips)]
        return dict(
            local=lambda: [pltpu.make_async_copy(w[i], region(i, x, y), local_sems.at[i]) for i in range(n)],
            first=lambda: [copy(6 * i + j, my_half(i), region(i, x, y, c), (*chip, c)) for i, j, chip in pairs],
            landed=lambda: [copy(6 * i + j, region(i, *chip, c), region(i, *chip, c), (*chip, c)) for i, j, chip in pairs],
            passed=lambda: [copy(6 * i + 3 + j, region(i, *chip, c), region(i, *chip, c), sibling) for i, j, chip in pairs],
            handed=lambda: [copy(6 * i + 3 + j, region(i, *chip, 1 - c), region(i, *chip, 1 - c), sibling)
                            for i, j, chip in pairs])

    def start(w, o, sems):
        cps = copies(w, o, sems)
        for cp in cps["local"]() + cps["first"]():
            cp.start()

    def mid(w, o, sems):
        cps = copies(w, o, sems)
        for arrived, onward in zip(cps["landed"](), cps["passed"]()):
            arrived.wait_recv()
            onward.start()

    def finish(w, o, sems):
        cps = copies(w, o, sems)
        for cp in cps["handed"]():
            cp.wait_recv()
        for cp in cps["first"]() + cps["passed"]():
            cp.wait_send()
        for cp in cps["local"]():
            cp.wait()

    return _Comm(shards, [_sds(_full_shape(s, k), BF16) for s, k in zip(shapes, kinds)],
                 [pltpu.SemaphoreType.DMA((6 * n,)), pltpu.SemaphoreType.DMA((6 * n,)), pltpu.SemaphoreType.DMA((n,))],
                 start, mid, finish)


def _scatter_comm(grads, kinds, shapes):
    n = len(grads)

    def copies(g, outs, sems):
        send_sems, recv_sems, local_sems = sems
        own, got = outs[0::2], outs[1::2]
        x, y, c = _place()
        chips = [(1 - x, y), (x, 1 - y), (1 - x, 1 - y)]
        local = [pltpu.make_async_copy(_shard_region(g[i], kinds[i], shapes[i], 2 * x + y), own[i], local_sems.at[i])
                 for i in range(n)]
        sends = [pltpu.make_async_remote_copy(
            src_ref=_shard_region(g[i], kinds[i], shapes[i], 2 * cx + cy), dst_ref=got[i].at[j],
            send_sem=send_sems.at[3 * i + j], recv_sem=recv_sems.at[3 * i + j],
            device_id=(cx, cy, c), device_id_type=MESH) for i in range(n) for j, (cx, cy) in enumerate(chips)]
        return local, sends

    def start(g, outs, sems):
        local, sends = copies(g, outs, sems)
        for cp in local + sends:
            cp.start()

    def mid(g, outs, sems):
        pass

    def finish(g, outs, sems):
        local, sends = copies(g, outs, sems)
        for cp in sends:
            cp.wait_recv()
        for cp in sends:
            cp.wait_send()
        for cp in local:
            cp.wait()

    out_shapes = []
    for s in shapes:
        out_shapes += [_sds(tuple(s), BF16), _sds((3,) + tuple(s), BF16)]
    return _Comm(grads, out_shapes,
                 [pltpu.SemaphoreType.DMA((3 * n,)), pltpu.SemaphoreType.DMA((3 * n,)), pltpu.SemaphoreType.DMA((n,))],
                 start, mid, finish)


def _grad_partial(name, own, got, stack, layer):
    A, W = own.shape
    tm = _tile(A, max(16, (1 << 19) // W // 16 * 16), 16)

    def body(own_ref, got_ref, stack_ref, out_ref):
        acc = own_ref[...].astype(F32)
        for j in range(3):
            acc = acc + got_ref[j].astype(F32)
        out_ref[...] = acc

    return pl.pallas_call(
        body, name=name, grid=(A // tm,),
        in_specs=[pl.BlockSpec((tm, W), lambda i: (i, 0)), pl.BlockSpec((3, tm, W), lambda i: (0, i, 0)),
                  pl.BlockSpec(memory_space=pl.ANY)],
        out_specs=pl.BlockSpec((None, tm, W), lambda i: (layer, i, 0)),
        out_shape=_sds(stack.shape, F32),
        input_output_aliases={2: 0},
        compiler_params=_cparams(("parallel",)),
    )(own, got, stack)


def _sibling_swap(parts):
    n = len(parts)

    def body(*refs):
        p, got = refs[:n], refs[n:2 * n]
        send_sems, recv_sems = refs[2 * n:]
        x, y, c = _place()
        copies = [pltpu.make_async_remote_copy(src_ref=p[i], dst_ref=got[i], send_sem=send_sems.at[i],
                                               recv_sem=recv_sems.at[i], device_id=(x, y, 1 - c), device_id_type=MESH)
                  for i in range(n)]
        for cp in copies:
            cp.start()
        for cp in copies:
            cp.wait()

    hbm = pl.BlockSpec(memory_space=pl.ANY)
    return pl.pallas_call(
        body, name="sibling_swap",
        in_specs=[hbm] * n, out_specs=[hbm] * n,
        out_shape=[_sds(p.shape, p.dtype) for p in parts],
        scratch_shapes=[pltpu.SemaphoreType.DMA((n,)), pltpu.SemaphoreType.DMA((n,))],
    )(*parts)


def _small_allreduce(vec):
    NR, W = vec.shape

    def body(v_ref, all_ref, sum_ref, send_sems, recv_sems):
        x, y, c = _place()
        me = 4 * x + 2 * y + c
        all_ref[me] = v_ref[...]
        copies = []
        for k in range(1, N_DEV):
            fx, fy, fc = (k >> 2) & 1, (k >> 1) & 1, k & 1
            peer = (x ^ fx, y ^ fy, c ^ fc)
            copies.append(pltpu.make_async_remote_copy(
                src_ref=v_ref, dst_ref=all_ref.at[me], send_sem=send_sems.at[k - 1], recv_sem=recv_sems.at[k - 1],
                device_id=peer, device_id_type=MESH))
        for cp in copies:
            cp.start()
        for cp in copies:
            cp.wait_recv()
        for cp in copies:
            cp.wait_send()
        acc = all_ref[0]
        for d in range(1, N_DEV):
            acc = acc + all_ref[d]
        sum_ref[...] = acc

    return pl.pallas_call(
        body, name="small_allreduce",
        in_specs=[pl.BlockSpec(memory_space=pltpu.VMEM)],
        out_specs=[pl.BlockSpec(memory_space=pltpu.VMEM), pl.BlockSpec(memory_space=pltpu.VMEM)],
        out_shape=[_sds((N_DEV, NR, W), F32), _sds((NR, W), F32)],
        scratch_shapes=[pltpu.SemaphoreType.DMA((N_DEV - 1,)), pltpu.SemaphoreType.DMA((N_DEV - 1,))],
    )(vec)[1]


def _adamw_math(g, w, m, v):
    m = ADAM_B1 * m + (1.0 - ADAM_B1) * g
    v = ADAM_B2 * v + (1.0 - ADAM_B2) * (g * g)
    m_hat = m / (1.0 - ADAM_B1 ** ADAM_STEP)
    v_hat = v / (1.0 - ADAM_B2 ** ADAM_STEP)
    delta = -ADAM_LR * (m_hat / (jnp.sqrt(v_hat) + ADAM_EPS) + ADAM_WD * w)
    return delta, m, v


def _adamw(name, ga, gb, w, m, v):
    rows, n = w.shape
    tm = _tile(rows, max(8, (1 << 18) // n // 8 * 8), 8)

    def body(ga_ref, gb_ref, w_ref, m_ref, v_ref, g_out, d_out, m_out, v_out):
        g = ga_ref[...] + gb_ref[...]
        delta, mn, vn = _adamw_math(g, w_ref[...], m_ref[...], v_ref[...])
        g_out[...] = g
        d_out[...] = delta
        m_out[...] = mn
        v_out[...] = vn

    blk = pl.BlockSpec((tm, n), lambda i: (i, 0))
    return pl.pallas_call(
        body, name=name, grid=(rows // tm,),
        in_specs=[blk] * 5, out_specs=[blk] * 4, out_shape=[_sds((rows, n), F32)] * 4,
        compiler_params=_cparams(("parallel",)),
    )(ga, gb, w, m, v)


def _pack(parts):
    flat = jnp.concatenate([p.reshape(-1) for p in parts])
    pad = (-flat.shape[0]) % 1024
    if pad:
        flat = jnp.concatenate([flat, jnp.zeros((pad,), flat.dtype)])
    return flat.reshape(-1, 128)


def _unpack(flat, shapes):
    flat = flat.reshape(-1)
    out, off = [], 0
    for s in shapes:
        n = int(np.prod(s))
        out.append(flat[off:off + n].reshape(s))
        off += n
    return out


def kernel(x, norm_mix, w_in, norm_qa, w_uq, norm_kva, w_ukv, rpb, w_o_mla, w_o_na, w_out, norm_mlp, w_ff1, w_ff2, norm_final, loss_target, m_norm_mix, m_w_in, m_norm_qa, m_w_uq, m_norm_kva, m_w_ukv, m_rpb, m_w_o_mla, m_w_o_na, m_w_out, m_norm_mlp, m_w_ff1, m_w_ff2, m_norm_final, v_norm_mix, v_w_in, v_norm_qa, v_w_uq, v_norm_kva, v_w_ukv, v_rpb, v_w_o_mla, v_w_o_na, v_w_out, v_norm_mlp, v_w_ff1, v_w_ff2, v_norm_final):
    wts = dict(norm_mix=norm_mix, w_in=w_in, norm_qa=norm_qa, w_uq=w_uq, norm_kva=norm_kva, w_ukv=w_ukv, rpb=rpb,
               w_o_mla=w_o_mla, w_o_na=w_o_na, w_out=w_out, norm_mlp=norm_mlp, w_ff1=w_ff1, w_ff2=w_ff2,
               norm_final=norm_final)
    mom = dict(norm_mix=m_norm_mix, w_in=m_w_in, norm_qa=m_norm_qa, w_uq=m_w_uq, norm_kva=m_norm_kva, w_ukv=m_w_ukv,
               rpb=m_rpb, w_o_mla=m_w_o_mla, w_o_na=m_w_o_na, w_out=m_w_out, norm_mlp=m_norm_mlp, w_ff1=m_w_ff1,
               w_ff2=m_w_ff2, norm_final=m_norm_final)
    var = dict(norm_mix=v_norm_mix, w_in=v_w_in, norm_qa=v_norm_qa, w_uq=v_w_uq, norm_kva=v_norm_kva, w_ukv=v_w_ukv,
               rpb=v_rpb, w_o_mla=v_w_o_mla, w_o_na=v_w_o_na, w_out=v_w_out, norm_mlp=v_norm_mlp, w_ff1=v_w_ff1,
               w_ff2=v_w_ff2, norm_final=v_norm_final)

    _, S, D = x.shape
    L = w_in.shape[0]
    QL, KL = norm_qa.shape[1], norm_kva.shape[1]
    H = w_uq.shape[2] * N_CHIPS // (HEAD + ROPE)
    NAW = w_o_na.shape[1]
    NH = NAW // HEAD
    rows = S // GRID_W
    x = x.reshape(S, D)
    target = loss_target.reshape(S, D)

    kind_of = {n: "slot" if n == "w_in" else ("row" if n in ROW_SHARDED else "col") for n in BIG}
    shape_of = {n: tuple(wts[n].shape[1:]) for n in BIG}

    def gather_of(items):
        return _gather_comm([wts[n][layer].astype(BF16) for n, layer in items], [kind_of[n] for n, _ in items])

    widths = (QL, KL, ROPE, NAW, NAW, NAW, D, D)
    starts = np.concatenate([[0], np.cumsum(widths)]).astype(int)
    order = (6, 7, 3, 4, 5, 0, 1, 2)
    nloc = w_in.shape[2]
    new_off = np.concatenate([[0], np.cumsum([widths[i] for i in order])]).astype(int)
    off_ga, off_gb, off_q, off_k, off_v, off_cq, off_ckv, off_kpe = (int(o) for o in new_off[:8])
    PW = int(new_off[-1]) + 128 - ROPE

    def prepared(full):
        pieces = []
        for i in order:
            for k in range(N_CHIPS):
                lo, hi = max(int(starts[i]), k * nloc), min(int(starts[i + 1]), (k + 1) * nloc)
                if lo < hi:
                    pieces.append(full["w_in"][k, :, lo - k * nloc:hi - k * nloc])
        full["w_in"] = jnp.concatenate(pieces + [jnp.zeros((D, 128 - ROPE), BF16)], axis=1)
        full["w_uq"] = jnp.pad(full["w_uq"].reshape(QL, H, HEAD + ROPE),
                               ((0, 0), (0, 0), (0, 256 - HEAD - ROPE))).reshape(QL, H * 256)
        return full

    pos = jnp.arange(S, dtype=F32)
    inv_freq = 1.0 / (ROPE_THETA ** (jnp.arange(0, ROPE, 2, dtype=F32) / ROPE))
    ang = pos[:, None] * inv_freq[None, :]
    cos, sin, zero = jnp.cos(ang), jnp.sin(ang), jnp.zeros((S, 128 - ROPE), F32)
    cos_t = jnp.concatenate([cos, cos, zero], axis=1)
    sin_t = jnp.concatenate([-sin, sin, zero], axis=1)

    dy_idx, dx_idx, bias_ok = _na_bias_index(rows)
    bblk = _na_block_bias(_na_bias(wts["rpb"], jnp.asarray(np.where(bias_ok[0], dx_idx[0], -1), jnp.int32)))
    mla_scale = float((HEAD + ROPE) ** -0.5)
    na_scale = float(HEAD ** -0.5)

    def rope_q_epilogue(acc, extra, outs):
        cv, sv = extra[0][...] * mla_scale, extra[1][...] * mla_scale
        for hh in range(acc.shape[1] // 256):
            outs[0][:, hh * 256:hh * 256 + 128] = (acc[:, hh * 256:hh * 256 + 128] * mla_scale).astype(BF16)
            outs[0][:, hh * 256 + 128:(hh + 1) * 256] = _rope128(acc[:, hh * 256 + 128:(hh + 1) * 256], cv, sv).astype(BF16)

    def store_f32(acc, extra, outs):
        outs[0][...] = acc

    def merge_epilogue(acc, extra, outs):
        ga, gb, ya = extra[0][...].astype(F32), extra[1][...].astype(F32), extra[2][...]
        outs[0][...] = (jax.nn.sigmoid(ga) * ya + jax.nn.sigmoid(gb) * acc).astype(BF16)
        outs[1][...] = acc

    def residual_epilogue(acc, extra, outs):
        outs[0][...] = extra[0][...] + acc

    def ff1_epilogue(acc, extra, outs):
        outs[0][...] = acc.astype(BF16)
        outs[1][...] = jnp.square(jnp.maximum(acc, 0.0)).astype(BF16)

    def dff_epilogue(acc, extra, outs):
        outs[0][...] = (acc * (2.0 * jnp.maximum(extra[0][...].astype(F32), 0.0))).astype(BF16)

    def dmerge_epilogue(acc, extra, outs):
        ga, gb = extra[0][...].astype(F32), extra[1][...].astype(F32)
        ya, yb = extra[2][...], extra[3][...]
        sa, sb = jax.nn.sigmoid(ga), jax.nn.sigmoid(gb)
        outs[0][...] = (acc * sa).astype(BF16)
        outs[1][...] = (acc * sb).astype(BF16)
        outs[2][...] = (acc * ya * sa * (1.0 - sa)).astype(BF16)
        outs[3][...] = (acc * yb * sb * (1.0 - sb)).astype(BF16)

    saved = []
    items = [(n, 0) for n in EARLY]
    w = prepared(dict(zip(EARLY, _run_comm("weight_gather_0", gather_of(items)))))
    for l in range(L):
        u, r1 = _rms_fwd(f"rms_mix_{l}", x, wts["norm_mix"][l][None])
        proj, = _mm(f"proj_{l}", u, w["w_in"], "nn", [_sds((S, PW), BF16)], tn_cap=2048, tk_cap=2048)
        nq, nkv, rq, rkv, kp = _lat_fwd(l, proj, wts["norm_qa"][l][None], wts["norm_kva"][l][None], cos_t, sin_t,
                                        off_cq, off_ckv, off_kpe)
        q, = _mm(f"q_up_{l}", nq, w["w_uq"], "nn", [_sds((S, H * 256), BF16)],
                 extras=(cos_t, sin_t), extra_specs=(_row_spec(128), _row_spec(128)), epilogue=rope_q_epilogue, tn_cap=512)
        kv, = _mm(f"kv_up_{l}", nkv, w["w_ukv"], "nn", [_sds((S, H * 256), BF16)])
        items = [(n, l) for n in LATE] + ([(n, l + 1) for n in EARLY] if l + 1 < L else [])
        o_a, lse, *gathered = _mla_fwd(l, q, kv, kp, gather_of(items))
        w.update(zip(LATE, gathered[:len(LATE)]))
        w_next = prepared(dict(zip(EARLY, gathered[len(LATE):]))) if l + 1 < L else None
        o_b = _na_fwd(l, proj, bblk, off_q, off_k, off_v, na_scale)
        y_a, = _mm(f"o_mla_{l}", o_a, w["w_o_mla"], "nn", [_sds((S, D), F32)], epilogue=store_f32, tn_cap=512)
        merged, y_b = _mm(f"o_na_merge_{l}", o_b, w["w_o_na"], "nn", [_sds((S, D), BF16), _sds((S, D), F32)],
                          extras=(proj, proj, y_a), extra_specs=(_col_spec(off_ga), _col_spec(off_gb), _tile_spec),
                          epilogue=merge_epilogue, tn_cap=512)
        x2, = _mm(f"w_out_{l}", merged, w["w_out"], "nn", [_sds((S, D), F32)],
                  extras=(x,), extra_specs=(_tile_spec,), epilogue=residual_epilogue, tk_cap=2048)
        u2, r2 = _rms_fwd(f"rms_mlp_{l}", x2, wts["norm_mlp"][l][None])
        h, a = _mm(f"ff1_{l}", u2, w["w_ff1"], "nn", [_sds((S, 4 * D), BF16), _sds((S, 4 * D), BF16)],
                   epilogue=ff1_epilogue, tk_cap=2048)
        x3, = _mm(f"ff2_{l}", a, w["w_ff2"], "nn", [_sds((S, D), F32)],
                  extras=(x2,), extra_specs=(_tile_spec,), epilogue=residual_epilogue, tk_cap=2048)
        saved.append(dict(w=w, x=x, r1=r1, u=u, proj=proj, nq=nq, nkv=nkv, rq=rq, rkv=rkv, q=q, kv=kv, kp=kp, o_a=o_a,
                          lse=lse, o_b=o_b, y_a=y_a, y_b=y_b, merged=merged, x2=x2, r2=r2, u2=u2, h=h, a=a))
        x, w = x3, w_next

    loss_lanes, dx, dxb, dg_final = _final_loss(x, target, wts["norm_final"][None])
    loss = lax.psum(loss_lanes[0, 0], ("x", "y", "c"))

    gsmall = {n: [None] * L for n in SMALL if n != "norm_final"}
    oh_dy = jnp.asarray(dy_idx[:, 0, :, None] == np.arange(2 * NA_KH - 1), F32)
    oh_dx = jnp.asarray((dx_idx[0, :, :, None] == np.arange(2 * NA_KW - 1)) & bias_ok[0, :, :, None], F32)
    scattered = {}

    def scatter_of(items, grads):
        names = [n for n, _ in items]
        return _scatter_comm(grads, [kind_of[n] for n in names], [shape_of[n] for n in names])

    def record(items, landed):
        for j, item in enumerate(items):
            scattered[item] = (landed[2 * j], landed[2 * j + 1])

    pending = []
    for l in reversed(range(L)):
        sv = saved[l]
        proj, w, g = sv["proj"], sv["w"], {}
        dh, = _mm(f"d_ff2_{l}", dxb, w["w_ff2"], "nt", [_sds((S, 4 * D), BF16)],
                  extras=(sv["h"],), extra_specs=(_tile_spec,), epilogue=dff_epilogue, tk_cap=2048)
        g["w_ff2"], = _mm(f"g_ff2_{l}", sv["a"], dxb, "tn", [_sds((4 * D, D), BF16)], tn_cap=2048)
        du2, = _mm(f"d_ff1_{l}", dh, w["w_ff1"], "nt", [_sds((S, D), F32)], epilogue=store_f32, tk_cap=2048)
        g["w_ff1"], = _mm(f"g_ff1_{l}", sv["u2"], dh, "tn", [_sds((D, 4 * D), BF16)], tn_cap=2048)
        dx2, dx2b, gsmall["norm_mlp"][l] = _rms_bwd(f"rms_mlp_bwd_{l}", sv["x2"], sv["r2"], wts["norm_mlp"][l][None], du2, dx)
        dya, dyb, dga, dgb = _mm(
            f"d_w_out_{l}", dx2b, w["w_out"], "nt", [_sds((S, D), BF16)] * 4,
            extras=(proj, proj, sv["y_a"], sv["y_b"]),
            extra_specs=(_col_spec(off_ga), _col_spec(off_gb), _tile_spec, _tile_spec),
            epilogue=dmerge_epilogue, tn_cap=512, tk_cap=2048)
        g["w_out"], = _mm(f"g_w_out_{l}", sv["merged"], dx2b, "tn", [_sds((D, D), BF16)], tn_cap=2048)
        do_a, = _mm(f"d_o_mla_{l}", dya, w["w_o_mla"], "nt", [_sds((S, H * HEAD), BF16)], tk_cap=2048)
        g["w_o_mla"], = _mm(f"g_o_mla_{l}", sv["o_a"], dya, "tn", [_sds((H * HEAD, D), BF16)], tn_cap=2048)
        do_b, = _mm(f"d_o_na_{l}", dyb, w["w_o_na"], "nt", [_sds((S, NAW), BF16)], tk_cap=2048)
        g["w_o_na"], = _mm(f"g_o_na_{l}", sv["o_b"], dyb, "tn", [_sds((NAW, D), BF16)], tn_cap=2048)
        dq_na, dk_na, dv_na, dbblk = _na_bwd(l, proj, bblk, sv["o_b"], do_b, off_q, off_k, off_v, na_scale)
        dbrow = _na_unblock(dbblk)
        tmp =jnp.einsum("hoqn,qnx->honx", dbrow, oh_dx, precision=lax.Precision.HIGHEST)
        gsmall["rpb"][l] = jnp.einsum("honx,ony->hyx", tmp, oh_dy, precision=lax.Precision.HIGHEST)
        dl = _delta(l, sv["o_a"], do_a)
        pending += [(n, l, g[n]) for n in LATE]
        items = [(n, layer) for n, layer, _ in pending]
        dq_f, dkv, dkp_h, *landed = _mla_bwd(l, sv["q"], sv["kv"], sv["kp"], do_a, sv["lse"], dl,
                                             scatter_of(items, [arr for _, _, arr in pending]))
        record(items, landed)
        dq =_rope_bwd_q(l, dq_f, cos_t, sin_t, mla_scale)
        dnq, = _mm(f"d_q_up_{l}", dq, w["w_uq"], "nt", [_sds((S, QL), F32)], epilogue=store_f32, tk_cap=2048)
        g_uq, = _mm(f"g_q_up_{l}", sv["nq"], dq, "tn", [_sds((QL, H * 256), BF16)], tn_cap=2048)
        g["w_uq"] = g_uq.reshape(QL, H, 256)[:, :, :HEAD + ROPE].reshape(QL, H * (HEAD + ROPE))
        dnkv, = _mm(f"d_kv_up_{l}", dkv, w["w_ukv"], "nt", [_sds((S, KL), F32)], epilogue=store_f32, tk_cap=2048)
        g["w_ukv"], = _mm(f"g_kv_up_{l}", sv["nkv"], dkv, "tn", [_sds((KL, H * 256), BF16)], tn_cap=2048)
        dcq, dckv, dkpe, gsmall["norm_qa"][l], gsmall["norm_kva"][l] = _lat_bwd(
            l, proj, sv["rq"], sv["rkv"], wts["norm_qa"][l][None], wts["norm_kva"][l][None], dnq, dnkv, dkp_h,
            cos_t, sin_t, off_cq, off_ckv)
        dproj = jnp.concatenate([dga, dgb, dq_na, dk_na.astype(BF16), dv_na.astype(BF16), dcq, dckv, dkpe], axis=1)
        du, = _mm(f"d_proj_{l}", dproj, w["w_in"], "nt", [_sds((S, D), F32)], epilogue=store_f32, tk_cap=2048)
        g_in, = _mm(f"g_proj_{l}", sv["u"], dproj, "tn", [_sds((D, PW), BF16)], tn_cap=2048)
        back = [None] * 8
        for pos_new, i in enumerate(order):
            back[i] = g_in[:, new_off[pos_new]:new_off[pos_new] + widths[i]]
        g_orig = jnp.concatenate(back, axis=1)
        g["w_in"] = jnp.stack([g_orig[:, k * nloc:(k + 1) * nloc] for k in range(N_CHIPS)])
        pending = [(n, l, g[n]) for n in EARLY]
        dx, dxb, gsmall["norm_mix"][l] = _rms_bwd(f"rms_mix_bwd_{l}", sv["x"], sv["r1"], wts["norm_mix"][l][None], du, dx2)
    grad_x = dx.reshape(1, S, D)
    items = [(n, layer) for n, layer, _ in pending]
    record(items, _run_comm("grad_scatter_0", scatter_of(items, [arr for _, _, arr in pending])))

    parts = []
    for n in BIG:
        stack = lax.empty((L,) + shape_of[n], F32)
        for l in range(L):
            stack = _grad_partial(f"grad_partial_{n}_{l}", *scattered[(n, l)], stack, l)
        parts.append(stack.reshape(L * shape_of[n][0], shape_of[n][1]))
    others = _sibling_swap(parts)
    part_w, other_w = dict(zip(BIG, parts)), dict(zip(BIG, others))
    two = lambda t: t.reshape(t.shape[0] * t.shape[1], t.shape[2])

    small_shapes = [wts[n].shape for n in SMALL]
    small_g = [jnp.stack([g.reshape(wts[n].shape[1:]) for g in gsmall[n]]) for n in SMALL if n != "norm_final"]
    small_g.append(dg_final.reshape(D))
    gsum = _small_allreduce(_pack(small_g))
    zeros = jnp.zeros_like(gsum)
    sg, sd, sm, svv = _adamw("adamw_small", gsum, zeros, _pack([wts[n] for n in SMALL]), _pack([mom[n] for n in SMALL]),
                             _pack([var[n] for n in SMALL]))
    res = {n: {} for n in WEIGHTS}
    for key, flat in (("g", sg), ("d", sd), ("m", sm), ("v", svv)):
        for n, arr in zip(SMALL, _unpack(flat, small_shapes)):
            res[n][key] = arr
    for n in BIG:
        shp = wts[n].shape
        outs = _adamw(f"adamw_{n}", part_w[n], other_w[n], two(wts[n]), two(mom[n]), two(var[n]))
        for key, arr in zip(("g", "d", "m", "v"), outs):
            res[n][key] = arr.reshape(shp)

    return (loss, grad_x, *[res[n]["g"] for n in WEIGHTS], *[res[n]["d"] for n in WEIGHTS],
            *[res[n]["m"] for n in WEIGHTS], *[res[n]["v"] for n in WEIGHTS])
```

```python
import functools

import numpy as np
import jax
import jax.numpy as jnp
from jax import lax
from jax.experimental import pallas as pl
from jax.experimental.pallas import tpu as pltpu

F32 = jnp.float32
BF16 = jnp.bfloat16
MESH = pl.DeviceIdType.MESH

EPS = 1e-6
ROPE_THETA = 10000.0
ROPE = 64
HEAD = 128
GRID_W = 64
NA_KH = 8
NA_KW = 16
N_CHIPS = 4
N_DEV = 8
NEG = -1e30
LOG2E = 1.4426950408889634

ADAM_LR = 0.001
ADAM_B1 = 0.9
ADAM_B2 = 0.999
ADAM_EPS = 1e-08
ADAM_WD = 0.01
ADAM_STEP = 10

VMEM_LIMIT = 56 * 1024 * 1024

BIG = ("w_in", "w_uq", "w_ukv", "w_o_mla", "w_o_na", "w_out", "w_ff1", "w_ff2")
EARLY = BIG[:3]
LATE = BIG[3:]
ROW_SHARDED = ("w_out", "w_ff2")
SMALL = ("norm_mix", "norm_qa", "norm_kva", "rpb", "norm_mlp", "norm_final")
WEIGHTS = ("norm_mix", "w_in", "norm_qa", "w_uq", "norm_kva", "w_ukv", "rpb", "w_o_mla", "w_o_na",
           "w_out", "norm_mlp", "w_ff1", "w_ff2", "norm_final")


def _cparams(sem, **kw):
    return pltpu.CompilerParams(dimension_semantics=sem, vmem_limit_bytes=VMEM_LIMIT, **kw)


def _tile(n, cap, unit=128):
    if n <= cap:
        return n
    best = None
    for t in range(unit, cap + 1, unit):
        if n % t == 0:
            best = t
    assert best is not None, (n, cap, unit)
    return best


def _sds(shape, dtype):
    return jax.ShapeDtypeStruct(shape, dtype)


_DIMS = {"nn": (((1,), (0,)), ((), ())), "nt": (((1,), (1,)), ((), ())), "tn": (((0,), (0,)), ((), ()))}


def _store_cast(acc, extra, outs):
    outs[0][...] = acc.astype(outs[0].dtype)


def _mm(name, a, b, mode, out_shapes, *, lb=None, extras=(), extra_specs=(), out_specs=None,
        epilogue=_store_cast, tm_cap=1024, tn_cap=1024, tk_cap=1024):
    bshape = b.shape[1:] if lb is not None else b.shape
    if mode == "nn":
        (M, K), (K2, N) = a.shape, bshape
    elif mode == "nt":
        (M, K), (N, K2) = a.shape, bshape
    else:
        (K, M), (K2, N) = a.shape, bshape
    assert K == K2, (name, a.shape, b.shape)
    tm, tn, tk = _tile(M, tm_cap), _tile(N, tn_cap), _tile(K, tk_cap)
    nk = K // tk
    if mode == "tn":
        a_spec = pl.BlockSpec((tk, tm), lambda i, j, k: (k, i))
    else:
        a_spec = pl.BlockSpec((tm, tk), lambda i, j, k: (i, k))
    bblk, bidx = ((tn, tk), lambda i, j, k: (j, k)) if mode == "nt" else ((tk, tn), lambda i, j, k: (k, j))
    if lb is not None:
        b_spec = pl.BlockSpec((None,) + bblk, lambda i, j, k: (lb,) + bidx(i, j, k))
    else:
        b_spec = pl.BlockSpec(bblk, bidx)
    ne, no = len(extras), len(out_shapes)
    if out_specs is None:
        out_specs = [lambda tm, tn: pl.BlockSpec((tm, tn), lambda i, j, k: (i, j))] * no
    dims = _DIMS[mode]

    def body(*refs):
        a_ref, b_ref = refs[0], refs[1]
        extra, outs, acc = refs[2:2 + ne], refs[2 + ne:2 + ne + no], refs[-1]
        k = pl.program_id(2)

        @pl.when(k == 0)
        def _():
            acc[...] = jnp.zeros_like(acc)

        acc[...] += lax.dot_general(a_ref[...], b_ref[...], dims, preferred_element_type=F32)

        @pl.when(k == nk - 1)
        def _():
            epilogue(acc[...], extra, outs)

    return pl.pallas_call(
        body, name=name, grid=(M // tm, N // tn, nk),
        in_specs=[a_spec, b_spec] + [s(tm, tn) for s in extra_specs],
        out_specs=[s(tm, tn) for s in out_specs],
        out_shape=list(out_shapes),
        scratch_shapes=[pltpu.VMEM((tm, tn), F32)],
        compiler_params=_cparams(("parallel", "parallel", "arbitrary")),
    )(a, b, *extras)


def _tile_spec(tm, tn):
    return pl.BlockSpec((tm, tn), lambda i, j, k: (i, j))


def _row_spec(width):
    return lambda tm, tn: pl.BlockSpec((tm, width), lambda i, j, k: (i, 0))


def _col_spec(off_cols):
    def make(tm, tn):
        assert off_cols % tn == 0, (off_cols, tn)
        return pl.BlockSpec((tm, tn), lambda i, j, k: (i, off_cols // tn + j))
    return make


def _rms_fwd(name, x, g):
    S, D = x.shape
    tm = _tile(S, 256, 8)

    def body(x_ref, g_ref, u_ref, r_ref):
        xv = x_ref[...]
        r = lax.rsqrt(jnp.mean(xv * xv, axis=-1, keepdims=True) + EPS)
        u_ref[...] = (xv * r * g_ref[...]).astype(BF16)
        r_ref[...] = r

    return pl.pallas_call(
        body, name=name, grid=(S // tm,),
        in_specs=[pl.BlockSpec((tm, D), lambda i: (i, 0)), pl.BlockSpec((1, D), lambda i: (0, 0))],
        out_specs=[pl.BlockSpec((tm, D), lambda i: (i, 0)), pl.BlockSpec((tm, 1), lambda i: (i, 0))],
        out_shape=[_sds((S, D), BF16), _sds((S, 1), F32)],
        compiler_params=_cparams(("parallel",)),
    )(x, g)


def _rms_bwd(name, x, r, g, du, dres):
    S, D = x.shape
    tm = _tile(S, 256, 8)

    def body(x_ref, r_ref, g_ref, du_ref, dres_ref, dx_ref, dxb_ref, dg_ref):
        rv = r_ref[...]
        xhat = x_ref[...] * rv
        duv = du_ref[...].astype(F32)
        dxh = duv * g_ref[...]
        m = jnp.mean(dxh * xhat, axis=-1, keepdims=True)
        dx = dres_ref[...] + rv * (dxh - xhat * m)
        dx_ref[...] = dx
        dxb_ref[...] = dx.astype(BF16)

        @pl.when(pl.program_id(0) == 0)
        def _():
            dg_ref[...] = jnp.zeros_like(dg_ref)

        dg_ref[...] += jnp.sum(duv * xhat, axis=0, keepdims=True)

    row = pl.BlockSpec((tm, D), lambda i: (i, 0))
    vec = pl.BlockSpec((1, D), lambda i: (0, 0))
    return pl.pallas_call(
        body, name=name, grid=(S // tm,),
        in_specs=[row, pl.BlockSpec((tm, 1), lambda i: (i, 0)), vec, row, row],
        out_specs=[row, row, vec],
        out_shape=[_sds((S, D), F32), _sds((S, D), BF16), _sds((1, D), F32)],
        compiler_params=_cparams(("arbitrary",)),
    )(x, r, g, du, dres)


def _final_loss(x, t, g):
    S, D = x.shape
    tm = _tile(S, 256, 8)

    def body(x_ref, t_ref, g_ref, loss_ref, dx_ref, dxb_ref, dg_ref):
        xv = x_ref[...]
        gv = g_ref[...]
        rv = lax.rsqrt(jnp.mean(xv * xv, axis=-1, keepdims=True) + EPS)
        xhat = xv * rv
        diff = xhat * gv - t_ref[...]
        dy = diff * (1.0 / D)
        dxh = dy * gv
        m = jnp.mean(dxh * xhat, axis=-1, keepdims=True)
        dx = rv * (dxh - xhat * m)
        dx_ref[...] = dx
        dxb_ref[...] = dx.astype(BF16)

        @pl.when(pl.program_id(0) == 0)
        def _():
            dg_ref[...] = jnp.zeros_like(dg_ref)
            loss_ref[...] = jnp.zeros_like(loss_ref)

        dg_ref[...] += jnp.sum(dy * xhat, axis=0, keepdims=True)
        per_row = jnp.mean(diff * diff, axis=-1, keepdims=True)
        loss_ref[...] += 0.5 * jnp.sum(per_row, axis=0, keepdims=True)

    row = pl.BlockSpec((tm, D), lambda i: (i, 0))
    vec = pl.BlockSpec((1, D), lambda i: (0, 0))
    return pl.pallas_call(
        body, name="final_loss", grid=(S // tm,),
        in_specs=[row, row, vec],
        out_specs=[pl.BlockSpec((1, 128), lambda i: (0, 0)), row, row, vec],
        out_shape=[_sds((1, 128), F32), _sds((S, D), F32), _sds((S, D), BF16), _sds((1, D), F32)],
        compiler_params=_cparams(("arbitrary",)),
    )(x, t, g)


def _rope128(v, cos_t, sin_t):
    lane = lax.broadcasted_iota(jnp.int32, v.shape, 1)
    up = pltpu.roll(v, 128 - ROPE // 2, 1)
    dn = pltpu.roll(v, ROPE // 2, 1)
    return v * cos_t + jnp.where(lane < ROPE // 2, up, dn) * sin_t


def _lat_fwd(l, proj, g_qa, g_kva, cos_t, sin_t, off_cq, off_ckv, off_kpe):
    S = proj.shape[0]
    QL, KL = g_qa.shape[1], g_kva.shape[1]
    tm = _tile(S, 512, 8)
    assert off_cq % QL == 0 and off_ckv % KL == 0 and off_kpe % 128 == 0

    def body(cq_ref, ckv_ref, kpe_ref, gq_ref, gkv_ref, cos_ref, sin_ref, nq_ref, nkv_ref, rq_ref, rkv_ref, kp_ref):
        for c_ref, g_ref, n_ref, r_ref in ((cq_ref, gq_ref, nq_ref, rq_ref), (ckv_ref, gkv_ref, nkv_ref, rkv_ref)):
            cv = c_ref[...].astype(F32)
            r = lax.rsqrt(jnp.mean(cv * cv, axis=-1, keepdims=True) + EPS)
            n_ref[...] = (cv * r * g_ref[...]).astype(BF16)
            r_ref[...] = r
        kp_ref[...] = _rope128(kpe_ref[...].astype(F32), cos_ref[...], sin_ref[...]).astype(BF16)

    col = lambda w, off: pl.BlockSpec((tm, w), lambda i: (i, off // w))
    row = lambda w: pl.BlockSpec((tm, w), lambda i: (i, 0))
    vec = lambda w: pl.BlockSpec((1, w), lambda i: (0, 0))
    return pl.pallas_call(
        body, name=f"lat_fwd_{l}", grid=(S // tm,),
        in_specs=[col(QL, off_cq), col(KL, off_ckv), col(128, off_kpe), vec(QL), vec(KL), row(128), row(128)],
        out_specs=[row(QL), row(KL), row(1), row(1), row(128)],
        out_shape=[_sds((S, QL), BF16), _sds((S, KL), BF16), _sds((S, 1), F32), _sds((S, 1), F32), _sds((S, 128), BF16)],
        compiler_params=_cparams(("parallel",)),
    )(proj, proj, proj, g_qa, g_kva, cos_t, sin_t)


def _lat_bwd(l, proj, rq, rkv, g_qa, g_kva, dnq, dnkv, dkp_h, cos_t, sin_t, off_cq, off_ckv):
    S = proj.shape[0]
    QL, KL = g_qa.shape[1], g_kva.shape[1]
    H = dkp_h.shape[0]
    tm = _tile(S, 512, 8)

    def body(cq_ref, ckv_ref, rq_ref, rkv_ref, gq_ref, gkv_ref, dnq_ref, dnkv_ref, dkp_ref, cos_ref, sin_ref,
             dcq_ref, dckv_ref, dkpe_ref, dgq_ref, dgkv_ref):
        first = pl.program_id(0) == 0
        for c_ref, r_ref, g_ref, dn_ref, dc_ref, dg_ref in (
                (cq_ref, rq_ref, gq_ref, dnq_ref, dcq_ref, dgq_ref),
                (ckv_ref, rkv_ref, gkv_ref, dnkv_ref, dckv_ref, dgkv_ref)):
            rv = r_ref[...]
            xhat = c_ref[...].astype(F32) * rv
            dn = dn_ref[...]
            dxh = dn * g_ref[...]
            m = jnp.mean(dxh * xhat, axis=-1, keepdims=True)
            dc_ref[...] = (rv * (dxh - xhat * m)).astype(BF16)

            @pl.when(first)
            def _():
                dg_ref[...] = jnp.zeros_like(dg_ref)

            dg_ref[...] += jnp.sum(dn * xhat, axis=0, keepdims=True)
        dkp = dkp_ref[0]
        for h in range(1, H):
            dkp = dkp + dkp_ref[h]
        dkpe_ref[...] = _rope128(dkp, cos_ref[...], -sin_ref[...]).astype(BF16)

    col = lambda w, off: pl.BlockSpec((tm, w), lambda i: (i, off // w))
    row = lambda w: pl.BlockSpec((tm, w), lambda i: (i, 0))
    vec = lambda w: pl.BlockSpec((1, w), lambda i: (0, 0))
    return pl.pallas_call(
        body, name=f"lat_bwd_{l}", grid=(S // tm,),
        in_specs=[col(QL, off_cq), col(KL, off_ckv), row(1), row(1), vec(QL), vec(KL), row(QL), row(KL),
                  pl.BlockSpec((H, tm, 128), lambda i: (0, i, 0)), row(128), row(128)],
        out_specs=[row(QL), row(KL), row(128), vec(QL), vec(KL)],
        out_shape=[_sds((S, QL), BF16), _sds((S, KL), BF16), _sds((S, 128), BF16), _sds((1, QL), F32), _sds((1, KL), F32)],
        compiler_params=_cparams(("arbitrary",)),
    )(proj, proj, rq, rkv, g_qa, g_kva, dnq, dnkv, dkp_h, cos_t, sin_t)


def _rope_bwd_q(l, dq, cos_t, sin_t, scale):
    S, W = dq.shape
    tm = _tile(S, 256, 8)
    nh = W // 256

    def body(dq_ref, cos_ref, sin_ref, out_ref):
        cv, sv = cos_ref[...] * scale, -sin_ref[...] * scale
        for h in range(nh):
            out_ref[:, h * 256:h * 256 + 128] = (dq_ref[:, h * 256:h * 256 + 128] * scale).astype(BF16)
            out_ref[:, h * 256 + 128:(h + 1) * 256] = _rope128(dq_ref[:, h * 256 + 128:(h + 1) * 256], cv, sv).astype(BF16)

    return pl.pallas_call(
        body, name=f"rope_bwd_q_{l}", grid=(S // tm,),
        in_specs=[pl.BlockSpec((tm, W), lambda i: (i, 0)), pl.BlockSpec((tm, 128), lambda i: (i, 0)),
                  pl.BlockSpec((tm, 128), lambda i: (i, 0))],
        out_specs=pl.BlockSpec((tm, W), lambda i: (i, 0)),
        out_shape=_sds((S, W), BF16),
        compiler_params=_cparams(("parallel",)),
    )(dq, cos_t, sin_t)


def _delta(l, o, do):
    S, W = o.shape
    H = W // HEAD
    tm = _tile(S, 1024, 8)

    def body(o_ref, do_ref, d_ref):
        d_ref[...] = jnp.sum(o_ref[...].astype(F32) * do_ref[...].astype(F32), axis=-1, keepdims=True)

    blk = pl.BlockSpec((tm, HEAD), lambda h, i: (i, h))
    return pl.pallas_call(
        body, name=f"delta_{l}", grid=(H, S // tm),
        in_specs=[blk, blk],
        out_specs=pl.BlockSpec((None, tm, 1), lambda h, i: (h, i, 0)),
        out_shape=_sds((H, S, 1), F32),
        compiler_params=_cparams(("parallel", "parallel")),
    )(o, do)


_NT = (((1,), (1,)), ((), ()))
_TN = (((0,), (0,)), ((), ()))


MLA_SUB = 256


def _carry(comm, body, n_in, n_out, n_scratch, steps):
    if comm is None:
        return body, [], [], [], []
    ni, no = len(comm.inputs), len(comm.out_shapes)

    def carrying(*refs):
        a = n_in + ni
        b = a + n_out + no
        ins, cin = refs[:n_in], refs[n_in:a]
        outs, cout = refs[a:a + n_out], refs[a + n_out:b]
        scratch, csem = refs[b:b + n_scratch], refs[b + n_scratch:]
        first, middle, last = steps()

        @pl.when(first)
        def _():
            comm.start(cin, cout, csem)

        @pl.when(middle)
        def _():
            comm.mid(cin, cout, csem)

        body(*ins, *outs, *scratch)

        @pl.when(last)
        def _():
            comm.finish(cin, cout, csem)

    hbm = pl.BlockSpec(memory_space=pl.ANY)
    return carrying, [hbm] * ni, [hbm] * no, comm.out_shapes, comm.scratch


def _mla_fwd(l, q, kv, kp, comm=None):
    S = q.shape[0]
    H = q.shape[1] // 256
    tq, tk = _tile(S, 1024, 8), _tile(S, 2048, 128)
    sub = min(MLA_SUB, tq)
    nq, nk = S // tq, S // tk
    assert H >= 2

    def steps():
        h, i, k = pl.program_id(0), pl.program_id(1), pl.program_id(2)
        origin = (i == 0) & (k == 0)
        return (h == 0) & origin, (h == max(1, 5 * H // 8)) & origin, (h == H - 1) & (i == nq - 1) & (k == nk - 1)

    def body(q_ref, kn_ref, v_ref, kp_ref, o_ref, lse_ref, m_sc, acc_sc):
        ki = pl.program_id(2)

        @pl.when(ki == 0)
        def _():
            m_sc[...] = jnp.full_like(m_sc, NEG)
            acc_sc[...] = jnp.zeros_like(acc_sc)

        kc = jnp.concatenate([kn_ref[...], kp_ref[...]], axis=1)
        vx = jnp.concatenate([v_ref[...], jnp.ones((tk, 128), BF16)], axis=1)
        for r in range(tq // sub):
            rows = slice(r * sub, (r + 1) * sub)
            s = lax.dot_general(q_ref[rows, :], kc, _NT, preferred_element_type=F32)
            m_prev = m_sc[rows, :]
            m_new = jnp.maximum(m_prev, jnp.max(s, axis=-1, keepdims=True))
            alpha = jnp.exp(m_prev - m_new)
            p = jnp.exp(s - jnp.tile(m_new, (1, tk // 128)))
            acc_sc[rows, :] = (jnp.tile(alpha, (1, 2)) * acc_sc[rows, :]
                               + jnp.dot(p.astype(BF16), vx, preferred_element_type=F32))
            m_sc[rows, :] = m_new

        @pl.when(ki == nk - 1)
        def _():
            l = acc_sc[:, HEAD:]
            o_ref[...] = (acc_sc[:, :HEAD] / l).astype(BF16)
            lse_ref[...] = m_sc[:, :1] + jnp.log(l[:, :1])

    body, cin_specs, cout_specs, cout_shapes, cscratch = _carry(comm, body, 4, 2, 2, steps)
    return pl.pallas_call(
        body, name=f"mla_fwd_{l}", grid=(H, nq, nk),
        in_specs=[pl.BlockSpec((tq, 256), lambda h, i, k: (i, h)),
                  pl.BlockSpec((tk, HEAD), lambda h, i, k: (k, 2 * h)),
                  pl.BlockSpec((tk, HEAD), lambda h, i, k: (k, 2 * h + 1)),
                  pl.BlockSpec((tk, 128), lambda h, i, k: (k, 0))] + cin_specs,
        out_specs=[pl.BlockSpec((tq, HEAD), lambda h, i, k: (i, h)),
                   pl.BlockSpec((None, tq, 1), lambda h, i, k: (h, i, 0))] + cout_specs,
        out_shape=[_sds((S, H * HEAD), BF16), _sds((H, S, 1), F32)] + cout_shapes,
        scratch_shapes=[pltpu.VMEM((tq, 128), F32), pltpu.VMEM((tq, 2 * HEAD), F32)] + cscratch,
        compiler_params=_cparams(("arbitrary", "arbitrary", "arbitrary")),
    )(q, kv, kv, kp, *(comm.inputs if comm else ()))


def _mla_bwd(l, q, kv, kp, do, lse, delta, comm=None):
    S = q.shape[0]
    H = q.shape[1] // 256
    tq, tk = _tile(S, 1024, 8), _tile(S, 1024, 128)
    sub = min(MLA_SUB, tq)
    nq, nk = S // tq, S // tk

    def steps():
        h, k, i = pl.program_id(0), pl.program_id(1), pl.program_id(2)
        origin = (k == 0) & (i == 0)
        return (h == 0) & origin, (h == H // 2) & origin, (h == H - 1) & (k == nk - 1) & (i == nq - 1)

    def body(q_ref, kn_ref, v_ref, kp_ref, do_ref, lse_ref, dl_ref, dq_ref, dkv_ref, dkp_ref, dkc_sc, dv_sc):
        ki, qi = pl.program_id(1), pl.program_id(2)
        kc = jnp.concatenate([kn_ref[...], kp_ref[...]], axis=1)
        vv = v_ref[...]
        dkc, dv, dq_tiles = None, None, []
        for r in range(tq // sub):
            rows = slice(r * sub, (r + 1) * sub)
            qv, dov = q_ref[rows, :], do_ref[rows, :]
            s = lax.dot_general(qv, kc, _NT, preferred_element_type=F32)
            p = jnp.exp(s - lse_ref[rows, :])
            dv_r = lax.dot_general(p.astype(BF16), dov, _TN, preferred_element_type=F32)
            dp = lax.dot_general(dov, vv, _NT, preferred_element_type=F32)
            ds = (p * (dp - dl_ref[rows, :])).astype(BF16)
            dkc_r = lax.dot_general(ds, qv, _TN, preferred_element_type=F32)
            dq_tiles.append(jnp.dot(ds, kc, preferred_element_type=F32))
            dkc = dkc_r if dkc is None else dkc + dkc_r
            dv = dv_r if dv is None else dv + dv_r
        dq_tile = jnp.concatenate(dq_tiles, axis=0) if len(dq_tiles) > 1 else dq_tiles[0]

        @pl.when(qi == 0)
        def _():
            dkc_sc[...] = dkc
            dv_sc[...] = dv

        @pl.when(qi > 0)
        def _():
            dkc_sc[...] += dkc
            dv_sc[...] += dv

        rows = pl.ds(pl.multiple_of(qi * tq, tq), tq)

        @pl.when(ki == 0)
        def _():
            dq_ref[rows, :] = dq_tile

        @pl.when(ki > 0)
        def _():
            dq_ref[rows, :] += dq_tile

        @pl.when(qi == nq - 1)
        def _():
            dkv_ref[:, :HEAD] = dkc_sc[:, :HEAD].astype(BF16)
            dkv_ref[:, HEAD:] = dv_sc[...].astype(BF16)
            dkp_ref[...] = dkc_sc[:, HEAD:]

    body, cin_specs, cout_specs, cout_shapes, cscratch = _carry(comm, body, 7, 3, 2, steps)
    return pl.pallas_call(
        body, name=f"mla_bwd_{l}", grid=(H, nk, nq),
        in_specs=[pl.BlockSpec((tq, 256), lambda h, k, i: (i, h)),
                  pl.BlockSpec((tk, HEAD), lambda h, k, i: (k, 2 * h)),
                  pl.BlockSpec((tk, HEAD), lambda h, k, i: (k, 2 * h + 1)),
                  pl.BlockSpec((tk, 128), lambda h, k, i: (k, 0)),
                  pl.BlockSpec((tq, HEAD), lambda h, k, i: (i, h)),
                  pl.BlockSpec((None, tq, 1), lambda h, k, i: (h, i, 0)),
                  pl.BlockSpec((None, tq, 1), lambda h, k, i: (h, i, 0))] + cin_specs,
        out_specs=[pl.BlockSpec((S, 256), lambda h, k, i: (0, h)),
                   pl.BlockSpec((tk, 256), lambda h, k, i: (k, h)),
                   pl.BlockSpec((None, tk, 128), lambda h, k, i: (h, k, 0))] + cout_specs,
        out_shape=[_sds((S, H * 256), F32), _sds((S, H * 256), BF16), _sds((H, S, 128), F32)] + cout_shapes,
        scratch_shapes=[pltpu.VMEM((tk, 256), F32), pltpu.VMEM((tk, HEAD), F32)] + cscratch,
        compiler_params=_cparams(("arbitrary", "arbitrary", "arbitrary")),
    )(q, kv, kv, kp, do, lse, delta, *(comm.inputs if comm else ()))


def _na_bias_index(rows):
    j = np.arange(NA_KH)
    dy = j[None, :] - (np.arange(8)[:, None] - 4) + 3
    c = np.arange(GRID_W)
    col_start = np.clip(c - NA_KW // 2, 0, GRID_W - NA_KW)
    ok = (c[None, :] >= col_start[:, None]) & (c[None, :] < col_start[:, None] + NA_KW)
    dx = np.clip(c[None, :] - c[:, None], -(NA_KW - 1), NA_KW - 1) + (NA_KW - 1)
    dy_full = np.broadcast_to(dy[:, None, :, None], (8, GRID_W, NA_KH, GRID_W)).reshape(8, GRID_W, NA_KH * GRID_W)
    dx_full = np.broadcast_to(dx[None, :, None, :], (8, GRID_W, NA_KH, GRID_W)).reshape(8, GRID_W, NA_KH * GRID_W)
    ok_full = np.broadcast_to(ok[None, :, None, :], (8, GRID_W, NA_KH, GRID_W)).reshape(8, GRID_W, NA_KH * GRID_W)
    valid = ok_full & (dy_full >= 0) & (dy_full <= 2 * NA_KH - 2)
    return np.clip(dy_full, 0, 2 * NA_KH - 2), dx_full, valid


def _na_bias(rpb, dx_masked):
    L, H, NY, NX = rpb.shape
    nkeys = NA_KH * GRID_W

    def body(rpb_ref, dx_ref, out_ref):
        base = (pl.program_id(0) * H + pl.program_id(1)) * (NY * NX)
        dxv = dx_ref[...]
        key_row = lax.shift_right_logical(lax.broadcasted_iota(jnp.int32, (1, nkeys), 1), 6)

        def variant(o, carry):
            acc = jnp.full((GRID_W, nkeys), NEG, F32)
            for xx in range(NX):
                row = jnp.zeros((1, nkeys), F32)
                for j in range(NA_KH):
                    row = jnp.where(key_row == j, rpb_ref[base + (j - o + NA_KH - 1) * NX + xx], row)
                acc = jnp.where(dxv == xx, row, acc)
            out_ref[o] = acc
            return carry

        lax.fori_loop(0, 8, variant, 0)

    return pl.pallas_call(
        body, name="na_bias", grid=(L, H),
        in_specs=[pl.BlockSpec(memory_space=pltpu.SMEM), pl.BlockSpec((GRID_W, nkeys), lambda l, h: (0, 0))],
        out_specs=pl.BlockSpec((None, None, 8, GRID_W, nkeys), lambda l, h: (l, h, 0, 0, 0)),
        out_shape=_sds((L, H, 8, GRID_W, nkeys), F32),
        compiler_params=_cparams(("parallel", "parallel")),
    )(rpb.reshape(-1), dx_masked)


NA_RB = 4
NA_WIN = NA_RB + NA_KH
NA_SUB = 128


def _na_block_plan():
    first = [(i, 0) for i in range(NA_RB)]
    interior = [(NA_KH // 2, i) for i in range(NA_RB)]
    last = [(NA_KH // 2 + i, NA_WIN - NA_KH) for i in range(NA_RB)]
    return first, interior, last


def _na_block_bias(brow):
    L, H = brow.shape[:2]
    neg = lambda n: jnp.full((L, H, GRID_W, n * GRID_W), NEG, F32)
    kinds = []
    for plan in _na_block_plan():
        rows_ = []
        for variant, joff in plan:
            parts = [neg(joff)] if joff else []
            parts.append(brow[:, :, variant])
            if NA_WIN - NA_KH - joff:
                parts.append(neg(NA_WIN - NA_KH - joff))
            rows_.append(jnp.concatenate(parts, axis=-1))
        kinds.append(jnp.concatenate(rows_, axis=-2))
    return jnp.stack(kinds, axis=2)


def _na_unblock(dblk):
    out = [None] * 8
    for kind, plan in enumerate(_na_block_plan()):
        for i, (variant, joff) in enumerate(plan):
            piece = dblk[:, kind, i * GRID_W:(i + 1) * GRID_W, joff * GRID_W:(joff + NA_KH) * GRID_W]
            out[variant] = piece if out[variant] is None else out[variant] + piece
    return jnp.stack(out, axis=1)


def _na_block(rb, rows):
    nrb = rows // NA_RB
    ks = jnp.clip(rb * NA_RB - NA_KH // 2, 0, rows - NA_WIN)
    kind = jnp.where(rb == 0, 0, jnp.where(rb == nrb - 1, 2, 1))
    return pl.ds(pl.multiple_of(ks * GRID_W, GRID_W), NA_WIN * GRID_W), kind


def _na_fwd(l, proj, bblk, off_q, off_k, off_v, scale):
    S = proj.shape[0]
    H = bblk.shape[1]
    rows = S // GRID_W
    assert rows % NA_RB == 0 and rows >= NA_WIN
    tq, nkeys = NA_RB * GRID_W, NA_WIN * GRID_W

    def body(q_ref, k_ref, v_ref, b_ref, o_ref):
        win, kind = _na_block(pl.program_id(1), rows)
        kw = k_ref[win, :]
        vx = jnp.concatenate([v_ref[win, :], jnp.ones((nkeys, 128), BF16)], axis=1)
        for c in range(tq // NA_SUB):
            qs = slice(c * NA_SUB, (c + 1) * NA_SUB)
            s = lax.dot_general(q_ref[qs, :], kw, _NT, preferred_element_type=F32) * scale + b_ref[kind, qs, :]
            e = jnp.exp(s - jnp.max(s, axis=-1, keepdims=True))
            ov = jnp.dot(e.astype(BF16), vx, preferred_element_type=F32)
            o_ref[qs, :] = (ov[:, :HEAD] / ov[:, HEAD:]).astype(BF16)

    return pl.pallas_call(
        body, name=f"na_fwd_{l}", grid=(H, rows // NA_RB),
        in_specs=[pl.BlockSpec((tq, HEAD), lambda h, r: (r, off_q // HEAD + h)),
                  pl.BlockSpec((S, HEAD), lambda h, r: (0, off_k // HEAD + h)),
                  pl.BlockSpec((S, HEAD), lambda h, r: (0, off_v // HEAD + h)),
                  pl.BlockSpec((None, None, 3, tq, nkeys), lambda h, r: (l, h, 0, 0, 0))],
        out_specs=pl.BlockSpec((tq, HEAD), lambda h, r: (r, h)),
        out_shape=_sds((S, H * HEAD), BF16),
        compiler_params=_cparams(("parallel", "arbitrary")),
    )(proj, proj, proj, bblk)


def _na_bwd(l, proj, bblk, o, do, off_q, off_k, off_v, scale):
    S = proj.shape[0]
    H = bblk.shape[1]
    rows = S // GRID_W
    tq, nkeys = NA_RB * GRID_W, NA_WIN * GRID_W

    def body(q_ref, k_ref, v_ref, b_ref, o_ref, do_ref, dq_ref, dk_ref, dv_ref, db_ref):
        rb = pl.program_id(1)

        @pl.when(rb == 0)
        def _():
            dk_ref[...] = jnp.zeros_like(dk_ref)
            dv_ref[...] = jnp.zeros_like(dv_ref)
            db_ref[...] = jnp.zeros_like(db_ref)

        win, kind = _na_block(rb, rows)
        kw, vw = k_ref[win, :], v_ref[win, :]
        dk, dv = None, None
        for c in range(tq // NA_SUB):
            qs = slice(c * NA_SUB, (c + 1) * NA_SUB)
            qv, dov = q_ref[qs, :], do_ref[qs, :]
            s = lax.dot_general(qv, kw, _NT, preferred_element_type=F32) * scale + b_ref[kind, qs, :]
            e = jnp.exp(s - jnp.max(s, axis=-1, keepdims=True))
            p = e * (1.0 / jnp.sum(e, axis=-1, keepdims=True))
            dv_c = lax.dot_general(p.astype(BF16), dov, _TN, preferred_element_type=F32)
            dp = lax.dot_general(dov, vw, _NT, preferred_element_type=F32)
            dl = jnp.sum(dov.astype(F32) * o_ref[qs, :].astype(F32), axis=-1, keepdims=True)
            ds = p * (dp - dl)
            db_ref[kind, qs, :] += ds
            dsb = (ds * scale).astype(BF16)
            dq_ref[qs, :] = jnp.dot(dsb, kw, preferred_element_type=F32).astype(BF16)
            dk_c = lax.dot_general(dsb, qv, _TN, preferred_element_type=F32)
            dk = dk_c if dk is None else dk + dk_c
            dv = dv_c if dv is None else dv + dv_c
        dk_ref[win, :] += dk
        dv_ref[win, :] += dv

    qblk = pl.BlockSpec((tq, HEAD), lambda h, r: (r, h))
    full = pl.BlockSpec((S, HEAD), lambda h, r: (0, h))
    bias = pl.BlockSpec((None, None, 3, tq, nkeys), lambda h, r: (l, h, 0, 0, 0))
    dbias = pl.BlockSpec((None, 3, tq, nkeys), lambda h, r: (h, 0, 0, 0))
    return pl.pallas_call(
        body, name=f"na_bwd_{l}", grid=(H, rows // NA_RB),
        in_specs=[pl.BlockSpec((tq, HEAD), lambda h, r: (r, off_q // HEAD + h)),
                  pl.BlockSpec((S, HEAD), lambda h, r: (0, off_k // HEAD + h)),
                  pl.BlockSpec((S, HEAD), lambda h, r: (0, off_v // HEAD + h)),
                  bias, qblk, qblk],
        out_specs=[qblk, full, full, dbias],
        out_shape=[_sds((S, H * HEAD), BF16), _sds((S, H * HEAD), F32), _sds((S, H * HEAD), F32),
                   _sds(bblk.shape[1:], F32)],
        compiler_params=_cparams(("parallel", "arbitrary")),
    )(proj, proj, proj, bblk, o, do)


def _place():
    return lax.axis_index("x"), lax.axis_index("y"), lax.axis_index("c")


class _Comm:
    def __init__(self, inputs, out_shapes, scratch, start, mid, finish):
        self.inputs, self.out_shapes, self.scratch = list(inputs), list(out_shapes), list(scratch)
        self.start, self.mid, self.finish = start, mid, finish


def _run_comm(name, comm):
    ni, no = len(comm.inputs), len(comm.out_shapes)

    def body(*refs):
        parts = refs[:ni], refs[ni:ni + no], refs[ni + no:]
        comm.start(*parts)
        comm.mid(*parts)
        comm.finish(*parts)

    hbm = pl.BlockSpec(memory_space=pl.ANY)
    return pl.pallas_call(body, name=name, in_specs=[hbm] * ni, out_specs=[hbm] * no, out_shape=comm.out_shapes,
                          scratch_shapes=comm.scratch)(*comm.inputs)


def _full_shape(shard_shape, kind):
    A, B = shard_shape
    return {"col": (A, N_CHIPS * B), "row": (N_CHIPS * A, B), "slot": (N_CHIPS, A, B)}[kind]


def _shard_region(ref, kind, shard_shape, k, half=None):
    A, B = shard_shape
    lo, n = (0, A) if half is None else (pl.multiple_of(half * (A // 2), 16), A // 2)
    if kind == "col":
        return ref.at[pl.ds(lo, n), pl.ds(pl.multiple_of(k * B, 128), B)]
    if kind == "row":
        return ref.at[pl.ds(pl.multiple_of(k * A + lo, 16), n), :]
    return ref.at[k, pl.ds(lo, n), :]


def _gather_comm(shards, kinds):
    n = len(shards)
    shapes = [tuple(s.shape) for s in shards]
    assert all(s[0] % 32 == 0 for s in shapes)

    def copies(w, o, sems):
        send_sems, recv_sems, local_sems = sems
        x, y, c = _place()
        sibling = (x, y, 1 - c)
        chips = [(1 - x, y), (x, 1 - y), (1 - x, 1 - y)]

        def copy(k, src, dst, to):
            return pltpu.make_async_remote_copy(src_ref=src, dst_ref=dst, send_sem=send_sems.at[k],
                                                recv_sem=recv_sems.at[k], device_id=to, device_id_type=MESH)

        def region(i, cx, cy, half=None):
            return _shard_region(o[i], kinds[i], shapes[i], 2 * cx + cy, half)

        def my_half(i):
            A = shapes[i][0]
            return w[i].at[pl.ds(pl.multiple_of(c * (A // 2), 16), A // 2), :]

        pairs = [(i, j, chip) for i in range(n) for j, chip in enumerate(chips)]
        return dict(
            local=lambda: [pltpu.make_async_copy(w[i], region(i, x, y), local_sems.at[i]) for i in range(n)],
            first=lambda: [copy(6 * i + j, my_half(i), region(i, x, y, c), (*chip, c)) for i, j, chip in pairs],
            landed=lambda: [copy(6 * i + j, region(i, *chip, c), region(i, *chip, c), (*chip, c)) for i, j, chip in pairs],
            passed=lambda: [copy(6 * i + 3 + j, region(i, *chip, c), region(i, *chip, c), sibling) for i, j, chip in pairs],
            handed=lambda: [copy(6 * i + 3 + j, region(i, *chip, 1 - c), region(i, *chip, 1 - c), sibling)
                            for i, j, chip in pairs])

    def start(w, o, sems):
        cps = copies(w, o, sems)
        for cp in cps["local"]() + cps["first"]():
            cp.start()

    def mid(w, o, sems):
        cps = copies(w, o, sems)
        for arrived, onward in zip(cps["landed"](), cps["passed"]()):
            arrived.wait_recv()
            onward.start()

    def finish(w, o, sems):
        cps = copies(w, o, sems)
        for cp in cps["handed"]():
            cp.wait_recv()
        for cp in cps["first"]() + cps["passed"]():
            cp.wait_send()
        for cp in cps["local"]():
            cp.wait()

    return _Comm(shards, [_sds(_full_shape(s, k), BF16) for s, k in zip(shapes, kinds)],
                 [pltpu.SemaphoreType.DMA((6 * n,)), pltpu.SemaphoreType.DMA((6 * n,)), pltpu.SemaphoreType.DMA((n,))],
                 start, mid, finish)


def _scatter_comm(grads, kinds, shapes):
    n = len(grads)

    def copies(g, outs, sems):
        send_sems, recv_sems, local_sems = sems
        own, got = outs[0::2], outs[1::2]
        x, y, c = _place()
        chips = [(1 - x, y), (x, 1 - y), (1 - x, 1 - y)]
        local = [pltpu.make_async_copy(_shard_region(g[i], kinds[i], shapes[i], 2 * x + y), own[i], local_sems.at[i])
                 for i in range(n)]
        sends = [pltpu.make_async_remote_copy(
            src_ref=_shard_region(g[i], kinds[i], shapes[i], 2 * cx + cy), dst_ref=got[i].at[j],
            send_sem=send_sems.at[3 * i + j], recv_sem=recv_sems.at[3 * i + j],
            device_id=(cx, cy, c), device_id_type=MESH) for i in range(n) for j, (cx, cy) in enumerate(chips)]
        return local, sends

    def start(g, outs, sems):
        local, sends = copies(g, outs, sems)
        for cp in local + sends:
            cp.start()

    def mid(g, outs, sems):
        pass

    def finish(g, outs, sems):
        local, sends = copies(g, outs, sems)
        for cp in sends:
            cp.wait_recv()
        for cp in sends:
            cp.wait_send()
        for cp in local:
            cp.wait()

    out_shapes = []
    for s in shapes:
        out_shapes += [_sds(tuple(s), BF16), _sds((3,) + tuple(s), BF16)]
    return _Comm(grads, out_shapes,
                 [pltpu.SemaphoreType.DMA((3 * n,)), pltpu.SemaphoreType.DMA((3 * n,)), pltpu.SemaphoreType.DMA((n,))],
                 start, mid, finish)


def _grad_partial(name, own, got, stack, layer):
    A, W = own.shape
    tm = _tile(A, max(16, (1 << 19) // W // 16 * 16), 16)

    def body(own_ref, got_ref, stack_ref, out_ref):
        acc = own_ref[...].astype(F32)
        for j in range(3):
            acc = acc + got_ref[j].astype(F32)
        out_ref[...] = acc

    return pl.pallas_call(
        body, name=name, grid=(A // tm,),
        in_specs=[pl.BlockSpec((tm, W), lambda i: (i, 0)), pl.BlockSpec((3, tm, W), lambda i: (0, i, 0)),
                  pl.BlockSpec(memory_space=pl.ANY)],
        out_specs=pl.BlockSpec((None, tm, W), lambda i: (layer, i, 0)),
        out_shape=_sds(stack.shape, F32),
        input_output_aliases={2: 0},
        compiler_params=_cparams(("parallel",)),
    )(own, got, stack)


def _sibling_swap(parts):
    n = len(parts)

    def body(*refs):
        p, got = refs[:n], refs[n:2 * n]
        send_sems, recv_sems = refs[2 * n:]
        x, y, c = _place()
        copies = [pltpu.make_async_remote_copy(src_ref=p[i], dst_ref=got[i], send_sem=send_sems.at[i],
                                               recv_sem=recv_sems.at[i], device_id=(x, y, 1 - c), device_id_type=MESH)
                  for i in range(n)]
        for cp in copies:
            cp.start()
        for cp in copies:
            cp.wait()

    hbm = pl.BlockSpec(memory_space=pl.ANY)
    return pl.pallas_call(
        body, name="sibling_swap",
        in_specs=[hbm] * n, out_specs=[hbm] * n,
        out_shape=[_sds(p.shape, p.dtype) for p in parts],
        scratch_shapes=[pltpu.SemaphoreType.DMA((n,)), pltpu.SemaphoreType.DMA((n,))],
    )(*parts)


def _small_allreduce(vec):
    NR, W = vec.shape

    def body(v_ref, all_ref, sum_ref, send_sems, recv_sems):
        x, y, c = _place()
        me = 4 * x + 2 * y + c
        all_ref[me] = v_ref[...]
        copies = []
        for k in range(1, N_DEV):
            fx, fy, fc = (k >> 2) & 1, (k >> 1) & 1, k & 1
            peer = (x ^ fx, y ^ fy, c ^ fc)
            copies.append(pltpu.make_async_remote_copy(
                src_ref=v_ref, dst_ref=all_ref.at[me], send_sem=send_sems.at[k - 1], recv_sem=recv_sems.at[k - 1],
                device_id=peer, device_id_type=MESH))
        for cp in copies:
            cp.start()
        for cp in copies:
            cp.wait_recv()
        for cp in copies:
            cp.wait_send()
        acc = all_ref[0]
        for d in range(1, N_DEV):
            acc = acc + all_ref[d]
        sum_ref[...] = acc

    return pl.pallas_call(
        body, name="small_allreduce",
        in_specs=[pl.BlockSpec(memory_space=pltpu.VMEM)],
        out_specs=[pl.BlockSpec(memory_space=pltpu.VMEM), pl.BlockSpec(memory_space=pltpu.VMEM)],
        out_shape=[_sds((N_DEV, NR, W), F32), _sds((NR, W), F32)],
        scratch_shapes=[pltpu.SemaphoreType.DMA((N_DEV - 1,)), pltpu.SemaphoreType.DMA((N_DEV - 1,))],
    )(vec)[1]


def _adamw_math(g, w, m, v):
    m = ADAM_B1 * m + (1.0 - ADAM_B1) * g
    v = ADAM_B2 * v + (1.0 - ADAM_B2) * (g * g)
    m_hat = m / (1.0 - ADAM_B1 ** ADAM_STEP)
    v_hat = v / (1.0 - ADAM_B2 ** ADAM_STEP)
    delta = -ADAM_LR * (m_hat / (jnp.sqrt(v_hat) + ADAM_EPS) + ADAM_WD * w)
    return delta, m, v


def _adamw(name, ga, gb, w, m, v):
    rows, n = w.shape
    tm = _tile(rows, max(8, (1 << 18) // n // 8 * 8), 8)

    def body(ga_ref, gb_ref, w_ref, m_ref, v_ref, g_out, d_out, m_out, v_out):
        g = ga_ref[...] + gb_ref[...]
        delta, mn, vn = _adamw_math(g, w_ref[...], m_ref[...], v_ref[...])
        g_out[...] = g
        d_out[...] = delta
        m_out[...] = mn
        v_out[...] = vn

    blk = pl.BlockSpec((tm, n), lambda i: (i, 0))
    return pl.pallas_call(
        body, name=name, grid=(rows // tm,),
        in_specs=[blk] * 5, out_specs=[blk] * 4, out_shape=[_sds((rows, n), F32)] * 4,
        compiler_params=_cparams(("parallel",)),
    )(ga, gb, w, m, v)


def _pack(parts):
    flat = jnp.concatenate([p.reshape(-1) for p in parts])
    pad = (-flat.shape[0]) % 1024
    if pad:
        flat = jnp.concatenate([flat, jnp.zeros((pad,), flat.dtype)])
    return flat.reshape(-1, 128)


def _unpack(flat, shapes):
    flat = flat.reshape(-1)
    out, off = [], 0
    for s in shapes:
        n = int(np.prod(s))
        out.append(flat[off:off + n].reshape(s))
        off += n
    return out


def kernel(x, norm_mix, w_in, norm_qa, w_uq, norm_kva, w_ukv, rpb, w_o_mla, w_o_na, w_out, norm_mlp, w_ff1, w_ff2, norm_final, loss_target, m_norm_mix, m_w_in, m_norm_qa, m_w_uq, m_norm_kva, m_w_ukv, m_rpb, m_w_o_mla, m_w_o_na, m_w_out, m_norm_mlp, m_w_ff1, m_w_ff2, m_norm_final, v_norm_mix, v_w_in, v_norm_qa, v_w_uq, v_norm_kva, v_w_ukv, v_rpb, v_w_o_mla, v_w_o_na, v_w_out, v_norm_mlp, v_w_ff1, v_w_ff2, v_norm_final):
    wts = dict(norm_mix=norm_mix, w_in=w_in, norm_qa=norm_qa, w_uq=w_uq, norm_kva=norm_kva, w_ukv=w_ukv, rpb=rpb,
               w_o_mla=w_o_mla, w_o_na=w_o_na, w_out=w_out, norm_mlp=norm_mlp, w_ff1=w_ff1, w_ff2=w_ff2,
               norm_final=norm_final)
    mom = dict(norm_mix=m_norm_mix, w_in=m_w_in, norm_qa=m_norm_qa, w_uq=m_w_uq, norm_kva=m_norm_kva, w_ukv=m_w_ukv,
               rpb=m_rpb, w_o_mla=m_w_o_mla, w_o_na=m_w_o_na, w_out=m_w_out, norm_mlp=m_norm_mlp, w_ff1=m_w_ff1,
               w_ff2=m_w_ff2, norm_final=m_norm_final)
    var = dict(norm_mix=v_norm_mix, w_in=v_w_in, norm_qa=v_norm_qa, w_uq=v_w_uq, norm_kva=v_norm_kva, w_ukv=v_w_ukv,
               rpb=v_rpb, w_o_mla=v_w_o_mla, w_o_na=v_w_o_na, w_out=v_w_out, norm_mlp=v_norm_mlp, w_ff1=v_w_ff1,
               w_ff2=v_w_ff2, norm_final=v_norm_final)

    _, S, D = x.shape
    L = w_in.shape[0]
    QL, KL = norm_qa.shape[1], norm_kva.shape[1]
    H = w_uq.shape[2] * N_CHIPS // (HEAD + ROPE)
    NAW = w_o_na.shape[1]
    NH = NAW // HEAD
    rows = S // GRID_W
    x = x.reshape(S, D)
    target = loss_target.reshape(S, D)

    kind_of = {n: "slot" if n == "w_in" else ("row" if n in ROW_SHARDED else "col") for n in BIG}
    shape_of = {n: tuple(wts[n].shape[1:]) for n in BIG}

    def gather_of(items):
        return _gather_comm([wts[n][layer].astype(BF16) for n, layer in items], [kind_of[n] for n, _ in items])

    widths = (QL, KL, ROPE, NAW, NAW, NAW, D, D)
    starts = np.concatenate([[0], np.cumsum(widths)]).astype(int)
    order = (6, 7, 3, 4, 5, 0, 1, 2)
    nloc = w_in.shape[2]
    new_off = np.concatenate([[0], np.cumsum([widths[i] for i in order])]).astype(int)
    off_ga, off_gb, off_q, off_k, off_v, off_cq, off_ckv, off_kpe = (int(o) for o in new_off[:8])
    PW = int(new_off[-1]) + 128 - ROPE

    def prepared(full):
        pieces = []
        for i in order:
            for k in range(N_CHIPS):
                lo, hi = max(int(starts[i]), k * nloc), min(int(starts[i + 1]), (k + 1) * nloc)
                if lo < hi:
                    pieces.append(full["w_in"][k, :, lo - k * nloc:hi - k * nloc])
        full["w_in"] = jnp.concatenate(pieces + [jnp.zeros((D, 128 - ROPE), BF16)], axis=1)
        full["w_uq"] = jnp.pad(full["w_uq"].reshape(QL, H, HEAD + ROPE),
                               ((0, 0), (0, 0), (0, 256 - HEAD - ROPE))).reshape(QL, H * 256)
        return full

    pos = jnp.arange(S, dtype=F32)
    inv_freq = 1.0 / (ROPE_THETA ** (jnp.arange(0, ROPE, 2, dtype=F32) / ROPE))
    ang = pos[:, None] * inv_freq[None, :]
    cos, sin, zero = jnp.cos(ang), jnp.sin(ang), jnp.zeros((S, 128 - ROPE), F32)
    cos_t = jnp.concatenate([cos, cos, zero], axis=1)
    sin_t = jnp.concatenate([-sin, sin, zero], axis=1)

    dy_idx, dx_idx, bias_ok = _na_bias_index(rows)
    bblk = _na_block_bias(_na_bias(wts["rpb"], jnp.asarray(np.where(bias_ok[0], dx_idx[0], -1), jnp.int32)))
    mla_scale = float((HEAD + ROPE) ** -0.5)
    na_scale = float(HEAD ** -0.5)

    def rope_q_epilogue(acc, extra, outs):
        cv, sv = extra[0][...] * mla_scale, extra[1][...] * mla_scale
        for hh in range(acc.shape[1] // 256):
            outs[0][:, hh * 256:hh * 256 + 128] = (acc[:, hh * 256:hh * 256 + 128] * mla_scale).astype(BF16)
            outs[0][:, hh * 256 + 128:(hh + 1) * 256] = _rope128(acc[:, hh * 256 + 128:(hh + 1) * 256], cv, sv).astype(BF16)

    def store_f32(acc, extra, outs):
        outs[0][...] = acc

    def merge_epilogue(acc, extra, outs):
        ga, gb, ya = extra[0][...].astype(F32), extra[1][...].astype(F32), extra[2][...]
        outs[0][...] = (jax.nn.sigmoid(ga) * ya + jax.nn.sigmoid(gb) * acc).astype(BF16)
        outs[1][...] = acc

    def residual_epilogue(acc, extra, outs):
        outs[0][...] = extra[0][...] + acc

    def ff1_epilogue(acc, extra, outs):
        outs[0][...] = acc.astype(BF16)
        outs[1][...] = jnp.square(jnp.maximum(acc, 0.0)).astype(BF16)

    def dff_epilogue(acc, extra, outs):
        outs[0][...] = (acc * (2.0 * jnp.maximum(extra[0][...].astype(F32), 0.0))).astype(BF16)

    def dmerge_epilogue(acc, extra, outs):
        ga, gb = extra[0][...].astype(F32), extra[1][...].astype(F32)
        ya, yb = extra[2][...], extra[3][...]
        sa, sb = jax.nn.sigmoid(ga), jax.nn.sigmoid(gb)
        outs[0][...] = (acc * sa).astype(BF16)
        outs[1][...] = (acc * sb).astype(BF16)
        outs[2][...] = (acc * ya * sa * (1.0 - sa)).astype(BF16)
        outs[3][...] = (acc * yb * sb * (1.0 - sb)).astype(BF16)

    saved = []
    items = [(n, 0) for n in EARLY]
    w = prepared(dict(zip(EARLY, _run_comm("weight_gather_0", gather_of(items)))))
    for l in range(L):
        u, r1 = _rms_fwd(f"rms_mix_{l}", x, wts["norm_mix"][l][None])
        proj, = _mm(f"proj_{l}", u, w["w_in"], "nn", [_sds((S, PW), BF16)], tn_cap=2048, tk_cap=2048)
        nq, nkv, rq, rkv, kp = _lat_fwd(l, proj, wts["norm_qa"][l][None], wts["norm_kva"][l][None], cos_t, sin_t,
                                        off_cq, off_ckv, off_kpe)
        q, = _mm(f"q_up_{l}", nq, w["w_uq"], "nn", [_sds((S, H * 256), BF16)],
                 extras=(cos_t, sin_t), extra_specs=(_row_spec(128), _row_spec(128)), epilogue=rope_q_epilogue, tn_cap=512)
        kv, = _mm(f"kv_up_{l}", nkv, w["w_ukv"], "nn", [_sds((S, H * 256), BF16)])
        items = [(n, l) for n in LATE] + ([(n, l + 1) for n in EARLY] if l + 1 < L else [])
        o_a, lse, *gathered = _mla_fwd(l, q, kv, kp, gather_of(items))
        w.update(zip(LATE, gathered[:len(LATE)]))
        w_next = prepared(dict(zip(EARLY, gathered[len(LATE):]))) if l + 1 < L else None
        o_b = _na_fwd(l, proj, bblk, off_q, off_k, off_v, na_scale)
        y_a, = _mm(f"o_mla_{l}", o_a, w["w_o_mla"], "nn", [_sds((S, D), F32)], epilogue=store_f32, tn_cap=512)
        merged, y_b = _mm(f"o_na_merge_{l}", o_b, w["w_o_na"], "nn", [_sds((S, D), BF16), _sds((S, D), F32)],
                          extras=(proj, proj, y_a), extra_specs=(_col_spec(off_ga), _col_spec(off_gb), _tile_spec),
                          epilogue=merge_epilogue, tn_cap=512)
        x2, = _mm(f"w_out_{l}", merged, w["w_out"], "nn", [_sds((S, D), F32)],
                  extras=(x,), extra_specs=(_tile_spec,), epilogue=residual_epilogue, tk_cap=2048)
        u2, r2 = _rms_fwd(f"rms_mlp_{l}", x2, wts["norm_mlp"][l][None])
        h, a = _mm(f"ff1_{l}", u2, w["w_ff1"], "nn", [_sds((S, 4 * D), BF16), _sds((S, 4 * D), BF16)],
                   epilogue=ff1_epilogue, tk_cap=2048)
        x3, = _mm(f"ff2_{l}", a, w["w_ff2"], "nn", [_sds((S, D), F32)],
                  extras=(x2,), extra_specs=(_tile_spec,), epilogue=residual_epilogue, tk_cap=2048)
        saved.append(dict(w=w, x=x, r1=r1, u=u, proj=proj, nq=nq, nkv=nkv, rq=rq, rkv=rkv, q=q, kv=kv, kp=kp, o_a=o_a,
                          lse=lse, o_b=o_b, y_a=y_a, y_b=y_b, merged=merged, x2=x2, r2=r2, u2=u2, h=h, a=a))
        x, w = x3, w_next

    loss_lanes, dx, dxb, dg_final = _final_loss(x, target, wts["norm_final"][None])
    loss = lax.psum(loss_lanes[0, 0], ("x", "y", "c"))

    gsmall = {n: [None] * L for n in SMALL if n != "norm_final"}
    oh_dy = jnp.asarray(dy_idx[:, 0, :, None] == np.arange(2 * NA_KH - 1), F32)
    oh_dx = jnp.asarray((dx_idx[0, :, :, None] == np.arange(2 * NA_KW - 1)) & bias_ok[0, :, :, None], F32)
    scattered = {}

    def scatter_of(items, grads):
        names = [n for n, _ in items]
        return _scatter_comm(grads, [kind_of[n] for n in names], [shape_of[n] for n in names])

    def record(items, landed):
        for j, item in enumerate(items):
            scattered[item] = (landed[2 * j], landed[2 * j + 1])

    pending = []
    for l in reversed(range(L)):
        sv = saved[l]
        proj, w, g = sv["proj"], sv["w"], {}
        dh, = _mm(f"d_ff2_{l}", dxb, w["w_ff2"], "nt", [_sds((S, 4 * D), BF16)],
                  extras=(sv["h"],), extra_specs=(_tile_spec,), epilogue=dff_epilogue, tk_cap=2048)
        g["w_ff2"], = _mm(f"g_ff2_{l}", sv["a"], dxb, "tn", [_sds((4 * D, D), BF16)], tn_cap=2048)
        du2, = _mm(f"d_ff1_{l}", dh, w["w_ff1"], "nt", [_sds((S, D), F32)], epilogue=store_f32, tk_cap=2048)
        g["w_ff1"], = _mm(f"g_ff1_{l}", sv["u2"], dh, "tn", [_sds((D, 4 * D), BF16)], tn_cap=2048)
        dx2, dx2b, gsmall["norm_mlp"][l] = _rms_bwd(f"rms_mlp_bwd_{l}", sv["x2"], sv["r2"], wts["norm_mlp"][l][None], du2, dx)
        dya, dyb, dga, dgb = _mm(
            f"d_w_out_{l}", dx2b, w["w_out"], "nt", [_sds((S, D), BF16)] * 4,
            extras=(proj, proj, sv["y_a"], sv["y_b"]),
            extra_specs=(_col_spec(off_ga), _col_spec(off_gb), _tile_spec, _tile_spec),
            epilogue=dmerge_epilogue, tn_cap=512, tk_cap=2048)
        g["w_out"], = _mm(f"g_w_out_{l}", sv["merged"], dx2b, "tn", [_sds((D, D), BF16)], tn_cap=2048)
        do_a, = _mm(f"d_o_mla_{l}", dya, w["w_o_mla"], "nt", [_sds((S, H * HEAD), BF16)], tk_cap=2048)
        g["w_o_mla"], = _mm(f"g_o_mla_{l}", sv["o_a"], dya, "tn", [_sds((H * HEAD, D), BF16)], tn_cap=2048)
        do_b, = _mm(f"d_o_na_{l}", dyb, w["w_o_na"], "nt", [_sds((S, NAW), BF16)], tk_cap=2048)
        g["w_o_na"], = _mm(f"g_o_na_{l}", sv["o_b"], dyb, "tn", [_sds((NAW, D), BF16)], tn_cap=2048)
        dq_na, dk_na, dv_na, dbblk = _na_bwd(l, proj, bblk, sv["o_b"], do_b, off_q, off_k, off_v, na_scale)
        dbrow = _na_unblock(dbblk)
        tmp =jnp.einsum("hoqn,qnx->honx", dbrow, oh_dx, precision=lax.Precision.HIGHEST)
        gsmall["rpb"][l] = jnp.einsum("honx,ony->hyx", tmp, oh_dy, precision=lax.Precision.HIGHEST)
        dl = _delta(l, sv["o_a"], do_a)
        pending += [(n, l, g[n]) for n in LATE]
        items = [(n, layer) for n, layer, _ in pending]
        dq_f, dkv, dkp_h, *landed = _mla_bwd(l, sv["q"], sv["kv"], sv["kp"], do_a, sv["lse"], dl,
                                             scatter_of(items, [arr for _, _, arr in pending]))
        record(items, landed)
        dq =_rope_bwd_q(l, dq_f, cos_t, sin_t, mla_scale)
        dnq, = _mm(f"d_q_up_{l}", dq, w["w_uq"], "nt", [_sds((S, QL), F32)], epilogue=store_f32, tk_cap=2048)
        g_uq, = _mm(f"g_q_up_{l}", sv["nq"], dq, "tn", [_sds((QL, H * 256), BF16)], tn_cap=2048)
        g["w_uq"] = g_uq.reshape(QL, H, 256)[:, :, :HEAD + ROPE].reshape(QL, H * (HEAD + ROPE))
        dnkv, = _mm(f"d_kv_up_{l}", dkv, w["w_ukv"], "nt", [_sds((S, KL), F32)], epilogue=store_f32, tk_cap=2048)
        g["w_ukv"], = _mm(f"g_kv_up_{l}", sv["nkv"], dkv, "tn", [_sds((KL, H * 256), BF16)], tn_cap=2048)
        dcq, dckv, dkpe, gsmall["norm_qa"][l], gsmall["norm_kva"][l] = _lat_bwd(
            l, proj, sv["rq"], sv["rkv"], wts["norm_qa"][l][None], wts["norm_kva"][l][None], dnq, dnkv, dkp_h,
            cos_t, sin_t, off_cq, off_ckv)
        dproj = jnp.concatenate([dga, dgb, dq_na, dk_na.astype(BF16), dv_na.astype(BF16), dcq, dckv, dkpe], axis=1)
        du, = _mm(f"d_proj_{l}", dproj, w["w_in"], "nt", [_sds((S, D), F32)], epilogue=store_f32, tk_cap=2048)
        g_in, = _mm(f"g_proj_{l}", sv["u"], dproj, "tn", [_sds((D, PW), BF16)], tn_cap=2048)
        back = [None] * 8
        for pos_new, i in enumerate(order):
            back[i] = g_in[:, new_off[pos_new]:new_off[pos_new] + widths[i]]
        g_orig = jnp.concatenate(back, axis=1)
        g["w_in"] = jnp.stack([g_orig[:, k * nloc:(k + 1) * nloc] for k in range(N_CHIPS)])
        pending = [(n, l, g[n]) for n in EARLY]
        dx, dxb, gsmall["norm_mix"][l] = _rms_bwd(f"rms_mix_bwd_{l}", sv["x"], sv["r1"], wts["norm_mix"][l][None], du, dx2)
    grad_x = dx.reshape(1, S, D)
    items = [(n, layer) for n, layer, _ in pending]
    record(items, _run_comm("grad_scatter_0", scatter_of(items, [arr for _, _, arr in pending])))

    parts = []
    for n in BIG:
        stack = lax.empty((L,) + shape_of[n], F32)
        for l in range(L):
            stack = _grad_partial(f"grad_partial_{n}_{l}", *scattered[(n, l)], stack, l)
        parts.append(stack.reshape(L * shape_of[n][0], shape_of[n][1]))
    others = _sibling_swap(parts)
    part_w, other_w = dict(zip(BIG, parts)), dict(zip(BIG, others))
    two = lambda t: t.reshape(t.shape[0] * t.shape[1], t.shape[2])

    small_shapes = [wts[n].shape for n in SMALL]
    small_g = [jnp.stack([g.reshape(wts[n].shape[1:]) for g in gsmall[n]]) for n in SMALL if n != "norm_final"]
    small_g.append(dg_final.reshape(D))
    gsum = _small_allreduce(_pack(small_g))
    zeros = jnp.zeros_like(gsum)
    sg, sd, sm, svv = _adamw("adamw_small", gsum, zeros, _pack([wts[n] for n in SMALL]), _pack([mom[n] for n in SMALL]),
                             _pack([var[n] for n in SMALL]))
    res = {n: {} for n in WEIGHTS}
    for key, flat in (("g", sg), ("d", sd), ("m", sm), ("v", svv)):
        for n, arr in zip(SMALL, _unpack(flat, small_shapes)):
            res[n][key] = arr
    for n in BIG:
        shp = wts[n].shape
        outs = _adamw(f"adamw_{n}", part_w[n], other_w[n], two(wts[n]), two(mom[n]), two(var[n]))
        for key, arr in zip(("g", "d", "m", "v"), outs):
            res[n][key] = arr.reshape(shp)

    return (loss, grad_x, *[res[n]["g"] for n in WEIGHTS], *[res[n]["d"] for n in WEIGHTS],
            *[res[n]["m"] for n in WEIGHTS], *[res[n]["v"] for n in WEIGHTS])
```

```python
import functools

import numpy as np
import jax
import jax.numpy as jnp
from jax import lax
from jax.experimental import pallas as pl
from jax.experimental.pallas import tpu as pltpu

F32 = jnp.float32
BF16 = jnp.bfloat16
MESH = pl.DeviceIdType.MESH

EPS = 1e-6
ROPE_THETA = 10000.0
ROPE = 64
HEAD = 128
GRID_W = 64
NA_KH = 8
NA_KW = 16
N_CHIPS = 4
N_DEV = 8
NEG = -1e30
LOG2E = 1.4426950408889634

ADAM_LR = 0.001
ADAM_B1 = 0.9
ADAM_B2 = 0.999
ADAM_EPS = 1e-08
ADAM_WD = 0.01
ADAM_STEP = 10

VMEM_LIMIT = 56 * 1024 * 1024

BIG = ("w_in", "w_uq", "w_ukv", "w_o_mla", "w_o_na", "w_out", "w_ff1", "w_ff2")
EARLY = BIG[:3]
LATE = BIG[3:]
ROW_SHARDED = ("w_out", "w_ff2")
SMALL = ("norm_mix", "norm_qa", "norm_kva", "rpb", "norm_mlp", "norm_final")
WEIGHTS = ("norm_mix", "w_in", "norm_qa", "w_uq", "norm_kva", "w_ukv", "rpb", "w_o_mla", "w_o_na",
           "w_out", "norm_mlp", "w_ff1", "w_ff2", "norm_final")


def _cparams(sem, **kw):
    return pltpu.CompilerParams(dimension_semantics=sem, vmem_limit_bytes=VMEM_LIMIT, **kw)


def _tile(n, cap, unit=128):
    if n <= cap:
        return n
    best = None
    for t in range(unit, cap + 1, unit):
        if n % t == 0:
            best = t
    assert best is not None, (n, cap, unit)
    return best


def _sds(shape, dtype):
    return jax.ShapeDtypeStruct(shape, dtype)


_DIMS = {"nn": (((1,), (0,)), ((), ())), "nt": (((1,), (1,)), ((), ())), "tn": (((0,), (0,)), ((), ()))}


def _store_cast(acc, extra, outs):
    outs[0][...] = acc.astype(outs[0].dtype)


def _mm(name, a, b, mode, out_shapes, *, lb=None, extras=(), extra_specs=(), out_specs=None,
        epilogue=_store_cast, tm_cap=1024, tn_cap=1024, tk_cap=1024):
    bshape = b.shape[1:] if lb is not None else b.shape
    if mode == "nn":
        (M, K), (K2, N) = a.shape, bshape
    elif mode == "nt":
        (M, K), (N, K2) = a.shape, bshape
    else:
        (K, M), (K2, N) = a.shape, bshape
    assert K == K2, (name, a.shape, b.shape)
    tm, tn, tk = _tile(M, tm_cap), _tile(N, tn_cap), _tile(K, tk_cap)
    nk = K // tk
    if mode == "tn":
        a_spec = pl.BlockSpec((tk, tm), lambda i, j, k: (k, i))
    else:
        a_spec = pl.BlockSpec((tm, tk), lambda i, j, k: (i, k))
    bblk, bidx = ((tn, tk), lambda i, j, k: (j, k)) if mode == "nt" else ((tk, tn), lambda i, j, k: (k, j))
    if lb is not None:
        b_spec = pl.BlockSpec((None,) + bblk, lambda i, j, k: (lb,) + bidx(i, j, k))
    else:
        b_spec = pl.BlockSpec(bblk, bidx)
    ne, no = len(extras), len(out_shapes)
    if out_specs is None:
        out_specs = [lambda tm, tn: pl.BlockSpec((tm, tn), lambda i, j, k: (i, j))] * no
    dims = _DIMS[mode]

    def body(*refs):
        a_ref, b_ref = refs[0], refs[1]
        extra, outs, acc = refs[2:2 + ne], refs[2 + ne:2 + ne + no], refs[-1]
        k = pl.program_id(2)

        @pl.when(k == 0)
        def _():
            acc[...] = jnp.zeros_like(acc)

        acc[...] += lax.dot_general(a_ref[...], b_ref[...], dims, preferred_element_type=F32)

        @pl.when(k == nk - 1)
        def _():
            epilogue(acc[...], extra, outs)

    return pl.pallas_call(
        body, name=name, grid=(M // tm, N // tn, nk),
        in_specs=[a_spec, b_spec] + [s(tm, tn) for s in extra_specs],
        out_specs=[s(tm, tn) for s in out_specs],
        out_shape=list(out_shapes),
        scratch_shapes=[pltpu.VMEM((tm, tn), F32)],
        compiler_params=_cparams(("parallel", "parallel", "arbitrary")),
    )(a, b, *extras)


def _tile_spec(tm, tn):
    return pl.BlockSpec((tm, tn), lambda i, j, k: (i, j))


def _row_spec(width):
    return lambda tm, tn: pl.BlockSpec((tm, width), lambda i, j, k: (i, 0))


def _col_spec(off_cols):
    def make(tm, tn):
        assert off_cols % tn == 0, (off_cols, tn)
        return pl.BlockSpec((tm, tn), lambda i, j, k: (i, off_cols // tn + j))
    return make


def _rms_fwd(name, x, g):
    S, D = x.shape
    tm = _tile(S, 256, 8)

    def body(x_ref, g_ref, u_ref, r_ref):
        xv = x_ref[...]
        r = lax.rsqrt(jnp.mean(xv * xv, axis=-1, keepdims=True) + EPS)
        u_ref[...] = (xv * r * g_ref[...]).astype(BF16)
        r_ref[...] = r

    return pl.pallas_call(
        body, name=name, grid=(S // tm,),
        in_specs=[pl.BlockSpec((tm, D), lambda i: (i, 0)), pl.BlockSpec((1, D), lambda i: (0, 0))],
        out_specs=[pl.BlockSpec((tm, D), lambda i: (i, 0)), pl.BlockSpec((tm, 1), lambda i: (i, 0))],
        out_shape=[_sds((S, D), BF16), _sds((S, 1), F32)],
        compiler_params=_cparams(("parallel",)),
    )(x, g)


def _rms_bwd(name, x, r, g, du, dres):
    S, D = x.shape
    tm = _tile(S, 256, 8)

    def body(x_ref, r_ref, g_ref, du_ref, dres_ref, dx_ref, dxb_ref, dg_ref):
        rv = r_ref[...]
        xhat = x_ref[...] * rv
        duv = du_ref[...].astype(F32)
        dxh = duv * g_ref[...]
        m = jnp.mean(dxh * xhat, axis=-1, keepdims=True)
        dx = dres_ref[...] + rv * (dxh - xhat * m)
        dx_ref[...] = dx
        dxb_ref[...] = dx.astype(BF16)

        @pl.when(pl.program_id(0) == 0)
        def _():
            dg_ref[...] = jnp.zeros_like(dg_ref)

        dg_ref[...] += jnp.sum(duv * xhat, axis=0, keepdims=True)

    row = pl.BlockSpec((tm, D), lambda i: (i, 0))
    vec = pl.BlockSpec((1, D), lambda i: (0, 0))
    return pl.pallas_call(
        body, name=name, grid=(S // tm,),
        in_specs=[row, pl.BlockSpec((tm, 1), lambda i: (i, 0)), vec, row, row],
        out_specs=[row, row, vec],
        out_shape=[_sds((S, D), F32), _sds((S, D), BF16), _sds((1, D), F32)],
        compiler_params=_cparams(("arbitrary",)),
    )(x, r, g, du, dres)


def _final_loss(x, t, g):
    S, D = x.shape
    tm = _tile(S, 256, 8)

    def body(x_ref, t_ref, g_ref, loss_ref, dx_ref, dxb_ref, dg_ref):
        xv = x_ref[...]
        gv = g_ref[...]
        rv = lax.rsqrt(jnp.mean(xv * xv, axis=-1, keepdims=True) + EPS)
        xhat = xv * rv
        diff = xhat * gv - t_ref[...]
        dy = diff * (1.0 / D)
        dxh = dy * gv
        m = jnp.mean(dxh * xhat, axis=-1, keepdims=True)
        dx = rv * (dxh - xhat * m)
        dx_ref[...] = dx
        dxb_ref[...] = dx.astype(BF16)

        @pl.when(pl.program_id(0) == 0)
        def _():
            dg_ref[...] = jnp.zeros_like(dg_ref)
            loss_ref[...] = jnp.zeros_like(loss_ref)

        dg_ref[...] += jnp.sum(dy * xhat, axis=0, keepdims=True)
        per_row = jnp.mean(diff * diff, axis=-1, keepdims=True)
        loss_ref[...] += 0.5 * jnp.sum(per_row, axis=0, keepdims=True)

    row = pl.BlockSpec((tm, D), lambda i: (i, 0))
    vec = pl.BlockSpec((1, D), lambda i: (0, 0))
    return pl.pallas_call(
        body, name="final_loss", grid=(S // tm,),
        in_specs=[row, row, vec],
        out_specs=[pl.BlockSpec((1, 128), lambda i: (0, 0)), row, row, vec],
        out_shape=[_sds((1, 128), F32), _sds((S, D), F32), _sds((S, D), BF16), _sds((1, D), F32)],
        compiler_params=_cparams(("arbitrary",)),
    )(x, t, g)


def _rope128(v, cos_t, sin_t):
    lane = lax.broadcasted_iota(jnp.int32, v.shape, 1)
    up = pltpu.roll(v, 128 - ROPE // 2, 1)
    dn = pltpu.roll(v, ROPE // 2, 1)
    return v * cos_t + jnp.where(lane < ROPE // 2, up, dn) * sin_t


def _lat_fwd(l, proj, g_qa, g_kva, cos_t, sin_t, off_cq, off_ckv, off_kpe):
    S = proj.shape[0]
    QL, KL = g_qa.shape[1], g_kva.shape[1]
    tm = _tile(S, 512, 8)
    assert off_cq % QL == 0 and off_ckv % KL == 0 and off_kpe % 128 == 0

    def body(cq_ref, ckv_ref, kpe_ref, gq_ref, gkv_ref, cos_ref, sin_ref, nq_ref, nkv_ref, rq_ref, rkv_ref, kp_ref):
        for c_ref, g_ref, n_ref, r_ref in ((cq_ref, gq_ref, nq_ref, rq_ref), (ckv_ref, gkv_ref, nkv_ref, rkv_ref)):
            cv = c_ref[...].astype(F32)
            r = lax.rsqrt(jnp.mean(cv * cv, axis=-1, keepdims=True) + EPS)
            n_ref[...] = (cv * r * g_ref[...]).astype(BF16)
            r_ref[...] = r
        kp_ref[...] = _rope128(kpe_ref[...].astype(F32), cos_ref[...], sin_ref[...]).astype(BF16)

    col = lambda w, off: pl.BlockSpec((tm, w), lambda i: (i, off // w))
    row = lambda w: pl.BlockSpec((tm, w), lambda i: (i, 0))
    vec = lambda w: pl.BlockSpec((1, w), lambda i: (0, 0))
    return pl.pallas_call(
        body, name=f"lat_fwd_{l}", grid=(S // tm,),
        in_specs=[col(QL, off_cq), col(KL, off_ckv), col(128, off_kpe), vec(QL), vec(KL), row(128), row(128)],
        out_specs=[row(QL), row(KL), row(1), row(1), row(128)],
        out_shape=[_sds((S, QL), BF16), _sds((S, KL), BF16), _sds((S, 1), F32), _sds((S, 1), F32), _sds((S, 128), BF16)],
        compiler_params=_cparams(("parallel",)),
    )(proj, proj, proj, g_qa, g_kva, cos_t, sin_t)


def _lat_bwd(l, proj, rq, rkv, g_qa, g_kva, dnq, dnkv, dkp_h, cos_t, sin_t, off_cq, off_ckv):
    S = proj.shape[0]
    QL, KL = g_qa.shape[1], g_kva.shape[1]
    H = dkp_h.shape[0]
    tm = _tile(S, 512, 8)

    def body(cq_ref, ckv_ref, rq_ref, rkv_ref, gq_ref, gkv_ref, dnq_ref, dnkv_ref, dkp_ref, cos_ref, sin_ref,
             dcq_ref, dckv_ref, dkpe_ref, dgq_ref, dgkv_ref):
        first = pl.program_id(0) == 0
        for c_ref, r_ref, g_ref, dn_ref, dc_ref, dg_ref in (
                (cq_ref, rq_ref, gq_ref, dnq_ref, dcq_ref, dgq_ref),
                (ckv_ref, rkv_ref, gkv_ref, dnkv_ref, dckv_ref, dgkv_ref)):
            rv = r_ref[...]
            xhat = c_ref[...].astype(F32) * rv
            dn = dn_ref[...]
            dxh = dn * g_ref[...]
            m = jnp.mean(dxh * xhat, axis=-1, keepdims=True)
            dc_ref[...] = (rv * (dxh - xhat * m)).astype(BF16)

            @pl.when(first)
            def _():
                dg_ref[...] = jnp.zeros_like(dg_ref)

            dg_ref[...] += jnp.sum(dn * xhat, axis=0, keepdims=True)
        dkp = dkp_ref[0]
        for h in range(1, H):
            dkp = dkp + dkp_ref[h]
        dkpe_ref[...] = _rope128(dkp, cos_ref[...], -sin_ref[...]).astype(BF16)

    col = lambda w, off: pl.BlockSpec((tm, w), lambda i: (i, off // w))
    row = lambda w: pl.BlockSpec((tm, w), lambda i: (i, 0))
    vec = lambda w: pl.BlockSpec((1, w), lambda i: (0, 0))
    return pl.pallas_call(
        body, name=f"lat_bwd_{l}", grid=(S // tm,),
        in_specs=[col(QL, off_cq), col(KL, off_ckv), row(1), row(1), vec(QL), vec(KL), row(QL), row(KL),
                  pl.BlockSpec((H, tm, 128), lambda i: (0, i, 0)), row(128), row(128)],
        out_specs=[row(QL), row(KL), row(128), vec(QL), vec(KL)],
        out_shape=[_sds((S, QL), BF16), _sds((S, KL), BF16), _sds((S, 128), BF16), _sds((1, QL), F32), _sds((1, KL), F32)],
        compiler_params=_cparams(("arbitrary",)),
    )(proj, proj, rq, rkv, g_qa, g_kva, dnq, dnkv, dkp_h, cos_t, sin_t)


def _rope_bwd_q(l, dq, cos_t, sin_t, scale):
    S, W = dq.shape
    tm = _tile(S, 256, 8)
    nh = W // 256

    def body(dq_ref, cos_ref, sin_ref, out_ref):
        cv, sv = cos_ref[...] * scale, -sin_ref[...] * scale
        for h in range(nh):
            out_ref[:, h * 256:h * 256 + 128] = (dq_ref[:, h * 256:h * 256 + 128] * scale).astype(BF16)
            out_ref[:, h * 256 + 128:(h + 1) * 256] = _rope128(dq_ref[:, h * 256 + 128:(h + 1) * 256], cv, sv).astype(BF16)

    return pl.pallas_call(
        body, name=f"rope_bwd_q_{l}", grid=(S // tm,),
        in_specs=[pl.BlockSpec((tm, W), lambda i: (i, 0)), pl.BlockSpec((tm, 128), lambda i: (i, 0)),
                  pl.BlockSpec((tm, 128), lambda i: (i, 0))],
        out_specs=pl.BlockSpec((tm, W), lambda i: (i, 0)),
        out_shape=_sds((S, W), BF16),
        compiler_params=_cparams(("parallel",)),
    )(dq, cos_t, sin_t)


def _delta(l, o, do):
    S, W = o.shape
    H = W // HEAD
    tm = _tile(S, 1024, 8)

    def body(o_ref, do_ref, d_ref):
        d_ref[...] = jnp.sum(o_ref[...].astype(F32) * do_ref[...].astype(F32), axis=-1, keepdims=True)

    blk = pl.BlockSpec((tm, HEAD), lambda h, i: (i, h))
    return pl.pallas_call(
        body, name=f"delta_{l}", grid=(H, S // tm),
        in_specs=[blk, blk],
        out_specs=pl.BlockSpec((None, tm, 1), lambda h, i: (h, i, 0)),
        out_shape=_sds((H, S, 1), F32),
        compiler_params=_cparams(("parallel", "parallel")),
    )(o, do)


_NT = (((1,), (1,)), ((), ()))
_TN = (((0,), (0,)), ((), ()))


MLA_SUB = 256


def _carry(comm, body, n_in, n_out, n_scratch, steps):
    if comm is None:
        return body, [], [], [], []
    ni, no = len(comm.inputs), len(comm.out_shapes)

    def carrying(*refs):
        a = n_in + ni
        b = a + n_out + no
        ins, cin = refs[:n_in], refs[n_in:a]
        outs, cout = refs[a:a + n_out], refs[a + n_out:b]
        scratch, csem = refs[b:b + n_scratch], refs[b + n_scratch:]
        first, middle, last = steps()

        @pl.when(first)
        def _():
            comm.start(cin, cout, csem)

        @pl.when(middle)
        def _():
            comm.mid(cin, cout, csem)

        body(*ins, *outs, *scratch)

        @pl.when(last)
        def _():
            comm.finish(cin, cout, csem)

    hbm = pl.BlockSpec(memory_space=pl.ANY)
    return carrying, [hbm] * ni, [hbm] * no, comm.out_shapes, comm.scratch


def _mla_fwd(l, q, kv, kp, comm=None):
    S = q.shape[0]
    H = q.shape[1] // 256
    tq, tk = _tile(S, 1024, 8), _tile(S, 2048, 128)
    sub = min(MLA_SUB, tq)
    nq, nk = S // tq, S // tk
    assert H >= 2

    def steps():
        h, i, k = pl.program_id(0), pl.program_id(1), pl.program_id(2)
        origin = (i == 0) & (k == 0)
        return (h == 0) & origin, (h == max(1, 5 * H // 8)) & origin, (h == H - 1) & (i == nq - 1) & (k == nk - 1)

    def body(q_ref, kn_ref, v_ref, kp_ref, o_ref, lse_ref, m_sc, acc_sc):
        ki = pl.program_id(2)

        @pl.when(ki == 0)
        def _():
            m_sc[...] = jnp.full_like(m_sc, NEG)
            acc_sc[...] = jnp.zeros_like(acc_sc)

        kc = jnp.concatenate([kn_ref[...], kp_ref[...]], axis=1)
        vx = jnp.concatenate([v_ref[...], jnp.ones((tk, 128), BF16)], axis=1)
        for r in range(tq // sub):
            rows = slice(r * sub, (r + 1) * sub)
            s = lax.dot_general(q_ref[rows, :], kc, _NT, preferred_element_type=F32)
            m_prev = m_sc[rows, :]
            m_new = jnp.maximum(m_prev, jnp.max(s, axis=-1, keepdims=True))
            alpha = jnp.exp(m_prev - m_new)
            p = jnp.exp(s - jnp.tile(m_new, (1, tk // 128)))
            acc_sc[rows, :] = (jnp.tile(alpha, (1, 2)) * acc_sc[rows, :]
                               + jnp.dot(p.astype(BF16), vx, preferred_element_type=F32))
            m_sc[rows, :] = m_new

        @pl.when(ki == nk - 1)
        def _():
            l = acc_sc[:, HEAD:]
            o_ref[...] = (acc_sc[:, :HEAD] / l).astype(BF16)
            lse_ref[...] = m_sc[:, :1] + jnp.log(l[:, :1])

    body, cin_specs, cout_specs, cout_shapes, cscratch = _carry(comm, body, 4, 2, 2, steps)
    return pl.pallas_call(
        body, name=f"mla_fwd_{l}", grid=(H, nq, nk),
        in_specs=[pl.BlockSpec((tq, 256), lambda h, i, k: (i, h)),
                  pl.BlockSpec((tk, HEAD), lambda h, i, k: (k, 2 * h)),
                  pl.BlockSpec((tk, HEAD), lambda h, i, k: (k, 2 * h + 1)),
                  pl.BlockSpec((tk, 128), lambda h, i, k: (k, 0))] + cin_specs,
        out_specs=[pl.BlockSpec((tq, HEAD), lambda h, i, k: (i, h)),
                   pl.BlockSpec((None, tq, 1), lambda h, i, k: (h, i, 0))] + cout_specs,
        out_shape=[_sds((S, H * HEAD), BF16), _sds((H, S, 1), F32)] + cout_shapes,
        scratch_shapes=[pltpu.VMEM((tq, 128), F32), pltpu.VMEM((tq, 2 * HEAD), F32)] + cscratch,
        compiler_params=_cparams(("arbitrary", "arbitrary", "arbitrary")),
    )(q, kv, kv, kp, *(comm.inputs if comm else ()))


def _mla_bwd(l, q, kv, kp, do, lse, delta, comm=None):
    S = q.shape[0]
    H = q.shape[1] // 256
    tq, tk = _tile(S, 1024, 8), _tile(S, 2048, 128)
    sub = min(MLA_SUB, tq)
    nq, nk = S // tq, S // tk

    def steps():
        h, k, i = pl.program_id(0), pl.program_id(1), pl.program_id(2)
        origin = (k == 0) & (i == 0)
        return (h == 0) & origin, (h == H // 2) & origin, (h == H - 1) & (k == nk - 1) & (i == nq - 1)

    def body(q_ref, kn_ref, v_ref, kp_ref, do_ref, lse_ref, dl_ref, dq_ref, dkv_ref, dkp_ref, dkc_sc, dv_sc):
        ki, qi = pl.program_id(1), pl.program_id(2)
        kc = jnp.concatenate([kn_ref[...], kp_ref[...]], axis=1)
        vv = v_ref[...]
        dkc, dv, dq_tiles = None, None, []
        for r in range(tq // sub):
            rows = slice(r * sub, (r + 1) * sub)
            qv, dov = q_ref[rows, :], do_ref[rows, :]
            s = lax.dot_general(qv, kc, _NT, preferred_element_type=F32)
            p = jnp.exp(s - lse_ref[rows, :])
            dv_r = lax.dot_general(p.astype(BF16), dov, _TN, preferred_element_type=F32)
            dp = lax.dot_general(dov, vv, _NT, preferred_element_type=F32)
            ds = (p * (dp - dl_ref[rows, :])).astype(BF16)
            dkc_r = lax.dot_general(ds, qv, _TN, preferred_element_type=F32)
            dq_tiles.append(jnp.dot(ds, kc, preferred_element_type=F32))
            dkc = dkc_r if dkc is None else dkc + dkc_r
            dv = dv_r if dv is None else dv + dv_r
        dq_tile = jnp.concatenate(dq_tiles, axis=0) if len(dq_tiles) > 1 else dq_tiles[0]

        @pl.when(qi == 0)
        def _():
            dkc_sc[...] = dkc
            dv_sc[...] = dv

        @pl.when(qi > 0)
        def _():
            dkc_sc[...] += dkc
            dv_sc[...] += dv

        rows = pl.ds(pl.multiple_of(qi * tq, tq), tq)

        @pl.when(ki == 0)
        def _():
            dq_ref[rows, :] = dq_tile

        @pl.when(ki > 0)
        def _():
            dq_ref[rows, :] += dq_tile

        @pl.when(qi == nq - 1)
        def _():
            dkv_ref[:, :HEAD] = dkc_sc[:, :HEAD].astype(BF16)
            dkv_ref[:, HEAD:] = dv_sc[...].astype(BF16)
            dkp_ref[...] = dkc_sc[:, HEAD:]

    body, cin_specs, cout_specs, cout_shapes, cscratch = _carry(comm, body, 7, 3, 2, steps)
    return pl.pallas_call(
        body, name=f"mla_bwd_{l}", grid=(H, nk, nq),
        in_specs=[pl.BlockSpec((tq, 256), lambda h, k, i: (i, h)),
                  pl.BlockSpec((tk, HEAD), lambda h, k, i: (k, 2 * h)),
                  pl.BlockSpec((tk, HEAD), lambda h, k, i: (k, 2 * h + 1)),
                  pl.BlockSpec((tk, 128), lambda h, k, i: (k, 0)),
                  pl.BlockSpec((tq, HEAD), lambda h, k, i: (i, h)),
                  pl.BlockSpec((None, tq, 1), lambda h, k, i: (h, i, 0)),
                  pl.BlockSpec((None, tq, 1), lambda h, k, i: (h, i, 0))] + cin_specs,
        out_specs=[pl.BlockSpec((S, 256), lambda h, k, i: (0, h)),
                   pl.BlockSpec((tk, 256), lambda h, k, i: (k, h)),
                   pl.BlockSpec((None, tk, 128), lambda h, k, i: (h, k, 0))] + cout_specs,
        out_shape=[_sds((S, H * 256), F32), _sds((S, H * 256), BF16), _sds((H, S, 128), F32)] + cout_shapes,
        scratch_shapes=[pltpu.VMEM((tk, 256), F32), pltpu.VMEM((tk, HEAD), F32)] + cscratch,
        compiler_params=_cparams(("arbitrary", "arbitrary", "arbitrary")),
    )(q, kv, kv, kp, do, lse, delta, *(comm.inputs if comm else ()))


def _na_bias_index(rows):
    j = np.arange(NA_KH)
    dy = j[None, :] - (np.arange(8)[:, None] - 4) + 3
    c = np.arange(GRID_W)
    col_start = np.clip(c - NA_KW // 2, 0, GRID_W - NA_KW)
    ok = (c[None, :] >= col_start[:, None]) & (c[None, :] < col_start[:, None] + NA_KW)
    dx = np.clip(c[None, :] - c[:, None], -(NA_KW - 1), NA_KW - 1) + (NA_KW - 1)
    dy_full = np.broadcast_to(dy[:, None, :, None], (8, GRID_W, NA_KH, GRID_W)).reshape(8, GRID_W, NA_KH * GRID_W)
    dx_full = np.broadcast_to(dx[None, :, None, :], (8, GRID_W, NA_KH, GRID_W)).reshape(8, GRID_W, NA_KH * GRID_W)
    ok_full = np.broadcast_to(ok[None, :, None, :], (8, GRID_W, NA_KH, GRID_W)).reshape(8, GRID_W, NA_KH * GRID_W)
    valid = ok_full & (dy_full >= 0) & (dy_full <= 2 * NA_KH - 2)
    return np.clip(dy_full, 0, 2 * NA_KH - 2), dx_full, valid


def _na_bias(rpb, dx_masked):
    L, H, NY, NX = rpb.shape
    nkeys = NA_KH * GRID_W

    def body(rpb_ref, dx_ref, out_ref):
        base = (pl.program_id(0) * H + pl.program_id(1)) * (NY * NX)
        dxv = dx_ref[...]
        key_row = lax.shift_right_logical(lax.broadcasted_iota(jnp.int32, (1, nkeys), 1), 6)

        def variant(o, carry):
            acc = jnp.full((GRID_W, nkeys), NEG, F32)
            for xx in range(NX):
                row = jnp.zeros((1, nkeys), F32)
                for j in range(NA_KH):
                    row = jnp.where(key_row == j, rpb_ref[base + (j - o + NA_KH - 1) * NX + xx], row)
                acc = jnp.where(dxv == xx, row, acc)
            out_ref[o] = acc
            return carry

        lax.fori_loop(0, 8, variant, 0)

    return pl.pallas_call(
        body, name="na_bias", grid=(L, H),
        in_specs=[pl.BlockSpec(memory_space=pltpu.SMEM), pl.BlockSpec((GRID_W, nkeys), lambda l, h: (0, 0))],
        out_specs=pl.BlockSpec((None, None, 8, GRID_W, nkeys), lambda l, h: (l, h, 0, 0, 0)),
        out_shape=_sds((L, H, 8, GRID_W, nkeys), F32),
        compiler_params=_cparams(("parallel", "parallel")),
    )(rpb.reshape(-1), dx_masked)


NA_RB = 4
NA_WIN = NA_RB + NA_KH
NA_SUB = 128


def _na_block_plan():
    first = [(i, 0) for i in range(NA_RB)]
    interior = [(NA_KH // 2, i) for i in range(NA_RB)]
    last = [(NA_KH // 2 + i, NA_WIN - NA_KH) for i in range(NA_RB)]
    return first, interior, last


def _na_block_bias(brow):
    L, H = brow.shape[:2]
    neg = lambda n: jnp.full((L, H, GRID_W, n * GRID_W), NEG, F32)
    kinds = []
    for plan in _na_block_plan():
        rows_ = []
        for variant, joff in plan:
            parts = [neg(joff)] if joff else []
            parts.append(brow[:, :, variant])
            if NA_WIN - NA_KH - joff:
                parts.append(neg(NA_WIN - NA_KH - joff))
            rows_.append(jnp.concatenate(parts, axis=-1))
        kinds.append(jnp.concatenate(rows_, axis=-2))
    return jnp.stack(kinds, axis=2)


def _na_unblock(dblk):
    out = [None] * 8
    for kind, plan in enumerate(_na_block_plan()):
        for i, (variant, joff) in enumerate(plan):
            piece = dblk[:, kind, i * GRID_W:(i + 1) * GRID_W, joff * GRID_W:(joff + NA_KH) * GRID_W]
            out[variant] = piece if out[variant] is None else out[variant] + piece
    return jnp.stack(out, axis=1)


def _na_block(rb, rows):
    nrb = rows // NA_RB
    ks = jnp.clip(rb * NA_RB - NA_KH // 2, 0, rows - NA_WIN)
    kind = jnp.where(rb == 0, 0, jnp.where(rb == nrb - 1, 2, 1))
    return pl.ds(pl.multiple_of(ks * GRID_W, GRID_W), NA_WIN * GRID_W), kind


def _na_fwd(l, proj, bblk, off_q, off_k, off_v, scale):
    S = proj.shape[0]
    H = bblk.shape[1]
    rows = S // GRID_W
    assert rows % NA_RB == 0 and rows >= NA_WIN
    tq, nkeys = NA_RB * GRID_W, NA_WIN * GRID_W

    def body(q_ref, k_ref, v_ref, b_ref, o_ref):
        win, kind = _na_block(pl.program_id(1), rows)
        kw = k_ref[win, :]
        vx = jnp.concatenate([v_ref[win, :], jnp.ones((nkeys, 128), BF16)], axis=1)
        for c in range(tq // NA_SUB):
            qs = slice(c * NA_SUB, (c + 1) * NA_SUB)
            s = lax.dot_general(q_ref[qs, :], kw, _NT, preferred_element_type=F32) * scale + b_ref[kind, qs, :]
            e = jnp.exp(s - jnp.max(s, axis=-1, keepdims=True))
            ov = jnp.dot(e.astype(BF16), vx, preferred_element_type=F32)
            o_ref[qs, :] = (ov[:, :HEAD] / ov[:, HEAD:]).astype(BF16)

    return pl.pallas_call(
        body, name=f"na_fwd_{l}", grid=(H, rows // NA_RB),
        in_specs=[pl.BlockSpec((tq, HEAD), lambda h, r: (r, off_q // HEAD + h)),
                  pl.BlockSpec((S, HEAD), lambda h, r: (0, off_k // HEAD + h)),
                  pl.BlockSpec((S, HEAD), lambda h, r: (0, off_v // HEAD + h)),
                  pl.BlockSpec((None, None, 3, tq, nkeys), lambda h, r: (l, h, 0, 0, 0))],
        out_specs=pl.BlockSpec((tq, HEAD), lambda h, r: (r, h)),
        out_shape=_sds((S, H * HEAD), BF16),
        compiler_params=_cparams(("parallel", "arbitrary")),
    )(proj, proj, proj, bblk)


def _na_bwd(l, proj, bblk, o, do, off_q, off_k, off_v, scale):
    S = proj.shape[0]
    H = bblk.shape[1]
    rows = S // GRID_W
    tq, nkeys = NA_RB * GRID_W, NA_WIN * GRID_W

    def body(q_ref, k_ref, v_ref, b_ref, o_ref, do_ref, dq_ref, dk_ref, dv_ref, db_ref):
        rb = pl.program_id(1)

        @pl.when(rb == 0)
        def _():
            dk_ref[...] = jnp.zeros_like(dk_ref)
            dv_ref[...] = jnp.zeros_like(dv_ref)
            db_ref[...] = jnp.zeros_like(db_ref)

        win, kind = _na_block(rb, rows)
        kw, vw = k_ref[win, :], v_ref[win, :]
        dk, dv = None, None
        for c in range(tq // NA_SUB):
            qs = slice(c * NA_SUB, (c + 1) * NA_SUB)
            qv, dov = q_ref[qs, :], do_ref[qs, :]
            s = lax.dot_general(qv, kw, _NT, preferred_element_type=F32) * scale + b_ref[kind, qs, :]
            e = jnp.exp(s - jnp.max(s, axis=-1, keepdims=True))
            p = e * (1.0 / jnp.sum(e, axis=-1, keepdims=True))
            dv_c = lax.dot_general(p.astype(BF16), dov, _TN, preferred_element_type=F32)
            dp = lax.dot_general(dov, vw, _NT, preferred_element_type=F32)
            dl = jnp.sum(dov.astype(F32) * o_ref[qs, :].astype(F32), axis=-1, keepdims=True)
            ds = p * (dp - dl)
            db_ref[kind, qs, :] += ds
            dsb = (ds * scale).astype(BF16)
            dq_ref[qs, :] = jnp.dot(dsb, kw, preferred_element_type=F32).astype(BF16)
            dk_c = lax.dot_general(dsb, qv, _TN, preferred_element_type=F32)
            dk = dk_c if dk is None else dk + dk_c
            dv = dv_c if dv is None else dv + dv_c
        dk_ref[win, :] += dk
        dv_ref[win, :] += dv

    qblk = pl.BlockSpec((tq, HEAD), lambda h, r: (r, h))
    full = pl.BlockSpec((S, HEAD), lambda h, r: (0, h))
    bias = pl.BlockSpec((None, None, 3, tq, nkeys), lambda h, r: (l, h, 0, 0, 0))
    dbias = pl.BlockSpec((None, 3, tq, nkeys), lambda h, r: (h, 0, 0, 0))
    return pl.pallas_call(
        body, name=f"na_bwd_{l}", grid=(H, rows // NA_RB),
        in_specs=[pl.BlockSpec((tq, HEAD), lambda h, r: (r, off_q // HEAD + h)),
                  pl.BlockSpec((S, HEAD), lambda h, r: (0, off_k // HEAD + h)),
                  pl.BlockSpec((S, HEAD), lambda h, r: (0, off_v // HEAD + h)),
                  bias, qblk, qblk],
        out_specs=[qblk, full, full, dbias],
        out_shape=[_sds((S, H * HEAD), BF16), _sds((S, H * HEAD), F32), _sds((S, H * HEAD), F32),
                   _sds(bblk.shape[1:], F32)],
        compiler_params=_cparams(("parallel", "arbitrary")),
    )(proj, proj, proj, bblk, o, do)


def _place():
    return lax.axis_index("x"), lax.axis_index("y"), lax.axis_index("c")


class _Comm:
    def __init__(self, inputs, out_shapes, scratch, start, mid, finish):
        self.inputs, self.out_shapes, self.scratch = list(inputs), list(out_shapes), list(scratch)
        self.start, self.mid, self.finish = start, mid, finish


def _run_comm(name, comm):
    ni, no = len(comm.inputs), len(comm.out_shapes)

    def body(*refs):
        parts = refs[:ni], refs[ni:ni + no], refs[ni + no:]
        comm.start(*parts)
        comm.mid(*parts)
        comm.finish(*parts)

    hbm = pl.BlockSpec(memory_space=pl.ANY)
    return pl.pallas_call(body, name=name, in_specs=[hbm] * ni, out_specs=[hbm] * no, out_shape=comm.out_shapes,
                          scratch_shapes=comm.scratch)(*comm.inputs)


def _full_shape(shard_shape, kind):
    A, B = shard_shape
    return {"col": (A, N_CHIPS * B), "row": (N_CHIPS * A, B), "slot": (N_CHIPS, A, B)}[kind]


def _shard_region(ref, kind, shard_shape, k, half=None):
    A, B = shard_shape
    lo, n = (0, A) if half is None else (pl.multiple_of(half * (A // 2), 16), A // 2)
    if kind == "col":
        return ref.at[pl.ds(lo, n), pl.ds(pl.multiple_of(k * B, 128), B)]
    if kind == "row":
        return ref.at[pl.ds(pl.multiple_of(k * A + lo, 16), n), :]
    return ref.at[k, pl.ds(lo, n), :]


def _gather_comm(shards, kinds):
    n = len(shards)
    shapes = [tuple(s.shape) for s in shards]
    assert all(s[0] % 32 == 0 for s in shapes)

    def copies(w, o, sems):
        send_sems, recv_sems, local_sems = sems
        x, y, c = _place()
        sibling = (x, y, 1 - c)
        chips = [(1 - x, y), (x, 1 - y), (1 - x, 1 - y)]

        def copy(k, src, dst, to):
            return pltpu.make_async_remote_copy(src_ref=src, dst_ref=dst, send_sem=send_sems.at[k],
                                                recv_sem=recv_sems.at[k], device_id=to, device_id_type=MESH)

        def region(i, cx, cy, half=None):
            return _shard_region(o[i], kinds[i], shapes[i], 2 * cx + cy, half)

        def my_half(i):
            A = shapes[i][0]
            return w[i].at[pl.ds(pl.multiple_of(c * (A // 2), 16), A // 2), :]

        pairs = [(i, j, chip) for i in range(n) for j, chip in enumerate(chips)]
        return dict(
            local=lambda: [pltpu.make_async_copy(w[i], region(i, x, y), local_sems.at[i]) for i in range(n)],
            first=lambda: [copy(6 * i + j, my_half(i), region(i, x, y, c), (*chip, c)) for i, j, chip in pairs],
            landed=lambda: [copy(6 * i + j, region(i, *chip, c), region(i, *chip, c), (*chip, c)) for i, j, chip in pairs],
            passed=lambda: [copy(6 * i + 3 + j, region(i, *chip, c), region(i, *chip, c), sibling) for i, j, chip in pairs],
            handed=lambda: [copy(6 * i + 3 + j, region(i, *chip, 1 - c), region(i, *chip, 1 - c), sibling)
                            for i, j, chip in pairs])

    def start(w, o, sems):
        cps = copies(w, o, sems)
        for cp in cps["local"]() + cps["first"]():
            cp.start()

    def mid(w, o, sems):
        cps = copies(w, o, sems)
        for arrived, onward in zip(cps["landed"](), cps["passed"]()):
            arrived.wait_recv()
            onward.start()

    def finish(w, o, sems):
        cps = copies(w, o, sems)
        for cp in cps["handed"]():
            cp.wait_recv()
        for cp in cps["first"]() + cps["passed"]():
            cp.wait_send()
        for cp in cps["local"]():
            cp.wait()

    return _Comm(shards, [_sds(_full_shape(s, k), BF16) for s, k in zip(shapes, kinds)],
                 [pltpu.SemaphoreType.DMA((6 * n,)), pltpu.SemaphoreType.DMA((6 * n,)), pltpu.SemaphoreType.DMA((n,))],
                 start, mid, finish)


def _scatter_comm(grads, kinds, shapes):
    n = len(grads)

    def copies(g, outs, sems):
        send_sems, recv_sems, local_sems = sems
        own, got = outs[0::2], outs[1::2]
        x, y, c = _place()
        chips = [(1 - x, y), (x, 1 - y), (1 - x, 1 - y)]
        local = [pltpu.make_async_copy(_shard_region(g[i], kinds[i], shapes[i], 2 * x + y), own[i], local_sems.at[i])
                 for i in range(n)]
        sends = [pltpu.make_async_remote_copy(
            src_ref=_shard_region(g[i], kinds[i], shapes[i], 2 * cx + cy), dst_ref=got[i].at[j],
            send_sem=send_sems.at[3 * i + j], recv_sem=recv_sems.at[3 * i + j],
            device_id=(cx, cy, c), device_id_type=MESH) for i in range(n) for j, (cx, cy) in enumerate(chips)]
        return local, sends

    def start(g, outs, sems):
        local, sends = copies(g, outs, sems)
        for cp in local + sends:
            cp.start()

    def mid(g, outs, sems):
        pass

    def finish(g, outs, sems):
        local, sends = copies(g, outs, sems)
        for cp in sends:
            cp.wait_recv()
        for cp in sends:
            cp.wait_send()
        for cp in local:
            cp.wait()

    out_shapes = []
    for s in shapes:
        out_shapes += [_sds(tuple(s), BF16), _sds((3,) + tuple(s), BF16)]
    return _Comm(grads, out_shapes,
                 [pltpu.SemaphoreType.DMA((3 * n,)), pltpu.SemaphoreType.DMA((3 * n,)), pltpu.SemaphoreType.DMA((n,))],
                 start, mid, finish)


def _grad_partial(name, own, got, stack, layer):
    A, W = own.shape
    tm = _tile(A, max(16, (1 << 19) // W // 16 * 16), 16)

    def body(own_ref, got_ref, stack_ref, out_ref):
        acc = own_ref[...].astype(F32)
        for j in range(3):
            acc = acc + got_ref[j].astype(F32)
        out_ref[...] = acc

    return pl.pallas_call(
        body, name=name, grid=(A // tm,),
        in_specs=[pl.BlockSpec((tm, W), lambda i: (i, 0)), pl.BlockSpec((3, tm, W), lambda i: (0, i, 0)),
                  pl.BlockSpec(memory_space=pl.ANY)],
        out_specs=pl.BlockSpec((None, tm, W), lambda i: (layer, i, 0)),
        out_shape=_sds(stack.shape, F32),
        input_output_aliases={2: 0},
        compiler_params=_cparams(("parallel",)),
    )(own, got, stack)


def _sibling_swap(parts):
    n = len(parts)

    def body(*refs):
        p, got = refs[:n], refs[n:2 * n]
        send_sems, recv_sems = refs[2 * n:]
        x, y, c = _place()
        copies = [pltpu.make_async_remote_copy(src_ref=p[i], dst_ref=got[i], send_sem=send_sems.at[i],
                                               recv_sem=recv_sems.at[i], device_id=(x, y, 1 - c), device_id_type=MESH)
                  for i in range(n)]
        for cp in copies:
            cp.start()
        for cp in copies:
            cp.wait()

    hbm = pl.BlockSpec(memory_space=pl.ANY)
    return pl.pallas_call(
        body, name="sibling_swap",
        in_specs=[hbm] * n, out_specs=[hbm] * n,
        out_shape=[_sds(p.shape, p.dtype) for p in parts],
        scratch_shapes=[pltpu.SemaphoreType.DMA((n,)), pltpu.SemaphoreType.DMA((n,))],
    )(*parts)


def _small_allreduce(vec):
    NR, W = vec.shape

    def body(v_ref, all_ref, sum_ref, send_sems, recv_sems):
        x, y, c = _place()
        me = 4 * x + 2 * y + c
        all_ref[me] = v_ref[...]
        copies = []
        for k in range(1, N_DEV):
            fx, fy, fc = (k >> 2) & 1, (k >> 1) & 1, k & 1
            peer = (x ^ fx, y ^ fy, c ^ fc)
            copies.append(pltpu.make_async_remote_copy(
                src_ref=v_ref, dst_ref=all_ref.at[me], send_sem=send_sems.at[k - 1], recv_sem=recv_sems.at[k - 1],
                device_id=peer, device_id_type=MESH))
        for cp in copies:
            cp.start()
        for cp in copies:
            cp.wait_recv()
        for cp in copies:
            cp.wait_send()
        acc = all_ref[0]
        for d in range(1, N_DEV):
            acc = acc + all_ref[d]
        sum_ref[...] = acc

    return pl.pallas_call(
        body, name="small_allreduce",
        in_specs=[pl.BlockSpec(memory_space=pltpu.VMEM)],
        out_specs=[pl.BlockSpec(memory_space=pltpu.VMEM), pl.BlockSpec(memory_space=pltpu.VMEM)],
        out_shape=[_sds((N_DEV, NR, W), F32), _sds((NR, W), F32)],
        scratch_shapes=[pltpu.SemaphoreType.DMA((N_DEV - 1,)), pltpu.SemaphoreType.DMA((N_DEV - 1,))],
    )(vec)[1]


def _adamw_math(g, w, m, v):
    m = ADAM_B1 * m + (1.0 - ADAM_B1) * g
    v = ADAM_B2 * v + (1.0 - ADAM_B2) * (g * g)
    m_hat = m / (1.0 - ADAM_B1 ** ADAM_STEP)
    v_hat = v / (1.0 - ADAM_B2 ** ADAM_STEP)
    delta = -ADAM_LR * (m_hat / (jnp.sqrt(v_hat) + ADAM_EPS) + ADAM_WD * w)
    return delta, m, v


def _adamw(name, ga, gb, w, m, v):
    rows, n = w.shape
    tm = _tile(rows, max(8, (1 << 18) // n // 8 * 8), 8)

    def body(ga_ref, gb_ref, w_ref, m_ref, v_ref, g_out, d_out, m_out, v_out):
        g = ga_ref[...] + gb_ref[...]
        delta, mn, vn = _adamw_math(g, w_ref[...], m_ref[...], v_ref[...])
        g_out[...] = g
        d_out[...] = delta
        m_out[...] = mn
        v_out[...] = vn

    blk = pl.BlockSpec((tm, n), lambda i: (i, 0))
    return pl.pallas_call(
        body, name=name, grid=(rows // tm,),
        in_specs=[blk] * 5, out_specs=[blk] * 4, out_shape=[_sds((rows, n), F32)] * 4,
        compiler_params=_cparams(("parallel",)),
    )(ga, gb, w, m, v)


def _pack(parts):
    flat = jnp.concatenate([p.reshape(-1) for p in parts])
    pad = (-flat.shape[0]) % 1024
    if pad:
        flat = jnp.concatenate([flat, jnp.zeros((pad,), flat.dtype)])
    return flat.reshape(-1, 128)


def _unpack(flat, shapes):
    flat = flat.reshape(-1)
    out, off = [], 0
    for s in shapes:
        n = int(np.prod(s))
        out.append(flat[off:off + n].reshape(s))
        off += n
    return out


def kernel(x, norm_mix, w_in, norm_qa, w_uq, norm_kva, w_ukv, rpb, w_o_mla, w_o_na, w_out, norm_mlp, w_ff1, w_ff2, norm_final, loss_target, m_norm_mix, m_w_in, m_norm_qa, m_w_uq, m_norm_kva, m_w_ukv, m_rpb, m_w_o_mla, m_w_o_na, m_w_out, m_norm_mlp, m_w_ff1, m_w_ff2, m_norm_final, v_norm_mix, v_w_in, v_norm_qa, v_w_uq, v_norm_kva, v_w_ukv, v_rpb, v_w_o_mla, v_w_o_na, v_w_out, v_norm_mlp, v_w_ff1, v_w_ff2, v_norm_final):
    wts = dict(norm_mix=norm_mix, w_in=w_in, norm_qa=norm_qa, w_uq=w_uq, norm_kva=norm_kva, w_ukv=w_ukv, rpb=rpb,
               w_o_mla=w_o_mla, w_o_na=w_o_na, w_out=w_out, norm_mlp=norm_mlp, w_ff1=w_ff1, w_ff2=w_ff2,
               norm_final=norm_final)
    mom = dict(norm_mix=m_norm_mix, w_in=m_w_in, norm_qa=m_norm_qa, w_uq=m_w_uq, norm_kva=m_norm_kva, w_ukv=m_w_ukv,
               rpb=m_rpb, w_o_mla=m_w_o_mla, w_o_na=m_w_o_na, w_out=m_w_out, norm_mlp=m_norm_mlp, w_ff1=m_w_ff1,
               w_ff2=m_w_ff2, norm_final=m_norm_final)
    var = dict(norm_mix=v_norm_mix, w_in=v_w_in, norm_qa=v_norm_qa, w_uq=v_w_uq, norm_kva=v_norm_kva, w_ukv=v_w_ukv,
               rpb=v_rpb, w_o_mla=v_w_o_mla, w_o_na=v_w_o_na, w_out=v_w_out, norm_mlp=v_norm_mlp, w_ff1=v_w_ff1,
               w_ff2=v_w_ff2, norm_final=v_norm_final)

    _, S, D = x.shape
    L = w_in.shape[0]
    QL, KL = norm_qa.shape[1], norm_kva.shape[1]
    H = w_uq.shape[2] * N_CHIPS // (HEAD + ROPE)
    NAW = w_o_na.shape[1]
    NH = NAW // HEAD
    rows = S // GRID_W
    x = x.reshape(S, D)
    target = loss_target.reshape(S, D)

    kind_of = {n: "slot" if n == "w_in" else ("row" if n in ROW_SHARDED else "col") for n in BIG}
    shape_of = {n: tuple(wts[n].shape[1:]) for n in BIG}

    def gather_of(items):
        return _gather_comm([wts[n][layer].astype(BF16) for n, layer in items], [kind_of[n] for n, _ in items])

    widths = (QL, KL, ROPE, NAW, NAW, NAW, D, D)
    starts = np.concatenate([[0], np.cumsum(widths)]).astype(int)
    order = (6, 7, 3, 4, 5, 0, 1, 2)
    nloc = w_in.shape[2]
    new_off = np.concatenate([[0], np.cumsum([widths[i] for i in order])]).astype(int)
    off_ga, off_gb, off_q, off_k, off_v, off_cq, off_ckv, off_kpe = (int(o) for o in new_off[:8])
    PW = int(new_off[-1]) + 128 - ROPE

    def prepared(full):
        pieces = []
        for i in order:
            for k in range(N_CHIPS):
                lo, hi = max(int(starts[i]), k * nloc), min(int(starts[i + 1]), (k + 1) * nloc)
                if lo < hi:
                    pieces.append(full["w_in"][k, :, lo - k * nloc:hi - k * nloc])
        full["w_in"] = jnp.concatenate(pieces + [jnp.zeros((D, 128 - ROPE), BF16)], axis=1)
        full["w_uq"] = jnp.pad(full["w_uq"].reshape(QL, H, HEAD + ROPE),
                               ((0, 0), (0, 0), (0, 256 - HEAD - ROPE))).reshape(QL, H * 256)
        return full

    pos = jnp.arange(S, dtype=F32)
    inv_freq = 1.0 / (ROPE_THETA ** (jnp.arange(0, ROPE, 2, dtype=F32) / ROPE))
    ang = pos[:, None] * inv_freq[None, :]
    cos, sin, zero = jnp.cos(ang), jnp.sin(ang), jnp.zeros((S, 128 - ROPE), F32)
    cos_t = jnp.concatenate([cos, cos, zero], axis=1)
    sin_t = jnp.concatenate([-sin, sin, zero], axis=1)

    dy_idx, dx_idx, bias_ok = _na_bias_index(rows)
    bblk = _na_block_bias(_na_bias(wts["rpb"], jnp.asarray(np.where(bias_ok[0], dx_idx[0], -1), jnp.int32)))
    mla_scale = float((HEAD + ROPE) ** -0.5)
    na_scale = float(HEAD ** -0.5)

    def rope_q_epilogue(acc, extra, outs):
        cv, sv = extra[0][...] * mla_scale, extra[1][...] * mla_scale
        for hh in range(acc.shape[1] // 256):
            outs[0][:, hh * 256:hh * 256 + 128] = (acc[:, hh * 256:hh * 256 + 128] * mla_scale).astype(BF16)
            outs[0][:, hh * 256 + 128:(hh + 1) * 256] = _rope128(acc[:, hh * 256 + 128:(hh + 1) * 256], cv, sv).astype(BF16)

    def store_f32(acc, extra, outs):
        outs[0][...] = acc

    def merge_epilogue(acc, extra, outs):
        ga, gb, ya = extra[0][...].astype(F32), extra[1][...].astype(F32), extra[2][...]
        outs[0][...] = (jax.nn.sigmoid(ga) * ya + jax.nn.sigmoid(gb) * acc).astype(BF16)
        outs[1][...] = acc

    def residual_epilogue(acc, extra, outs):
        outs[0][...] = extra[0][...] + acc

    def ff1_epilogue(acc, extra, outs):
        outs[0][...] = acc.astype(BF16)
        outs[1][...] = jnp.square(jnp.maximum(acc, 0.0)).astype(BF16)

    def dff_epilogue(acc, extra, outs):
        outs[0][...] = (acc * (2.0 * jnp.maximum(extra[0][...].astype(F32), 0.0))).astype(BF16)

    def dmerge_epilogue(acc, extra, outs):
        ga, gb = extra[0][...].astype(F32), extra[1][...].astype(F32)
        ya, yb = extra[2][...], extra[3][...]
        sa, sb = jax.nn.sigmoid(ga), jax.nn.sigmoid(gb)
        outs[0][...] = (acc * sa).astype(BF16)
        outs[1][...] = (acc * sb).astype(BF16)
        outs[2][...] = (acc * ya * sa * (1.0 - sa)).astype(BF16)
        outs[3][...] = (acc * yb * sb * (1.0 - sb)).astype(BF16)

    saved = []
    items = [(n, 0) for n in EARLY]
    w = prepared(dict(zip(EARLY, _run_comm("weight_gather_0", gather_of(items)))))
    for l in range(L):
        u, r1 = _rms_fwd(f"rms_mix_{l}", x, wts["norm_mix"][l][None])
        proj, = _mm(f"proj_{l}", u, w["w_in"], "nn", [_sds((S, PW), BF16)], tn_cap=2048, tk_cap=2048)
        nq, nkv, rq, rkv, kp = _lat_fwd(l, proj, wts["norm_qa"][l][None], wts["norm_kva"][l][None], cos_t, sin_t,
                                        off_cq, off_ckv, off_kpe)
        q, = _mm(f"q_up_{l}", nq, w["w_uq"], "nn", [_sds((S, H * 256), BF16)],
                 extras=(cos_t, sin_t), extra_specs=(_row_spec(128), _row_spec(128)), epilogue=rope_q_epilogue, tn_cap=512)
        kv, = _mm(f"kv_up_{l}", nkv, w["w_ukv"], "nn", [_sds((S, H * 256), BF16)])
        items = [(n, l) for n in LATE] + ([(n, l + 1) for n in EARLY] if l + 1 < L else [])
        o_a, lse, *gathered = _mla_fwd(l, q, kv, kp, gather_of(items))
        w.update(zip(LATE, gathered[:len(LATE)]))
        w_next = prepared(dict(zip(EARLY, gathered[len(LATE):]))) if l + 1 < L else None
        o_b = _na_fwd(l, proj, bblk, off_q, off_k, off_v, na_scale)
        y_a, = _mm(f"o_mla_{l}", o_a, w["w_o_mla"], "nn", [_sds((S, D), F32)], epilogue=store_f32, tn_cap=512)
        merged, y_b = _mm(f"o_na_merge_{l}", o_b, w["w_o_na"], "nn", [_sds((S, D), BF16), _sds((S, D), F32)],
                          extras=(proj, proj, y_a), extra_specs=(_col_spec(off_ga), _col_spec(off_gb), _tile_spec),
                          epilogue=merge_epilogue, tn_cap=512)
        x2, = _mm(f"w_out_{l}", merged, w["w_out"], "nn", [_sds((S, D), F32)],
                  extras=(x,), extra_specs=(_tile_spec,), epilogue=residual_epilogue, tk_cap=2048)
        u2, r2 = _rms_fwd(f"rms_mlp_{l}", x2, wts["norm_mlp"][l][None])
        h, a = _mm(f"ff1_{l}", u2, w["w_ff1"], "nn", [_sds((S, 4 * D), BF16), _sds((S, 4 * D), BF16)],
                   epilogue=ff1_epilogue, tk_cap=2048)
        x3, = _mm(f"ff2_{l}", a, w["w_ff2"], "nn", [_sds((S, D), F32)],
                  extras=(x2,), extra_specs=(_tile_spec,), epilogue=residual_epilogue, tk_cap=2048)
        saved.append(dict(w=w, x=x, r1=r1, u=u, proj=proj, nq=nq, nkv=nkv, rq=rq, rkv=rkv, q=q, kv=kv, kp=kp, o_a=o_a,
                          lse=lse, o_b=o_b, y_a=y_a, y_b=y_b, merged=merged, x2=x2, r2=r2, u2=u2, h=h, a=a))
        x, w = x3, w_next

    loss_lanes, dx, dxb, dg_final = _final_loss(x, target, wts["norm_final"][None])
    loss = lax.psum(loss_lanes[0, 0], ("x", "y", "c"))

    gsmall = {n: [None] * L for n in SMALL if n != "norm_final"}
    oh_dy = jnp.asarray(dy_idx[:, 0, :, None] == np.arange(2 * NA_KH - 1), F32)
    oh_dx = jnp.asarray((dx_idx[0, :, :, None] == np.arange(2 * NA_KW - 1)) & bias_ok[0, :, :, None], F32)
    scattered = {}

    def scatter_of(items, grads):
        names = [n for n, _ in items]
        return _scatter_comm(grads, [kind_of[n] for n in names], [shape_of[n] for n in names])

    def record(items, landed):
        for j, item in enumerate(items):
            scattered[item] = (landed[2 * j], landed[2 * j + 1])

    pending = []
    for l in reversed(range(L)):
        sv = saved[l]
        proj, w, g = sv["proj"], sv["w"], {}
        dh, = _mm(f"d_ff2_{l}", dxb, w["w_ff2"], "nt", [_sds((S, 4 * D), BF16)],
                  extras=(sv["h"],), extra_specs=(_tile_spec,), epilogue=dff_epilogue, tk_cap=2048)
        g["w_ff2"], = _mm(f"g_ff2_{l}", sv["a"], dxb, "tn", [_sds((4 * D, D), BF16)], tn_cap=2048)
        du2, = _mm(f"d_ff1_{l}", dh, w["w_ff1"], "nt", [_sds((S, D), F32)], epilogue=store_f32, tk_cap=2048)
        g["w_ff1"], = _mm(f"g_ff1_{l}", sv["u2"], dh, "tn", [_sds((D, 4 * D), BF16)], tn_cap=2048)
        dx2, dx2b, gsmall["norm_mlp"][l] = _rms_bwd(f"rms_mlp_bwd_{l}", sv["x2"], sv["r2"], wts["norm_mlp"][l][None], du2, dx)
        dya, dyb, dga, dgb = _mm(
            f"d_w_out_{l}", dx2b, w["w_out"], "nt", [_sds((S, D), BF16)] * 4,
            extras=(proj, proj, sv["y_a"], sv["y_b"]),
            extra_specs=(_col_spec(off_ga), _col_spec(off_gb), _tile_spec, _tile_spec),
            epilogue=dmerge_epilogue, tn_cap=512, tk_cap=2048)
        g["w_out"], = _mm(f"g_w_out_{l}", sv["merged"], dx2b, "tn", [_sds((D, D), BF16)], tn_cap=2048)
        do_a, = _mm(f"d_o_mla_{l}", dya, w["w_o_mla"], "nt", [_sds((S, H * HEAD), BF16)], tk_cap=2048)
        g["w_o_mla"], = _mm(f"g_o_mla_{l}", sv["o_a"], dya, "tn", [_sds((H * HEAD, D), BF16)], tn_cap=2048)
        do_b, = _mm(f"d_o_na_{l}", dyb, w["w_o_na"], "nt", [_sds((S, NAW), BF16)], tk_cap=2048)
        g["w_o_na"], = _mm(f"g_o_na_{l}", sv["o_b"], dyb, "tn", [_sds((NAW, D), BF16)], tn_cap=2048)
        dq_na, dk_na, dv_na, dbblk = _na_bwd(l, proj, bblk, sv["o_b"], do_b, off_q, off_k, off_v, na_scale)
        dbrow = _na_unblock(dbblk)
        tmp =jnp.einsum("hoqn,qnx->honx", dbrow, oh_dx, precision=lax.Precision.HIGHEST)
        gsmall["rpb"][l] = jnp.einsum("honx,ony->hyx", tmp, oh_dy, precision=lax.Precision.HIGHEST)
        dl = _delta(l, sv["o_a"], do_a)
        pending += [(n, l, g[n]) for n in LATE]
        items = [(n, layer) for n, layer, _ in pending]
        dq_f, dkv, dkp_h, *landed = _mla_bwd(l, sv["q"], sv["kv"], sv["kp"], do_a, sv["lse"], dl,
                                             scatter_of(items, [arr for _, _, arr in pending]))
        record(items, landed)
        dq =_rope_bwd_q(l, dq_f, cos_t, sin_t, mla_scale)
        dnq, = _mm(f"d_q_up_{l}", dq, w["w_uq"], "nt", [_sds((S, QL), F32)], epilogue=store_f32, tk_cap=2048)
        g_uq, = _mm(f"g_q_up_{l}", sv["nq"], dq, "tn", [_sds((QL, H * 256), BF16)], tn_cap=2048)
        g["w_uq"] = g_uq.reshape(QL, H, 256)[:, :, :HEAD + ROPE].reshape(QL, H * (HEAD + ROPE))
        dnkv, = _mm(f"d_kv_up_{l}", dkv, w["w_ukv"], "nt", [_sds((S, KL), F32)], epilogue=store_f32, tk_cap=2048)
        g["w_ukv"], = _mm(f"g_kv_up_{l}", sv["nkv"], dkv, "tn", [_sds((KL, H * 256), BF16)], tn_cap=2048)
        dcq, dckv, dkpe, gsmall["norm_qa"][l], gsmall["norm_kva"][l] = _lat_bwd(
            l, proj, sv["rq"], sv["rkv"], wts["norm_qa"][l][None], wts["norm_kva"][l][None], dnq, dnkv, dkp_h,
            cos_t, sin_t, off_cq, off_ckv)
        dproj = jnp.concatenate([dga, dgb, dq_na, dk_na.astype(BF16), dv_na.astype(BF16), dcq, dckv, dkpe], axis=1)
        du, = _mm(f"d_proj_{l}", dproj, w["w_in"], "nt", [_sds((S, D), F32)], epilogue=store_f32, tk_cap=2048)
        g_in, = _mm(f"g_proj_{l}", sv["u"], dproj, "tn", [_sds((D, PW), BF16)], tn_cap=2048)
        back = [None] * 8
        for pos_new, i in enumerate(order):
            back[i] = g_in[:, new_off[pos_new]:new_off[pos_new] + widths[i]]
        g_orig = jnp.concatenate(back, axis=1)
        g["w_in"] = jnp.stack([g_orig[:, k * nloc:(k + 1) * nloc] for k in range(N_CHIPS)])
        pending = [(n, l, g[n]) for n in EARLY]
        dx, dxb, gsmall["norm_mix"][l] = _rms_bwd(f"rms_mix_bwd_{l}", sv["x"], sv["r1"], wts["norm_mix"][l][None], du, dx2)
    grad_x = dx.reshape(1, S, D)
    items = [(n, layer) for n, layer, _ in pending]
    record(items, _run_comm("grad_scatter_0", scatter_of(items, [arr for _, _, arr in pending])))

    parts = []
    for n in BIG:
        stack = lax.empty((L,) + shape_of[n], F32)
        for l in range(L):
            stack = _grad_partial(f"grad_partial_{n}_{l}", *scattered[(n, l)], stack, l)
        parts.append(stack.reshape(L * shape_of[n][0], shape_of[n][1]))
    others = _sibling_swap(parts)
    part_w, other_w = dict(zip(BIG, parts)), dict(zip(BIG, others))
    two = lambda t: t.reshape(t.shape[0] * t.shape[1], t.shape[2])

    small_shapes = [wts[n].shape for n in SMALL]
    small_g = [jnp.stack([g.reshape(wts[n].shape[1:]) for g in gsmall[n]]) for n in SMALL if n != "norm_final"]
    small_g.append(dg_final.reshape(D))
    gsum = _small_allreduce(_pack(small_g))
    zeros = jnp.zeros_like(gsum)
    sg, sd, sm, svv = _adamw("adamw_small", gsum, zeros, _pack([wts[n] for n in SMALL]), _pack([mom[n] for n in SMALL]),
                             _pack([var[n] for n in SMALL]))
    res = {n: {} for n in WEIGHTS}
    for key, flat in (("g", sg), ("d", sd), ("m", sm), ("v", svv)):
        for n, arr in zip(SMALL, _unpack(flat, small_shapes)):
            res[n][key] = arr
    for n in BIG:
        shp = wts[n].shape
        outs = _adamw(f"adamw_{n}", part_w[n], other_w[n], two(wts[n]), two(mom[n]), two(var[n]))
        for key, arr in zip(("g", "d", "m", "v"), outs):
            res[n][key] = arr.reshape(shp)

    return (loss, grad_x, *[res[n]["g"] for n in WEIGHTS], *[res[n]["d"] for n in WEIGHTS],
            *[res[n]["m"] for n in WEIGHTS], *[res[n]["v"] for n in WEIGHTS])
```
